```python
import math
import jax, jax.numpy as jnp
from jax import lax
import numpy as np

D_MODEL = 1024
BATCH = 4
SEQ = 8192
DEPTH = 2

D_MIX = D_MODEL
N_MIXERS = 4
D_BRANCH = D_MIX // N_MIXERS
HEAD_DIM = 64
N_HEADS_BRANCH = D_BRANCH // HEAD_DIM

SSM_STATE = 64
SSM_GROUPS = 2
SSM_CONV = 4
SSM_CHUNK = 128
SSM_XBC = D_BRANCH + 2 * SSM_GROUPS * SSM_STATE
HGRN_CHUNK = 64
RET_CHUNK = 128
MOBA_BLOCK = 256
MOBA_TOPK = 3
MOBA_QBLOCK = 128

LN_EPS = 1e-5
RMS_EPS = 1e-6
DEEPNORM_ALPHA = (2.0 * DEPTH) ** 0.25
DEEPNORM_BETA = (8.0 * DEPTH) ** -0.25

SPLIT_SIZES = (D_BRANCH, SSM_XBC, N_HEADS_BRANCH,
               D_BRANCH, D_BRANCH, D_BRANCH, D_BRANCH,
               D_BRANCH, D_BRANCH, D_BRANCH, D_BRANCH,
               D_BRANCH, D_BRANCH, D_BRANCH, D_BRANCH)
N_IN = sum(SPLIT_SIZES)

kernel_name = "hymba_ssd_hgrn2_retnet_moba_deepnorm"

F32 = jnp.float32


def _layernorm(x, g, b):
    xf = x.astype(F32)
    mu = jnp.mean(xf, -1, keepdims=True)
    var = jnp.mean(jnp.square(xf - mu), -1, keepdims=True)
    return ((xf - mu) * lax.rsqrt(var + LN_EPS) * g + b).astype(x.dtype)


def _rms(x):
    xf = x.astype(F32)
    return xf * lax.rsqrt(jnp.mean(xf * xf, -1, keepdims=True) + RMS_EPS)


def _groupnorm(x):
    xf = x.astype(F32)
    mu = jnp.mean(xf, -1, keepdims=True)
    var = jnp.mean(jnp.square(xf - mu), -1, keepdims=True)
    return (xf - mu) * lax.rsqrt(var + LN_EPS)


def _segsum(a):
    T = a.shape[-1]
    rep = jnp.broadcast_to(a[..., :, None], a.shape + (T,))
    idx = jnp.arange(T)
    rep = jnp.where(idx[:, None] > idx[None, :], rep, 0.0)
    cs = jnp.cumsum(rep, axis=-2)
    return jnp.where(idx[:, None] >= idx[None, :], cs, -jnp.inf)


def _ssd(z, xbc_raw, dt_raw, conv_w, conv_b, dt_bias, a_log, d_skip, norm_w):
    Bsz, L, ch = xbc_raw.shape
    H, P, G, N, C = N_HEADS_BRANCH, HEAD_DIM, SSM_GROUPS, SSM_STATE, SSM_CHUNK
    nc = L // C
    xbc = lax.conv_general_dilated(xbc_raw, conv_w[:, None, :], window_strides=(1,),
                                   padding=((SSM_CONV - 1, 0),),
                                   dimension_numbers=('NWC', 'WIO', 'NWC'),
                                   feature_group_count=ch) + conv_b
    xbc = jax.nn.silu(xbc)
    xs, Bm, Cm = jnp.split(xbc, [D_BRANCH, D_BRANCH + G * N], axis=-1)
    xs = xs.reshape(Bsz, nc, C, H, P)
    Bm = jnp.repeat(Bm.reshape(Bsz, nc, C, G, N), H // G, axis=3)
    Cm = jnp.repeat(Cm.reshape(Bsz, nc, C, G, N), H // G, axis=3)
    dt = jax.nn.softplus(dt_raw.astype(F32) + dt_bias)
    A = -jnp.exp(a_log.astype(F32))
    dt4 = dt.reshape(Bsz, nc, C, H)
    a = (dt4 * A).transpose(0, 3, 1, 2)
    a_cum = jnp.cumsum(a, axis=-1)
    xdt = xs * dt4[..., None]
    Lmat = jnp.exp(_segsum(a))
    scores = jnp.einsum('bclhn,bcshn->bhcls', Cm, Bm) * Lmat
    y_diag = jnp.einsum('bhcls,bcshp->bclhp', scores, xdt)
    decay_states = jnp.exp(a_cum[..., -1:] - a_cum)
    states = jnp.einsum('bclhn,bhcl,bclhp->bchpn', Bm, decay_states, xdt)
    states = jnp.concatenate([jnp.zeros_like(states[:, :1]), states], axis=1)
    chunk_decay = jnp.exp(_segsum(jnp.pad(a_cum[..., -1], ((0, 0), (0, 0), (1, 0)))))
    new_states = jnp.einsum('bhzc,bchpn->bzhpn', chunk_decay, states)
    prev_states = new_states[:, :-1]
    y_off = jnp.einsum('bclhn,bchpn,bhcl->bclhp', Cm, prev_states, jnp.exp(a_cum))
    y = (y_diag + y_off + xs * d_skip[:, None]).reshape(Bsz, L, D_BRANCH)
    g = (y * jax.nn.silu(z.astype(F32))).reshape(Bsz, L, G, D_BRANCH // G)
    return _rms(g).reshape(Bsz, L, D_BRANCH) * norm_w


def _hgrn2(q, f_raw, i, gate, lb, norm_w):
    Bsz, L, _ = q.shape
    H, dk, C = N_HEADS_BRANCH, HEAD_DIM, HGRN_CHUNK
    nc = L // C
    lbf = lb.astype(F32)
    f = lbf + (1.0 - lbf) * jax.nn.sigmoid(f_raw.astype(F32))
    logf = jnp.log(f)
    k = 1.0 - f

    def chunks(t):
        return t.astype(F32).reshape(Bsz, nc, C, H, dk).transpose(1, 0, 3, 2, 4)

    qs, ks, vs, ls = chunks(q), chunks(k), chunks(i), chunks(logf)
    causal = jnp.tril(jnp.ones((C, C), dtype=bool))

    def step(S, inp):
        qc, kc, vc, lc = inp
        b = jnp.cumsum(lc, axis=2)
        o_inter = jnp.einsum('bhtd,bhdv->bhtv', qc * jnp.exp(b), S)
        diff = b[:, :, :, None, :] - b[:, :, None, :, :]
        diff = jnp.where(causal[None, None, :, :, None], diff, -jnp.inf)
        att = jnp.einsum('bhtd,bhsd,bhtsd->bhts', qc, kc, jnp.exp(diff))
        o_intra = jnp.einsum('bhts,bhsv->bhtv', att, vc)
        b_last = b[:, :, -1]
        S_new = jnp.exp(b_last)[..., None] * S + jnp.einsum(
            'bhsd,bhsv->bhdv', kc * jnp.exp(b_last[:, :, None] - b), vc)
        return S_new, o_inter + o_intra

    S0 = jnp.zeros((Bsz, H, dk, dk), F32)
    _, o = lax.scan(step, S0, (qs, ks, vs, ls))
    o = o.transpose(1, 0, 3, 2, 4).reshape(Bsz, L, H, dk)
    o = (_rms(o) * norm_w.reshape(H, dk)).reshape(Bsz, L, D_BRANCH)
    return o * jax.nn.silu(gate.astype(F32))


def _retention(q, k, v, gate):
    Bsz, L, _ = q.shape
    H, d, C = N_HEADS_BRANCH, HEAD_DIM, RET_CHUNK
    nc = L // C
    log_g = jnp.log(1.0 - 2.0 ** (-5.0 - jnp.arange(H, dtype=F32)))
    qh = q.astype(F32).reshape(Bsz, nc, C, H, d)
    kh = k.astype(F32).reshape(Bsz, nc, C, H, d) * (d ** -0.5)
    vh = v.astype(F32).reshape(Bsz, nc, C, H, d)
    pos = jnp.arange(C, dtype=F32)
    dist = pos[:, None] - pos[None, :]
    intra_decay = jnp.where(dist >= 0, jnp.exp(log_g[:, None, None] * jnp.maximum(dist, 0.0)), 0.0)
    scores = jnp.einsum('bclhd,bcshd->bchls', qh, kh) * intra_decay
    y_intra = jnp.einsum('bchls,bcshv->bclhv', scores, vh)
    k_decay = jnp.exp(log_g[:, None] * (C - 1.0 - pos)[None, :])
    U = jnp.einsum('bcshd,hs,bcshv->bchdv', kh, k_decay, vh)
    chunk_g = jnp.exp(log_g * C)[None, :, None, None]

    def step(R, u):
        return chunk_g * R + u, R

    _, R_prev = lax.scan(step, jnp.zeros((Bsz, H, d, d), F32), U.transpose(1, 0, 2, 3, 4))
    R_prev = R_prev.transpose(1, 0, 2, 3, 4)
    q_decay = jnp.exp(log_g[:, None] * (pos + 1.0)[None, :])
    y_inter = jnp.einsum('bclhd,hl,bchdv->bclhv', qh, q_decay, R_prev)
    y = _groupnorm((y_intra + y_inter).reshape(Bsz, L, H, d)).reshape(Bsz, L, D_BRANCH)
    return y * jax.nn.silu(gate.astype(F32))


def _moba(q, k, v, gate):
    Bsz, L, _ = q.shape
    H, d, S, Q = N_HEADS_BRANCH, HEAD_DIM, MOBA_BLOCK, MOBA_QBLOCK
    nb = -(-L // S)
    pad = nb * S - L

    def heads(t):
        t = jnp.pad(t, ((0, 0), (0, pad), (0, 0)))
        return t.reshape(Bsz, nb * S, H, d).transpose(0, 2, 1, 3)

    qh = heads(q) * (d ** -0.5)
    kh, vh = heads(k), heads(v)
    k_blocks = kh.reshape(Bsz, H, nb, S, d)
    v_blocks = vh.reshape(Bsz, H, nb, S, d)
    k_mean = jnp.mean(k_blocks, axis=3)
    n_sel = min(MOBA_TOPK, nb)
    slopes = 2.0 ** (-8.0 * jnp.arange(1, H + 1, dtype=F32) / H)
    bi = jnp.arange(Bsz)[:, None, None, None]
    hi = jnp.arange(H)[None, :, None, None]
    qb_per_kb = S // Q

    def one_block(qb):
        q0 = qb * Q
        own = qb // qb_per_kb
        qc = lax.dynamic_slice_in_dim(qh, q0, Q, axis=2)
        t_pos = q0 + jnp.arange(Q)
        gsc = jnp.einsum('bhqd,bhnd->bhqn', qc, k_mean).astype(F32)
        gsc = jnp.where(jnp.arange(nb) < own, gsc, -jnp.inf)
        _, sel = lax.top_k(gsc, n_sel)
        valid = jnp.arange(n_sel) < own
        ks = k_blocks[bi, hi, sel]
        vs = v_blocks[bi, hi, sel]
        s_sel = jnp.einsum('bhqd,bhqnsd->bhqns', qc, ks).astype(F32)
        s_pos = sel[..., None] * S + jnp.arange(S)
        rel_sel = (t_pos[:, None, None] - s_pos).astype(F32)
        s_sel = s_sel - slopes[:, None, None, None] * rel_sel
        s_sel = jnp.where(valid[:, None], s_sel, -jnp.inf)
        k_own = lax.dynamic_slice_in_dim(kh, own * S, S, axis=2)
        v_own = lax.dynamic_slice_in_dim(vh, own * S, S, axis=2)
        rel = (t_pos[:, None] - (own * S + jnp.arange(S))[None, :]).astype(F32)
        s_own = jnp.einsum('bhqd,bhsd->bhqs', qc, k_own).astype(F32)
        s_own = jnp.where(rel >= 0, s_own - slopes[:, None, None] * rel, -jnp.inf)
        scores = jnp.concatenate([s_sel.reshape(Bsz, H, Q, n_sel * S), s_own], axis=-1)
        p = jax.nn.softmax(scores, axis=-1)
        p_sel = p[..., :n_sel * S].reshape(Bsz, H, Q, n_sel, S)
        p_own = p[..., n_sel * S:]
        return (jnp.einsum('bhqns,bhqnsd->bhqd', p_sel, vs)
                + jnp.einsum('bhqs,bhsd->bhqd', p_own, v_own))

    outs = lax.map(one_block, jnp.arange(L // Q))
    o = outs.transpose(1, 0, 3, 2, 4).reshape(Bsz, L, D_BRANCH)
    return o * jax.nn.silu(gate.astype(F32))


def _layer(x, w_in, conv_w, conv_b, dt_bias, a_log, d_skip, ssd_norm_w, lb, hgrn_norm_w,
           w_out, ln_g, ln_b):
    proj = jnp.einsum('bld,dn->bln', x, w_in)
    idx = [int(v) for v in np.cumsum(SPLIT_SIZES)[:-1]]
    (s_z, s_xbc, s_dt, h_q, h_f, h_i, h_g, r_q, r_k, r_v, r_g,
     m_q, m_k, m_v, m_g) = jnp.split(proj, idx, axis=-1)
    y_ssd = _ssd(s_z, s_xbc, s_dt, conv_w, conv_b, dt_bias, a_log, d_skip, ssd_norm_w)
    y_hgrn = _hgrn2(h_q, h_f, h_i, h_g, lb, hgrn_norm_w)
    y_ret = _retention(r_q, r_k, r_v, r_g)
    y_moba = _moba(m_q, m_k, m_v, m_g)
    y_cat = jnp.concatenate([y_ssd, y_hgrn, y_ret, y_moba], axis=-1).astype(x.dtype)
    y = jnp.einsum('blm,md->bld', y_cat, w_out)
    return _layernorm(DEEPNORM_ALPHA * x + y, ln_g, ln_b)


def setup_inputs(seed: int = 0) -> dict:
    key = jax.random.key(seed)
    ks = jax.random.split(key, 16)
    offs = np.cumsum((0,) + SPLIT_SIZES)
    col_scale = np.ones((N_IN,), np.float32)
    col_scale[offs[1]:offs[1] + D_BRANCH] = DEEPNORM_BETA
    col_scale[offs[5]:offs[6]] = DEEPNORM_BETA
    col_scale[offs[9]:offs[10]] = DEEPNORM_BETA
    col_scale[offs[13]:offs[14]] = DEEPNORM_BETA
    w_in = (jax.random.normal(ks[3], (DEPTH, D_MODEL, N_IN), F32) * (D_MODEL ** -0.5)
            * jnp.asarray(col_scale))
    dt0 = jnp.exp(jax.random.uniform(ks[6], (DEPTH, N_HEADS_BRANCH), F32,
                                     math.log(1e-3), math.log(1e-1)))
    dt_bias = dt0 + jnp.log(-jnp.expm1(-dt0))
    a_log = jnp.log(jax.random.uniform(ks[7], (DEPTH, N_HEADS_BRANCH), F32, 1.0, 16.0))
    return {
        "x": jax.random.normal(ks[0], (BATCH, SEQ, D_MODEL), F32),
        "emb_ln_g": 1.0 + 0.02 * jax.random.normal(ks[1], (D_MODEL,), F32),
        "emb_ln_b": 0.02 * jax.random.normal(ks[2], (D_MODEL,), F32),
        "w_in": w_in,
        "ssd_conv_w": jax.random.normal(ks[4], (DEPTH, SSM_CONV, SSM_XBC), F32) * (SSM_CONV ** -0.5),
        "ssd_conv_b": 0.02 * jax.random.normal(ks[5], (DEPTH, SSM_XBC), F32),
        "ssd_dt_bias": dt_bias,
        "ssd_a_log": a_log,
        "ssd_d": 1.0 + 0.02 * jax.random.normal(ks[8], (DEPTH, N_HEADS_BRANCH), F32),
        "ssd_norm_w": 1.0 + 0.02 * jax.random.normal(ks[9], (DEPTH, D_BRANCH), F32),
        "hgrn_lb_logits": 0.1 * jax.random.normal(ks[10], (DEPTH, D_BRANCH), F32),
        "hgrn_norm_w": 1.0 + 0.02 * jax.random.normal(ks[11], (DEPTH, D_BRANCH), F32),
        "w_out": jax.random.normal(ks[12], (DEPTH, D_MIX, D_MODEL), F32) * (D_MIX ** -0.5) * DEEPNORM_BETA,
        "ln_g": 1.0 + 0.02 * jax.random.normal(ks[13], (DEPTH, D_MODEL), F32),
        "ln_b": 0.02 * jax.random.normal(ks[14], (DEPTH, D_MODEL), F32),
    }


def reference(x, emb_ln_g, emb_ln_b, w_in, ssd_conv_w, ssd_conv_b, ssd_dt_bias, ssd_a_log,
              ssd_d, ssd_norm_w, hgrn_lb_logits, hgrn_norm_w, w_out, ln_g, ln_b):
    lbs = jnp.cumsum(jax.nn.softmax(hgrn_lb_logits.astype(F32), axis=0), axis=0)
    lbs = lbs - lbs[0]
    h = _layernorm(x, emb_ln_g, emb_ln_b)
    for l in range(DEPTH):
        h = _layer(h, w_in[l], ssd_conv_w[l], ssd_conv_b[l], ssd_dt_bias[l], ssd_a_log[l],
                   ssd_d[l], ssd_norm_w[l], lbs[l], hgrn_norm_w[l], w_out[l], ln_g[l], ln_b[l])
    return h
```

```python
import functools
import math

import numpy as np
import jax
import jax.numpy as jnp
from jax import lax
from jax.experimental import pallas as pl
from jax.experimental.pallas import tpu as pltpu

F32 = jnp.float32
BF16 = jnp.bfloat16

D_BRANCH = 256
HEAD_DIM = 64
N_HEADS = 4
SSM_STATE = 64
SSM_GROUPS = 2
SSM_CONV = 4
SSM_XBC = D_BRANCH + 2 * SSM_GROUPS * SSM_STATE
MOBA_BLOCK = 256
MOBA_TOPK = 3
LN_EPS = 1e-5
RMS_EPS = 1e-6
NEG_BIG = -1e30

LANES = 128
SUBLANES = 8
VMEM_BYTES_V7X = 64 * 1024 * 1024

SSD_CHUNK = 128
RET_CHUNK = 128
HGRN_CHUNK = 64
MIX_ROWS = 256

N_PROJ = 4096
COL_Z = SSM_XBC // D_BRANCH
COL_HGRN = COL_Z + 1
COL_RET = COL_HGRN + 4
COL_MOBA = COL_RET + 4
COL_DT = (COL_MOBA + 4) * D_BRANCH // LANES


def _dot(a, b):
    return jnp.dot(a, b, preferred_element_type=F32)


def _dot_nt(a, b):
    return lax.dot_general(a, b, (((1,), (1,)), ((), ())), preferred_element_type=F32)


def _dot_tn(a, b):
    return lax.dot_general(a, b, (((0,), (0,)), ((), ())), preferred_element_type=F32)


def _split3(x):
    hi = x.astype(BF16)
    r1 = x - hi.astype(F32)
    mid = r1.astype(BF16)
    lo = (r1 - mid.astype(F32)).astype(BF16)
    return hi, mid, lo


def _sel_dot(sel, x):
    hi, mid, lo = _split3(x)
    return _dot(sel, hi) + _dot(sel, mid) + _dot(sel, lo)


def _dot_sel(x, sel):
    hi, mid, lo = _split3(x)
    return _dot(hi, sel) + _dot(mid, sel) + _dot(lo, sel)


def _sigmoid(x):
    return 1.0 / (1.0 + jnp.exp(-x))


def _silu(x):
    return x * _sigmoid(x)


def _softplus(x):
    return jnp.maximum(x, 0.0) + jnp.log(1.0 + jnp.exp(-jnp.abs(x)))


def _row_stack(x, masks):
    return jnp.concatenate([x * masks[h:h + 1, :] for h in range(masks.shape[0])], axis=0)


def _params(n_axes, vmem_bytes):
    return pltpu.CompilerParams(dimension_semantics=("arbitrary",) * n_axes,
                                vmem_limit_bytes=int(vmem_bytes))


def _ln_rows(x, g, b):
    mu = jnp.mean(x, axis=-1, keepdims=True)
    xc = x - mu
    var = jnp.mean(xc * xc, axis=-1, keepdims=True)
    return xc * lax.rsqrt(var + LN_EPS) * g + b


def _ln_kernel(x_ref, g_ref, b_ref, o_ref):
    o_ref[...] = _ln_rows(x_ref[...], g_ref[...], b_ref[...])


def _layernorm(x2, g, b, tm=512):
    m, d = x2.shape
    return pl.pallas_call(
        _ln_kernel,
        grid=(m // tm,),
        in_specs=[pl.BlockSpec((tm, d), lambda i: (i, 0)),
                  pl.BlockSpec((1, d), lambda i: (0, 0)),
                  pl.BlockSpec((1, d), lambda i: (0, 0))],
        out_specs=pl.BlockSpec((tm, d), lambda i: (i, 0)),
        out_shape=jax.ShapeDtypeStruct((m, d), F32),
        compiler_params=_params(1, 4 * tm * d * 4 + (4 << 20)),
        name="embed_layernorm",
    )(x2, g.reshape(1, d), b.reshape(1, d))


def _inproj_kernel(x_ref, w_ref, o_ref, xb_ref):
    @pl.when(pl.program_id(1) == 0)
    def _():
        xb_ref[...] = x_ref[...].astype(BF16)

    o_ref[...] = _dot(xb_ref[...], w_ref[...])


def _inproj(h2, w_bf16, tm=1024, tn=512):
    m, d = h2.shape
    n = w_bf16.shape[1]
    vmem = 2 * tm * d * 4 + 2 * d * tn * 2 + 2 * tm * tn * 4 + tm * d * 2 + (4 << 20)
    return pl.pallas_call(
        _inproj_kernel,
        grid=(m // tm, n // tn),
        in_specs=[pl.BlockSpec((tm, d), lambda i, j: (i, 0)),
                  pl.BlockSpec((d, tn), lambda i, j: (0, j))],
        out_specs=pl.BlockSpec((tm, tn), lambda i, j: (i, j)),
        out_shape=jax.ShapeDtypeStruct((m, n), F32),
        scratch_shapes=[pltpu.VMEM((tm, d), BF16)],
        compiler_params=_params(2, vmem),
        name="in_projection",
    )(h2, w_bf16)


def _outproj_kernel(h_ref, y0_ref, y1_ref, y2_ref, y3_ref, w_ref, g_ref, b_ref, o_ref, *, alpha):
    ycat = jnp.concatenate([y0_ref[...], y1_ref[...], y2_ref[...], y3_ref[...]], axis=1)
    y = _dot(ycat.astype(BF16), w_ref[...])
    o_ref[...] = _ln_rows(alpha * h_ref[...] + y, g_ref[...], b_ref[...])


def _outproj(h2, ys, w_bf16, g, b, alpha, tm=512):
    m, d = h2.shape
    dm = w_bf16.shape[0]
    yspec = pl.BlockSpec((tm, D_BRANCH), lambda i: (i, 0))
    vmem = 2 * (2 * tm * d * 4 + tm * dm * 4 + dm * d * 2) + tm * d * 8 + (4 << 20)
    return pl.pallas_call(
        functools.partial(_outproj_kernel, alpha=alpha),
        grid=(m // tm,),
        in_specs=[pl.BlockSpec((tm, d), lambda i: (i, 0)), yspec, yspec, yspec, yspec,
                  pl.BlockSpec((dm, d), lambda i: (0, 0)),
                  pl.BlockSpec((1, d), lambda i: (0, 0)),
                  pl.BlockSpec((1, d), lambda i: (0, 0))],
        out_specs=pl.BlockSpec((tm, d), lambda i: (i, 0)),
        out_shape=jax.ShapeDtypeStruct((m, d), F32),
        compiler_params=_params(1, vmem),
        name="out_projection_layernorm",
    )(h2, *ys, w_bf16, g.reshape(1, d), b.reshape(1, d))


def _head_lane_masks():
    m = np.zeros((N_HEADS, D_BRANCH), np.float32)
    for h in range(N_HEADS):
        m[h, h * HEAD_DIM:(h + 1) * HEAD_DIM] = 1.0
    return m


def _head_block_diag(value=1.0):
    hm = _head_lane_masks()
    return (hm.T @ hm) * value


def _mixer_call(kernel, proj, col_blocks, consts, scratch, n_batch, seq, name, rows=MIX_ROWS, vmem=24 << 20):
    steps = seq // rows
    in_specs = [pl.BlockSpec((rows, w), functools.partial(lambda b, t, c: (b * steps + t, c), c=c))
                for (w, c) in col_blocks]
    in_specs += [pl.BlockSpec(a.shape, functools.partial(lambda b, t, nd: (0,) * nd, nd=a.ndim))
                 for a in consts]
    return pl.pallas_call(
        kernel,
        grid=(n_batch, steps),
        in_specs=in_specs,
        out_specs=pl.BlockSpec((rows, D_BRANCH), lambda b, t: (b * steps + t, 0)),
        out_shape=jax.ShapeDtypeStruct((n_batch * seq, D_BRANCH), F32),
        scratch_shapes=scratch,
        compiler_params=_params(2, vmem),
        name=name,
    )(*([proj] * len(col_blocks)), *consts)


def _ssd_kernel(xbc_ref, z_ref, dt_ref, convw_ref, convb_ref, dtb_ref, aneg_ref, dskip_ref, nw_ref,
                tri_ref, negmask_ref, exph_ref, gmask_ref, hmask_ref, smask_ref,
                o_ref, xp_ref, state_ref):
    C = SSD_CHUNK
    rows = xbc_ref.shape[0]
    pad = SUBLANES

    @pl.when(pl.program_id(1) == 0)
    def _():
        xp_ref[0:pad, :] = jnp.zeros((pad, SSM_XBC), F32)
        state_ref[...] = jnp.zeros_like(state_ref)

    xp_ref[pad:pad + rows, :] = xbc_ref[...]

    convw = convw_ref[...]
    gmask = gmask_ref[...]
    hmask = hmask_ref[...]
    exph = exph_ref[...]
    for c in range(rows // C):
        r0 = c * C
        acc = convb_ref[...]
        for j in range(SSM_CONV):
            off = pad + r0 - (SSM_CONV - 1) + j
            acc = acc + convw[j:j + 1, :] * xp_ref[off:off + C, :]
        xbc = _silu(acc)
        xs = xbc[:, :D_BRANCH]
        bm = xbc[:, D_BRANCH:D_BRANCH + LANES]
        cm = xbc[:, D_BRANCH + LANES:]
        dt = _softplus(dt_ref[r0:r0 + C, :] + dtb_ref[...])
        a = dt * aneg_ref[...]
        a_cum = _sel_dot(tri_ref[...], a)
        a_cum_t = a_cum.T
        col = jnp.concatenate(
            [jnp.broadcast_to(a_cum[:, h:h + 1], (C, C)) for h in range(N_HEADS)], axis=1)
        row = jnp.concatenate([a_cum_t[h:h + 1, :] for h in range(N_HEADS)], axis=1)
        lcat = jnp.exp(col - row + negmask_ref[...])
        bstack = jnp.concatenate(
            [bm * gmask[h // (N_HEADS // SSM_GROUPS):h // (N_HEADS // SSM_GROUPS) + 1, :]
             for h in range(N_HEADS)], axis=0).astype(BF16)
        cmb = cm.astype(BF16)
        p = (_dot_nt(cmb, bstack) * lcat).astype(BF16)
        dth = _dot_sel(dt, exph)
        ach = _dot_sel(a_cum, exph)
        xdt = xs * dth
        y = _dot(p, _row_stack(xdt, hmask).astype(BF16))
        s_old = state_ref[...]
        y = y + _dot(cmb, s_old.astype(BF16)) * jnp.exp(ach)
        alast = ach[C - 1:C, :]
        decay = jnp.exp(alast - ach)
        state_ref[...] = (s_old * jnp.exp(alast)
                          + smask_ref[...] * _dot_tn(bm.astype(BF16), (xdt * decay).astype(BF16)))
        y = y + xs * dskip_ref[...]
        g = y * _silu(z_ref[r0:r0 + C, :])
        outs = []
        for grp in range(SSM_GROUPS):
            gg = g[:, grp * LANES:(grp + 1) * LANES]
            ms = jnp.mean(gg * gg, axis=-1, keepdims=True)
            outs.append(gg * lax.rsqrt(ms + RMS_EPS))
        o_ref[r0:r0 + C, :] = jnp.concatenate(outs, axis=1) * nw_ref[...]

    xp_ref[0:pad, :] = xbc_ref[rows - pad:rows, :]


def _ssd_consts():
    C = SSD_CHUNK
    idx = np.arange(C)
    tri = (idx[:, None] >= idx[None, :]).astype(np.float32)
    negmask = np.tile(np.where(tri > 0, 0.0, NEG_BIG).astype(np.float32), (1, N_HEADS))
    hm = _head_lane_masks()
    exph = np.zeros((LANES, D_BRANCH), np.float32)
    exph[:N_HEADS] = hm
    gmask = np.zeros((SSM_GROUPS, LANES), np.float32)
    for g in range(SSM_GROUPS):
        gmask[g, g * SSM_STATE:(g + 1) * SSM_STATE] = 1.0
    heads_per_group = N_HEADS // SSM_GROUPS
    smask = np.zeros((LANES, D_BRANCH), np.float32)
    for h in range(N_HEADS):
        g = h // heads_per_group
        smask[g * SSM_STATE:(g + 1) * SSM_STATE, h * HEAD_DIM:(h + 1) * HEAD_DIM] = 1.0
    return (jnp.asarray(tri, BF16), jnp.asarray(negmask), jnp.asarray(exph, BF16),
            jnp.asarray(gmask), jnp.asarray(hm), jnp.asarray(smask))


def _ssd(proj, conv_w, conv_b, dt_bias, a_log, d_skip, norm_w, n_batch, seq):
    lane_pad = LANES - N_HEADS
    dtb = jnp.pad(dt_bias.astype(F32), (0, lane_pad)).reshape(1, LANES)
    aneg = jnp.pad(-jnp.exp(a_log.astype(F32)), (0, lane_pad)).reshape(1, LANES)
    dskip = jnp.repeat(d_skip.astype(F32), HEAD_DIM).reshape(1, D_BRANCH)
    consts = [conv_w.astype(F32), conv_b.reshape(1, SSM_XBC).astype(F32), dtb, aneg, dskip,
              norm_w.reshape(1, D_BRANCH).astype(F32), *_ssd_consts()]
    scratch = [pltpu.VMEM((MIX_ROWS + SUBLANES, SSM_XBC), F32),
               pltpu.VMEM((SSM_GROUPS * SSM_STATE, D_BRANCH), F32)]
    return _mixer_call(_ssd_kernel, proj, [(SSM_XBC, 0), (D_BRANCH, COL_Z), (LANES, COL_DT)],
                       consts, scratch, n_batch, seq, "ssd_mixer")


def _seg_mean(x, avg):
    return _dot_sel(x, avg)


def _ret_kernel(q_ref, k_ref, v_ref, g_ref, dcat_ref, qdec_ref, kdec_ref, cg_ref, hmask_ref,
                bd_ref, avg_ref, o_ref, state_ref):
    C = RET_CHUNK
    rows = q_ref.shape[0]

    @pl.when(pl.program_id(1) == 0)
    def _():
        state_ref[...] = jnp.zeros_like(state_ref)

    hmask = hmask_ref[...]
    for c in range(rows // C):
        sl = slice(c * C, (c + 1) * C)
        q = q_ref[sl, :]
        k = k_ref[sl, :]
        v = v_ref[sl, :]
        vb = v.astype(BF16)
        scores = _dot_nt(q.astype(BF16), _row_stack(k, hmask).astype(BF16))
        p = (scores * dcat_ref[...]).astype(BF16)
        y = _dot(p, _row_stack(v, hmask).astype(BF16))
        r_old = state_ref[...]
        y = y + _dot((q * qdec_ref[...]).astype(BF16), r_old.astype(BF16))
        state_ref[...] = (r_old * cg_ref[...]
                          + bd_ref[...] * _dot_tn((k * kdec_ref[...]).astype(BF16), vb))
        mu = _seg_mean(y, avg_ref[...])
        yc = y - mu
        var = _seg_mean(yc * yc, avg_ref[...])
        o_ref[sl, :] = yc * lax.rsqrt(var + LN_EPS) * _silu(g_ref[sl, :])


def _ret_consts():
    C = RET_CHUNK
    log_g = jnp.log(1.0 - 2.0 ** (-5.0 - jnp.arange(N_HEADS, dtype=F32)))
    pos = jnp.arange(C, dtype=F32)
    dist = pos[:, None] - pos[None, :]
    intra = jnp.where(dist >= 0, jnp.exp(log_g[:, None, None] * jnp.maximum(dist, 0.0)), 0.0)
    scale = HEAD_DIM ** -0.5
    dcat = jnp.concatenate([intra[h] for h in range(N_HEADS)], axis=1) * scale
    k_decay = jnp.exp(log_g[:, None] * (C - 1.0 - pos)[None, :]) * scale
    q_decay = jnp.exp(log_g[:, None] * (pos + 1.0)[None, :])
    kdec = jnp.repeat(k_decay.T, HEAD_DIM, axis=1)
    qdec = jnp.repeat(q_decay.T, HEAD_DIM, axis=1)
    cg = jnp.repeat(jnp.exp(log_g * C), HEAD_DIM).reshape(1, D_BRANCH)
    return (dcat, qdec, kdec, cg, jnp.asarray(_head_lane_masks()),
            jnp.asarray(_head_block_diag()), jnp.asarray(_head_block_diag(1.0 / HEAD_DIM), BF16))


def _retention(proj, n_batch, seq):
    scratch = [pltpu.VMEM((D_BRANCH, D_BRANCH), F32)]
    cols = [(D_BRANCH, COL_RET + i) for i in range(4)]
    return _mixer_call(_ret_kernel, proj, cols, list(_ret_consts()), scratch, n_batch, seq, "retention_mixer")


def _hgrn_levels():
    return int(math.log2(HGRN_CHUNK))


def _hgrn_kernel(q_ref, f_ref, i_ref, g_ref, lb_ref, nw_ref, mstack_ref, lmask_ref, hmask_ref,
                 bd_ref, avg_ref, o_ref, state_ref):
    C = HGRN_CHUNK
    nl = _hgrn_levels()
    rows = q_ref.shape[0]

    @pl.when(pl.program_id(1) == 0)
    def _():
        state_ref[...] = jnp.zeros_like(state_ref)

    hmask = hmask_ref[...]
    lb = lb_ref[...]
    for c in range(rows // C):
        sl = slice(c * C, (c + 1) * C)
        q = q_ref[sl, :]
        v = i_ref[sl, :]
        f = lb + (1.0 - lb) * _sigmoid(f_ref[sl, :])
        logf = jnp.log(f)
        k = 1.0 - f
        e_all = jnp.exp(_sel_dot(mstack_ref[...], logf))
        kstack = _row_stack(k, hmask)
        att = lmask_ref[nl] * _dot_nt(q.astype(BF16), kstack.astype(BF16))
        for j in range(nl):
            ej = e_all[j * C:(j + 1) * C, :]
            qj = (q * ej).astype(BF16)
            kj = (kstack * jnp.concatenate([ej] * N_HEADS, axis=0)).astype(BF16)
            att = att + lmask_ref[j] * _dot_nt(qj, kj)
        o = _dot(att.astype(BF16), _row_stack(v, hmask).astype(BF16))
        eb = e_all[nl * C:(nl + 1) * C, :]
        er = e_all[(nl + 1) * C:, :]
        st_old = state_ref[...]
        o = o + _dot_nt((q * eb).astype(BF16), st_old.astype(BF16))
        state_ref[...] = (st_old * eb[C - 1:C, :]
                          + bd_ref[...] * _dot_tn(v.astype(BF16), (k * er).astype(BF16)))
        ms = _seg_mean(o * o, avg_ref[...])
        o_ref[sl, :] = o * lax.rsqrt(ms + RMS_EPS) * nw_ref[...] * _silu(g_ref[sl, :])


def _hgrn_consts():
    C = HGRN_CHUNK
    nl = _hgrn_levels()
    t = np.arange(C)[:, None]
    u = np.arange(C)[None, :]
    blocks, masks = [], []
    for j in range(nl):
        s = 1 << j
        bnd = (t // (2 * s)) * (2 * s) + s - 1
        upper = (t % (2 * s)) >= s
        m = np.where(upper, (u > bnd) & (u <= t), (u > t) & (u <= bnd))
        blocks.append(m.astype(np.float32))
        pair = (t // (2 * s)) == (u // (2 * s))
        masks.append((upper & pair & ((u % (2 * s)) < s)).astype(np.float32))
    blocks.append((u <= t).astype(np.float32))
    blocks.append((u > t).astype(np.float32))
    masks.append((u == t).astype(np.float32))
    mstack = np.concatenate(blocks, axis=0)
    lmask = np.stack([np.tile(m, (1, N_HEADS)) for m in masks])
    return (jnp.asarray(mstack, BF16), jnp.asarray(lmask), jnp.asarray(_head_lane_masks()),
            jnp.asarray(_head_block_diag()), jnp.asarray(_head_block_diag(1.0 / HEAD_DIM), BF16))


def _hgrn2(proj, lb, norm_w, n_batch, seq):
    consts = [lb.reshape(1, D_BRANCH).astype(F32), norm_w.reshape(1, D_BRANCH).astype(F32), *_hgrn_consts()]
    scratch = [pltpu.VMEM((D_BRANCH, D_BRANCH), F32)]
    cols = [(D_BRANCH, COL_HGRN + i) for i in range(4)]
    return _mixer_call(_hgrn_kernel, proj, cols, consts, scratch, n_batch, seq, "hgrn2_mixer")


MOBA_VROWS = 80


def _moba_kernel(q_ref, k_ref, v_ref, g_ref, ksel_ref, kpos_ref, qaug_ref, slope_ref, o_ref,
                 kaug_ref, vt_ref, kmean_ref, mb_ref, qt_ref, acc_ref, m_ref):
    S = MOBA_BLOCK
    i = pl.program_id(1)
    nbp = kmean_ref.shape[0]

    @pl.when(i == 0)
    def _():
        kmean_ref[...] = jnp.zeros_like(kmean_ref)

    k = k_ref[...]
    kb = k.astype(BF16)
    kmean_ref[pl.ds(i, 1), :] = jnp.mean(k, axis=0, keepdims=True)
    v_t = v_ref[...].T
    q_t = (q_ref[...] * (HEAD_DIM ** -0.5)).T
    ones_pad = (lax.broadcasted_iota(jnp.int32, (MOBA_VROWS - HEAD_DIM, S), 0) == 0).astype(F32)
    blk = lax.broadcasted_iota(jnp.int32, (nbp, S), 0)
    blk_f = blk.astype(F32)
    past = blk < i
    kmean = kmean_ref[...]
    for h in range(N_HEADS):
        hs = slice(h * HEAD_DIM, (h + 1) * HEAD_DIM)
        kaug_ref[i, h] = (_dot(kb, ksel_ref[h]) + kpos_ref[...]).astype(BF16)
        vt_ref[i, h] = jnp.concatenate([v_t[hs, :], ones_pad], axis=0).astype(BF16)
        qh_t = q_t[hs, :]
        qt_ref[h] = jnp.concatenate([qh_t, qaug_ref[h]], axis=0).astype(BF16)
        km_hi, km_mid, _ = _split3(kmean[:, hs])
        q_hi, q_mid, _ = _split3(qh_t)
        gate = _dot(km_hi, q_hi) + _dot(km_hi, q_mid) + _dot(km_mid, q_hi)
        gate = jnp.where(past, gate, -jnp.inf)
        sel = jnp.zeros((nbp, S), F32)
        for _ in range(MOBA_TOPK):
            best = jnp.max(gate, axis=0, keepdims=True)
            first = jnp.min(jnp.where(gate == best, blk_f, float(nbp)), axis=0, keepdims=True)
            hit = blk_f == first
            sel = jnp.where(hit, 1.0, sel)
            gate = jnp.where(hit, -jnp.inf, gate)
        block_bias = slope_ref[h] * ((blk - i) * S).astype(F32)
        mb_ref[h] = jnp.where(jnp.logical_and(sel > 0.0, past), block_bias, NEG_BIG)

    key_pos = lax.broadcasted_iota(jnp.int32, (S, S), 0)
    qry_pos = lax.broadcasted_iota(jnp.int32, (S, S), 1)
    for h in range(N_HEADS):
        s_t = _dot(kaug_ref[i, h], qt_ref[h])
        s_t = jnp.where(key_pos <= qry_pos, s_t, NEG_BIG)
        m_new = jnp.max(s_t, axis=0, keepdims=True)
        p = jnp.exp(s_t - m_new)
        acc_ref[h] = _dot(vt_ref[i, h], p.astype(BF16))
        m_ref[h] = m_new

    def past_block(n, carry):
        for h in range(N_HEADS):
            s_t = _dot(kaug_ref[n, h], qt_ref[h]) + mb_ref[h, pl.ds(n, 1), :]
            m_old = m_ref[h]
            m_new = jnp.maximum(m_old, jnp.max(s_t, axis=0, keepdims=True))
            p = jnp.exp(s_t - m_new)
            acc_ref[h] = jnp.exp(m_old - m_new) * acc_ref[h] + _dot(vt_ref[n, h], p.astype(BF16))
            m_ref[h] = m_new
        return carry

    lax.fori_loop(0, i, past_block, 0)

    outs = []
    for h in range(N_HEADS):
        acc = acc_ref[h]
        outs.append(acc[:HEAD_DIM, :] / acc[HEAD_DIM:HEAD_DIM + 1, :])
    o_ref[...] = jnp.concatenate(outs, axis=0).T * _silu(g_ref[...])


def _moba(proj, n_batch, seq):
    S = MOBA_BLOCK
    assert seq % S == 0
    nb = seq // S
    nbp = -(-nb // SUBLANES) * SUBLANES
    ksel = np.zeros((N_HEADS, D_BRANCH, LANES), np.float32)
    for h in range(N_HEADS):
        ksel[h, h * HEAD_DIM + np.arange(HEAD_DIM), np.arange(HEAD_DIM)] = 1.0
    kpos = np.zeros((S, LANES), np.float32)
    kpos[:, HEAD_DIM] = np.arange(S)
    slopes = 2.0 ** (-8.0 * np.arange(1, N_HEADS + 1, dtype=np.float32) / N_HEADS)
    qaug = np.zeros((N_HEADS, LANES - HEAD_DIM, S), np.float32)
    qaug[:, 0, :] = slopes[:, None]
    slope_tab = np.broadcast_to(slopes[:, None, None], (N_HEADS, 1, S)).astype(np.float32)
    consts = [jnp.asarray(ksel, BF16), jnp.asarray(kpos), jnp.asarray(qaug), jnp.asarray(slope_tab)]
    scratch = [pltpu.VMEM((nb, N_HEADS, S, LANES), BF16),
               pltpu.VMEM((nb, N_HEADS, MOBA_VROWS, S), BF16),
               pltpu.VMEM((nbp, D_BRANCH), F32),
               pltpu.VMEM((N_HEADS, nbp, S), F32),
               pltpu.VMEM((N_HEADS, LANES, S), BF16),
               pltpu.VMEM((N_HEADS, MOBA_VROWS, S), F32),
               pltpu.VMEM((N_HEADS, 1, S), F32)]
    vmem = nb * N_HEADS * S * (LANES + MOBA_VROWS) * 2 + (24 << 20)
    cols = [(D_BRANCH, COL_MOBA + i) for i in range(4)]
    return _mixer_call(_moba_kernel, proj, cols, consts, scratch, n_batch, seq, "moba_mixer",
                       rows=S, vmem=vmem)


def _pack_w_in(w):
    o = np.cumsum((0, D_BRANCH, SSM_XBC, N_HEADS) + (D_BRANCH,) * 12)
    z, xbc, dt, rest = w[:, o[0]:o[1]], w[:, o[1]:o[2]], w[:, o[2]:o[3]], w[:, o[3]:]
    used = SSM_XBC + D_BRANCH + rest.shape[1] + N_HEADS
    pad = jnp.zeros((w.shape[0], N_PROJ - used), w.dtype)
    return jnp.concatenate([xbc, z, rest, dt, pad], axis=1).astype(BF16)


def kernel(x, emb_ln_g, emb_ln_b, w_in, ssd_conv_w, ssd_conv_b, ssd_dt_bias, ssd_a_log, ssd_d, ssd_norm_w,
           hgrn_lb_logits, hgrn_norm_w, w_out, ln_g, ln_b):
    n_batch, seq, d_model = x.shape
    depth = w_in.shape[0]
    alpha = (2.0 * depth) ** 0.25
    assert seq % MIX_ROWS == 0 and seq % MOBA_BLOCK == 0

    lbs = jnp.cumsum(jax.nn.softmax(hgrn_lb_logits.astype(F32), axis=0), axis=0)
    lbs = lbs - lbs[0]

    h = _layernorm(x.reshape(n_batch * seq, d_model), emb_ln_g, emb_ln_b)
    for l in range(depth):
        proj = _inproj(h, _pack_w_in(w_in[l]))
        ys = [_ssd(proj, ssd_conv_w[l], ssd_conv_b[l], ssd_dt_bias[l], ssd_a_log[l], ssd_d[l],
                   ssd_norm_w[l], n_batch, seq),
              _hgrn2(proj, lbs[l], hgrn_norm_w[l], n_batch, seq),
              _retention(proj, n_batch, seq),
              _moba(proj, n_batch, seq)]
        h = _outproj(h, ys, w_out[l].astype(BF16), ln_g[l], ln_b[l], alpha)
    return h.reshape(n_batch, seq, d_model)
```

```python
import functools
import math

import numpy as np
import jax
import jax.numpy as jnp
from jax import lax
from jax.experimental import pallas as pl
from jax.experimental.pallas import tpu as pltpu

F32 = jnp.float32
BF16 = jnp.bfloat16

D_BRANCH = 256
HEAD_DIM = 64
N_HEADS = 4
SSM_STATE = 64
SSM_GROUPS = 2
SSM_CONV = 4
SSM_XBC = D_BRANCH + 2 * SSM_GROUPS * SSM_STATE
MOBA_BLOCK = 256
MOBA_TOPK = 3
LN_EPS = 1e-5
RMS_EPS = 1e-6
NEG_BIG = -1e30
LOG2E = 1.4426950408889634

LANES = 128
SUBLANES = 8
VMEM_BYTES_V7X = 64 * 1024 * 1024

SSD_CHUNK = 128
RET_CHUNK = 128
HGRN_CHUNK = 64
MIX_ROWS = 256

N_PROJ = 4096
COL_Z = SSM_XBC // D_BRANCH
COL_HGRN = COL_Z + 1
COL_RET = COL_HGRN + 4
COL_MOBA = COL_RET + 4
COL_DT = (COL_MOBA + 4) * D_BRANCH // LANES


def _dot(a, b):
    return jnp.dot(a, b, preferred_element_type=F32)


def _dot_nt(a, b):
    return lax.dot_general(a, b, (((1,), (1,)), ((), ())), preferred_element_type=F32)


def _dot_tn(a, b):
    return lax.dot_general(a, b, (((0,), (0,)), ((), ())), preferred_element_type=F32)


def _split3(x):
    hi = x.astype(BF16)
    r1 = x - hi.astype(F32)
    mid = r1.astype(BF16)
    lo = (r1 - mid.astype(F32)).astype(BF16)
    return hi, mid, lo


def _sel_dot(sel, x):
    hi, mid, lo = _split3(x)
    return _dot(sel, hi) + _dot(sel, mid) + _dot(sel, lo)


def _dot_sel(x, sel):
    hi, mid, lo = _split3(x)
    return _dot(hi, sel) + _dot(mid, sel) + _dot(lo, sel)


def _sigmoid(x):
    return 1.0 / (1.0 + jnp.exp(-x))


def _silu(x):
    return x * _sigmoid(x)


def _softplus(x):
    return jnp.maximum(x, 0.0) + jnp.log(1.0 + jnp.exp(-jnp.abs(x)))


def _row_stack(x, masks):
    return jnp.concatenate([x * masks[h:h + 1, :] for h in range(masks.shape[0])], axis=0)


def _params(n_axes, vmem_bytes):
    return pltpu.CompilerParams(dimension_semantics=("arbitrary",) * n_axes,
                                vmem_limit_bytes=int(vmem_bytes))


def _ln_rows(x, g, b):
    mu = jnp.mean(x, axis=-1, keepdims=True)
    xc = x - mu
    var = jnp.mean(xc * xc, axis=-1, keepdims=True)
    return xc * lax.rsqrt(var + LN_EPS) * g + b


def _ln_kernel(x_ref, g_ref, b_ref, o_ref):
    o_ref[...] = _ln_rows(x_ref[...], g_ref[...], b_ref[...])


def _layernorm(x2, g, b, tm=512):
    m, d = x2.shape
    return pl.pallas_call(
        _ln_kernel,
        grid=(m // tm,),
        in_specs=[pl.BlockSpec((tm, d), lambda i: (i, 0)),
                  pl.BlockSpec((1, d), lambda i: (0, 0)),
                  pl.BlockSpec((1, d), lambda i: (0, 0))],
        out_specs=pl.BlockSpec((tm, d), lambda i: (i, 0)),
        out_shape=jax.ShapeDtypeStruct((m, d), F32),
        compiler_params=_params(1, 4 * tm * d * 4 + (4 << 20)),
        name="embed_layernorm",
    )(x2, g.reshape(1, d), b.reshape(1, d))


def _inproj_kernel(x_ref, wt_ref, o_ref, xb_ref, *, tn):
    j = pl.program_id(1)
    k_tiles = x_ref.shape[1] // LANES

    @pl.when(j == 0)
    def _():
        xb_ref[...] = x_ref[...].astype(BF16)

    base = pl.multiple_of(j * (tn * k_tiles), tn * k_tiles)
    wt = jnp.concatenate([wt_ref[pl.ds(base + kt, tn, stride=k_tiles), :] for kt in range(k_tiles)],
                         axis=1).astype(BF16)
    o_ref[...] = _dot_nt(xb_ref[...], wt)


def _inproj(h2, wt_all, layer, tm=1024, tn=512):
    m, d = h2.shape
    rows = wt_all.shape[1]
    n = rows * LANES // d
    vmem = 2 * tm * d * 4 + d * n * 4 + 2 * tm * tn * 4 + tm * d * 2 + 2 * d * tn * 4 + (4 << 20)
    return pl.pallas_call(
        functools.partial(_inproj_kernel, tn=tn),
        grid=(m // tm, n // tn),
        in_specs=[pl.BlockSpec((tm, d), lambda i, j: (i, 0)),
                  pl.BlockSpec((None, rows, LANES), lambda i, j: (layer, 0, 0), pipeline_mode=pl.Buffered(1))],
        out_specs=pl.BlockSpec((tm, tn), lambda i, j: (i, j)),
        out_shape=jax.ShapeDtypeStruct((m, n), F32),
        scratch_shapes=[pltpu.VMEM((tm, d), BF16)],
        compiler_params=_params(2, vmem),
        name="in_projection",
    )(h2, wt_all)


def _outproj_kernel(h_ref, y0_ref, y1_ref, y2_ref, y3_ref, w_ref, g_ref, b_ref, o_ref, *, alpha):
    ycat = jnp.concatenate([y0_ref[...], y1_ref[...], y2_ref[...], y3_ref[...]], axis=1)
    y = _dot(ycat.astype(BF16), w_ref[...])
    o_ref[...] = _ln_rows(alpha * h_ref[...] + y, g_ref[...], b_ref[...])


def _outproj(h2, ys, w_bf16, g, b, alpha, tm=512):
    m, d = h2.shape
    dm = w_bf16.shape[0]
    yspec = pl.BlockSpec((tm, D_BRANCH), lambda i: (i, 0))
    vmem = 2 * (2 * tm * d * 4 + tm * dm * 4 + dm * d * 2) + tm * d * 8 + (4 << 20)
    return pl.pallas_call(
        functools.partial(_outproj_kernel, alpha=alpha),
        grid=(m // tm,),
        in_specs=[pl.BlockSpec((tm, d), lambda i: (i, 0)), yspec, yspec, yspec, yspec,
                  pl.BlockSpec((dm, d), lambda i: (0, 0)),
                  pl.BlockSpec((1, d), lambda i: (0, 0)),
                  pl.BlockSpec((1, d), lambda i: (0, 0))],
        out_specs=pl.BlockSpec((tm, d), lambda i: (i, 0)),
        out_shape=jax.ShapeDtypeStruct((m, d), F32),
        compiler_params=_params(1, vmem),
        name="out_projection_layernorm",
    )(h2, *ys, w_bf16, g.reshape(1, d), b.reshape(1, d))


def _head_lane_masks():
    m = np.zeros((N_HEADS, D_BRANCH), np.float32)
    for h in range(N_HEADS):
        m[h, h * HEAD_DIM:(h + 1) * HEAD_DIM] = 1.0
    return m


def _head_block_diag(value=1.0):
    hm = _head_lane_masks()
    return (hm.T @ hm) * value


def _mixer_call(kernel, proj, col_blocks, consts, scratch, n_batch, seq, name, rows=MIX_ROWS, vmem=24 << 20):
    steps = seq // rows
    in_specs = [pl.BlockSpec((rows, w), functools.partial(lambda b, t, c: (b * steps + t, c), c=c))
                for (w, c) in col_blocks]
    in_specs += [pl.BlockSpec(a.shape, functools.partial(lambda b, t, nd: (0,) * nd, nd=a.ndim))
                 for a in consts]
    return pl.pallas_call(
        kernel,
        grid=(n_batch, steps),
        in_specs=in_specs,
        out_specs=pl.BlockSpec((rows, D_BRANCH), lambda b, t: (b * steps + t, 0)),
        out_shape=jax.ShapeDtypeStruct((n_batch * seq, D_BRANCH), F32),
        scratch_shapes=scratch,
        compiler_params=_params(2, vmem),
        name=name,
    )(*([proj] * len(col_blocks)), *consts)


def _ssd_kernel(xbc_ref, z_ref, dt_ref, convw_ref, convb_ref, dtb_ref, aneg_ref, dskip_ref, nw_ref,
                tri_ref, negmask_ref, exph_ref, gmask_ref, hmask_ref, smask_ref,
                o_ref, xp_ref, state_ref):
    C = SSD_CHUNK
    rows = xbc_ref.shape[0]
    pad = SUBLANES

    @pl.when(pl.program_id(1) == 0)
    def _():
        xp_ref[0:pad, :] = jnp.zeros((pad, SSM_XBC), F32)
        state_ref[...] = jnp.zeros_like(state_ref)

    xp_ref[pad:pad + rows, :] = xbc_ref[...]

    convw = convw_ref[...]
    gmask = gmask_ref[...]
    hmask = hmask_ref[...]
    exph = exph_ref[...]
    for c in range(rows // C):
        r0 = c * C
        acc = convb_ref[...]
        for j in range(SSM_CONV):
            off = pad + r0 - (SSM_CONV - 1) + j
            acc = acc + convw[j:j + 1, :] * xp_ref[off:off + C, :]
        xbc = _silu(acc)
        xs = xbc[:, :D_BRANCH]
        bm = xbc[:, D_BRANCH:D_BRANCH + LANES]
        cm = xbc[:, D_BRANCH + LANES:]
        dt = _softplus(dt_ref[r0:r0 + C, :] + dtb_ref[...])
        a = dt * aneg_ref[...]
        a_cum = _sel_dot(tri_ref[...], a)
        a_cum_t = a_cum.T
        col = jnp.concatenate(
            [jnp.broadcast_to(a_cum[:, h:h + 1], (C, C)) for h in range(N_HEADS)], axis=1)
        row = jnp.concatenate([a_cum_t[h:h + 1, :] for h in range(N_HEADS)], axis=1)
        lcat = jnp.exp(col - row + negmask_ref[...])
        bstack = jnp.concatenate(
            [bm * gmask[h // (N_HEADS // SSM_GROUPS):h // (N_HEADS // SSM_GROUPS) + 1, :]
             for h in range(N_HEADS)], axis=0).astype(BF16)
        cmb = cm.astype(BF16)
        p = (_dot_nt(cmb, bstack) * lcat).astype(BF16)
        dth = _dot_sel(dt, exph)
        ach = _dot_sel(a_cum, exph)
        xdt = xs * dth
        y = _dot(p, _row_stack(xdt, hmask).astype(BF16))
        s_old = state_ref[...]
        y = y + _dot(cmb, s_old.astype(BF16)) * jnp.exp(ach)
        alast = ach[C - 1:C, :]
        decay = jnp.exp(alast - ach)
        state_ref[...] = (s_old * jnp.exp(alast)
                          + smask_ref[...] * _dot_tn(bm.astype(BF16), (xdt * decay).astype(BF16)))
        y = y + xs * dskip_ref[...]
        g = y * _silu(z_ref[r0:r0 + C, :])
        outs = []
        for grp in range(SSM_GROUPS):
            gg = g[:, grp * LANES:(grp + 1) * LANES]
            ms = jnp.mean(gg * gg, axis=-1, keepdims=True)
            outs.append(gg * lax.rsqrt(ms + RMS_EPS))
        o_ref[r0:r0 + C, :] = jnp.concatenate(outs, axis=1) * nw_ref[...]

    xp_ref[0:pad, :] = xbc_ref[rows - pad:rows, :]


def _ssd_consts():
    C = SSD_CHUNK
    idx = np.arange(C)
    tri = (idx[:, None] >= idx[None, :]).astype(np.float32)
    negmask = np.tile(np.where(tri > 0, 0.0, NEG_BIG).astype(np.float32), (1, N_HEADS))
    hm = _head_lane_masks()
    exph = np.zeros((LANES, D_BRANCH), np.float32)
    exph[:N_HEADS] = hm
    gmask = np.zeros((SSM_GROUPS, LANES), np.float32)
    for g in range(SSM_GROUPS):
        gmask[g, g * SSM_STATE:(g + 1) * SSM_STATE] = 1.0
    heads_per_group = N_HEADS // SSM_GROUPS
    smask = np.zeros((LANES, D_BRANCH), np.float32)
    for h in range(N_HEADS):
        g = h // heads_per_group
        smask[g * SSM_STATE:(g + 1) * SSM_STATE, h * HEAD_DIM:(h + 1) * HEAD_DIM] = 1.0
    return (jnp.asarray(tri, BF16), jnp.asarray(negmask), jnp.asarray(exph, BF16),
            jnp.asarray(gmask), jnp.asarray(hm), jnp.asarray(smask))


def _ssd(proj, conv_w, conv_b, dt_bias, a_log, d_skip, norm_w, n_batch, seq):
    lane_pad = LANES - N_HEADS
    dtb = jnp.pad(dt_bias.astype(F32), (0, lane_pad)).reshape(1, LANES)
    aneg = jnp.pad(-jnp.exp(a_log.astype(F32)), (0, lane_pad)).reshape(1, LANES)
    dskip = jnp.repeat(d_skip.astype(F32), HEAD_DIM).reshape(1, D_BRANCH)
    consts = [conv_w.astype(F32), conv_b.reshape(1, SSM_XBC).astype(F32), dtb, aneg, dskip,
              norm_w.reshape(1, D_BRANCH).astype(F32), *_ssd_consts()]
    scratch = [pltpu.VMEM((MIX_ROWS + SUBLANES, SSM_XBC), F32),
               pltpu.VMEM((SSM_GROUPS * SSM_STATE, D_BRANCH), F32)]
    return _mixer_call(_ssd_kernel, proj, [(SSM_XBC, 0), (D_BRANCH, COL_Z), (LANES, COL_DT)],
                       consts, scratch, n_batch, seq, "ssd_mixer")


def _seg_mean(x, avg):
    return _dot_sel(x, avg)


def _ret_kernel(q_ref, k_ref, v_ref, g_ref, dcat_ref, qdec_ref, kdec_ref, cg_ref, hmask_ref,
                bd_ref, avg_ref, o_ref, state_ref):
    C = RET_CHUNK
    rows = q_ref.shape[0]

    @pl.when(pl.program_id(1) == 0)
    def _():
        state_ref[...] = jnp.zeros_like(state_ref)

    hmask = hmask_ref[...]
    for c in range(rows // C):
        sl = slice(c * C, (c + 1) * C)
        q = q_ref[sl, :]
        k = k_ref[sl, :]
        v = v_ref[sl, :]
        vb = v.astype(BF16)
        scores = _dot_nt(q.astype(BF16), _row_stack(k, hmask).astype(BF16))
        p = (scores * dcat_ref[...]).astype(BF16)
        y = _dot(p, _row_stack(v, hmask).astype(BF16))
        r_old = state_ref[...]
        y = y + _dot((q * qdec_ref[...]).astype(BF16), r_old.astype(BF16))
        state_ref[...] = (r_old * cg_ref[...]
                          + bd_ref[...] * _dot_tn((k * kdec_ref[...]).astype(BF16), vb))
        mu = _seg_mean(y, avg_ref[...])
        yc = y - mu
        var = _seg_mean(yc * yc, avg_ref[...])
        o_ref[sl, :] = yc * lax.rsqrt(var + LN_EPS) * _silu(g_ref[sl, :])


def _ret_consts():
    C = RET_CHUNK
    log_g = jnp.log(1.0 - 2.0 ** (-5.0 - jnp.arange(N_HEADS, dtype=F32)))
    pos = jnp.arange(C, dtype=F32)
    dist = pos[:, None] - pos[None, :]
    intra = jnp.where(dist >= 0, jnp.exp(log_g[:, None, None] * jnp.maximum(dist, 0.0)), 0.0)
    scale = HEAD_DIM ** -0.5
    dcat = jnp.concatenate([intra[h] for h in range(N_HEADS)], axis=1) * scale
    k_decay = jnp.exp(log_g[:, None] * (C - 1.0 - pos)[None, :]) * scale
    q_decay = jnp.exp(log_g[:, None] * (pos + 1.0)[None, :])
    kdec = jnp.repeat(k_decay.T, HEAD_DIM, axis=1)
    qdec = jnp.repeat(q_decay.T, HEAD_DIM, axis=1)
    cg = jnp.repeat(jnp.exp(log_g * C), HEAD_DIM).reshape(1, D_BRANCH)
    return (dcat, qdec, kdec, cg, jnp.asarray(_head_lane_masks()),
            jnp.asarray(_head_block_diag()), jnp.asarray(_head_block_diag(1.0 / HEAD_DIM), BF16))


def _retention(proj, n_batch, seq):
    scratch = [pltpu.VMEM((D_BRANCH, D_BRANCH), F32)]
    cols = [(D_BRANCH, COL_RET + i) for i in range(4)]
    return _mixer_call(_ret_kernel, proj, cols, list(_ret_consts()), scratch, n_batch, seq, "retention_mixer")


def _hgrn_levels():
    return int(math.log2(HGRN_CHUNK))


def _hgrn_kernel(q_ref, f_ref, i_ref, g_ref, lb_ref, nw_ref, mstack_ref, lmask_ref, hmask_ref,
                 bd_ref, avg_ref, o_ref, state_ref):
    C = HGRN_CHUNK
    nl = _hgrn_levels()
    rows = q_ref.shape[0]

    @pl.when(pl.program_id(1) == 0)
    def _():
        state_ref[...] = jnp.zeros_like(state_ref)

    hmask = hmask_ref[...]
    lb = lb_ref[...]
    for c in range(rows // C):
        sl = slice(c * C, (c + 1) * C)
        q = q_ref[sl, :]
        v = i_ref[sl, :]
        f = lb + (1.0 - lb) * _sigmoid(f_ref[sl, :])
        logf = jnp.log(f)
        k = 1.0 - f
        e_all = jnp.exp(_sel_dot(mstack_ref[...], logf))
        kstack = _row_stack(k, hmask)
        att = lmask_ref[nl] * _dot_nt(q.astype(BF16), kstack.astype(BF16))
        for j in range(nl):
            ej = e_all[j * C:(j + 1) * C, :]
            qj = (q * ej).astype(BF16)
            kj = (kstack * jnp.concatenate([ej] * N_HEADS, axis=0)).astype(BF16)
            att = att + lmask_ref[j] * _dot_nt(qj, kj)
        o = _dot(att.astype(BF16), _row_stack(v, hmask).astype(BF16))
        eb = e_all[nl * C:(nl + 1) * C, :]
        er = e_all[(nl + 1) * C:, :]
        st_old = state_ref[...]
        o = o + _dot_nt((q * eb).astype(BF16), st_old.astype(BF16))
        state_ref[...] = (st_old * eb[C - 1:C, :]
                          + bd_ref[...] * _dot_tn(v.astype(BF16), (k * er).astype(BF16)))
        ms = _seg_mean(o * o, avg_ref[...])
        o_ref[sl, :] = o * lax.rsqrt(ms + RMS_EPS) * nw_ref[...] * _silu(g_ref[sl, :])


def _hgrn_consts():
    C = HGRN_CHUNK
    nl = _hgrn_levels()
    t = np.arange(C)[:, None]
    u = np.arange(C)[None, :]
    blocks, masks = [], []
    for j in range(nl):
        s = 1 << j
        bnd = (t // (2 * s)) * (2 * s) + s - 1
        upper = (t % (2 * s)) >= s
        m = np.where(upper, (u > bnd) & (u <= t), (u > t) & (u <= bnd))
        blocks.append(m.astype(np.float32))
        pair = (t // (2 * s)) == (u // (2 * s))
        masks.append((upper & pair & ((u % (2 * s)) < s)).astype(np.float32))
    blocks.append((u <= t).astype(np.float32))
    blocks.append((u > t).astype(np.float32))
    masks.append((u == t).astype(np.float32))
    mstack = np.concatenate(blocks, axis=0)
    lmask = np.stack([np.tile(m, (1, N_HEADS)) for m in masks])
    return (jnp.asarray(mstack, BF16), jnp.asarray(lmask), jnp.asarray(_head_lane_masks()),
            jnp.asarray(_head_block_diag()), jnp.asarray(_head_block_diag(1.0 / HEAD_DIM), BF16))


def _hgrn2(proj, lb, norm_w, n_batch, seq):
    consts = [lb.reshape(1, D_BRANCH).astype(F32), norm_w.reshape(1, D_BRANCH).astype(F32), *_hgrn_consts()]
    scratch = [pltpu.VMEM((D_BRANCH, D_BRANCH), F32)]
    cols = [(D_BRANCH, COL_HGRN + i) for i in range(4)]
    return _mixer_call(_hgrn_kernel, proj, cols, consts, scratch, n_batch, seq, "hgrn2_mixer")


MOBA_VROWS = 80


def _moba_kernel(q_ref, k_ref, v_ref, g_ref, ksel_ref, kpos_ref, kblk_ref, qaug_ref, o_ref,
                 kaug_ref, vt_ref, kmean_ref, sel_ref, qt_ref, acc_ref, m_ref, mcur_ref, alpha_ref,
                 s_ref, p_ref):
    S = MOBA_BLOCK
    i = pl.program_id(1)
    nbp = kmean_ref.shape[0]

    @pl.when(i == 0)
    def _():
        kmean_ref[...] = jnp.zeros_like(kmean_ref)

    k = k_ref[...]
    kb = k.astype(BF16)
    kmean_ref[pl.ds(i, 1), :] = jnp.mean(k, axis=0, keepdims=True)
    v_t = v_ref[...].T
    q_t = (q_ref[...] * (HEAD_DIM ** -0.5)).T
    ones_pad = (lax.broadcasted_iota(jnp.int32, (MOBA_VROWS - HEAD_DIM, S), 0) == 0).astype(F32)
    blk = lax.broadcasted_iota(jnp.int32, (nbp, S), 0)
    blk_f = blk.astype(F32)
    past = blk < i
    kmean = kmean_ref[...]
    own_rows = (lax.broadcasted_iota(jnp.int32, (SUBLANES, S), 0) == 0).astype(F32)
    kextra = kpos_ref[...] + i.astype(F32) * kblk_ref[...]
    for h in range(N_HEADS):
        hs = slice(h * HEAD_DIM, (h + 1) * HEAD_DIM)
        kaug_ref[i, h] = (_dot(kb, ksel_ref[h]) + kextra).astype(BF16)
        vt_ref[i, h] = jnp.concatenate([v_t[hs, :], ones_pad], axis=0).astype(BF16)
        qh_t = q_t[hs, :]
        qt_ref[h] = jnp.concatenate([qh_t * LOG2E, qaug_ref[h]], axis=0).astype(BF16)
        km_hi, km_mid, _ = _split3(kmean[:, hs])
        q_hi, q_mid, _ = _split3(qh_t)
        gate = _dot(km_hi, q_hi) + _dot(km_hi, q_mid) + _dot(km_mid, q_hi)
        gate = jnp.where(past, gate, -jnp.inf)
        sel = jnp.zeros((nbp, S), F32)
        for _ in range(MOBA_TOPK):
            best = jnp.max(gate, axis=0, keepdims=True)
            first = jnp.min(jnp.where(gate == best, blk_f, float(nbp)), axis=0, keepdims=True)
            hit = blk_f == first
            sel = jnp.where(hit, 1.0, sel)
            gate = jnp.where(hit, -jnp.inf, gate)
        sel_ref[h, 0:nbp, :] = jnp.where(past, sel, 0.0)
        sel_ref[h, nbp:nbp + SUBLANES, :] = own_rows

    def block_of(t):
        return jnp.where(t == 0, i, jnp.minimum(t - 1, i))

    def sel_row_of(t):
        return jnp.where(t == 0, nbp, jnp.where(t <= i, t - 1, nbp + 1))

    def track_max(h, slot, s_t, chosen):
        m_old = m_ref[h]
        m_new = jnp.where(chosen, jnp.maximum(m_old, jnp.max(s_t, axis=0, keepdims=True)), m_old)
        m_ref[h] = m_new
        mcur_ref[slot, h] = m_new
        alpha_ref[slot, h] = jnp.exp2(m_old - m_new)

    def score_phase(t, slot):
        n, row = block_of(t), sel_row_of(t)
        for h in range(N_HEADS):
            s_t = _dot(kaug_ref[n, h], qt_ref[h])
            s_ref[slot, h] = s_t
            track_max(h, slot, s_t, sel_ref[h, pl.ds(row, 1), :] > 0.0)

    def value_phase(t, slot):
        n, row = block_of(t), sel_row_of(t)
        for h in range(N_HEADS):
            p_ref[h] = jnp.exp2(s_ref[slot, h] - mcur_ref[slot, h]).astype(BF16)
        for h in range(N_HEADS):
            chosen = sel_ref[h, pl.ds(row, 1), :] > 0.0
            upd = _dot(vt_ref[n, h], p_ref[h])
            acc_ref[h] = alpha_ref[slot, h] * acc_ref[h] + jnp.where(chosen, upd, 0.0)

    key_pos = lax.broadcasted_iota(jnp.int32, (S, S), 0)
    qry_pos = lax.broadcasted_iota(jnp.int32, (S, S), 1)
    for h in range(N_HEADS):
        acc_ref[h] = jnp.zeros((MOBA_VROWS, S), F32)
        m_ref[h] = jnp.full((1, S), NEG_BIG, F32)
        s_t = jnp.where(key_pos <= qry_pos, _dot(kaug_ref[i, h], qt_ref[h]), NEG_BIG)
        s_ref[0, h] = s_t
        track_max(h, 0, s_t, True)

    def position_pair(j, carry):
        t0 = 2 * j
        score_phase(t0 + 1, 1)
        value_phase(t0, 0)
        score_phase(t0 + 2, 0)
        value_phase(t0 + 1, 1)
        return carry

    lax.fori_loop(0, (i + 2) // 2, position_pair, 0)

    outs = []
    for h in range(N_HEADS):
        acc = acc_ref[h]
        outs.append(acc[:HEAD_DIM, :] / acc[HEAD_DIM:HEAD_DIM + 1, :])
    o_ref[...] = jnp.concatenate(outs, axis=0).T * _silu(g_ref[...])


def _moba(proj, n_batch, seq):
    S = MOBA_BLOCK
    assert seq % S == 0
    nb = seq // S
    nbp = -(-nb // SUBLANES) * SUBLANES
    ksel = np.zeros((N_HEADS, D_BRANCH, LANES), np.float32)
    for h in range(N_HEADS):
        ksel[h, h * HEAD_DIM + np.arange(HEAD_DIM), np.arange(HEAD_DIM)] = 1.0
    n_split = 3
    kpos = np.zeros((S, LANES), np.float32)
    kblk = np.zeros((S, LANES), np.float32)
    kpos[:, HEAD_DIM:HEAD_DIM + n_split] = np.arange(S, dtype=np.float32)[:, None]
    kblk[:, HEAD_DIM + n_split:HEAD_DIM + 2 * n_split] = 1.0
    slopes = 2.0 ** (-8.0 * np.arange(1, N_HEADS + 1, dtype=np.float32) / N_HEADS)
    qaug = np.zeros((N_HEADS, LANES - HEAD_DIM, S), np.float32)
    for h in range(N_HEADS):
        for base, coef in ((0, slopes[h] * LOG2E), (n_split, slopes[h] * LOG2E * S)):
            rem = np.float32(coef)
            for r in range(n_split):
                piece = np.float32(np.asarray(rem).astype(BF16))
                qaug[h, base + r, :] = piece
                rem = np.float32(rem - piece)
    consts = [jnp.asarray(ksel, BF16), jnp.asarray(kpos), jnp.asarray(kblk), jnp.asarray(qaug)]
    scratch = [pltpu.VMEM((nb, N_HEADS, S, LANES), BF16),
               pltpu.VMEM((nb, N_HEADS, MOBA_VROWS, S), BF16),
               pltpu.VMEM((nbp, D_BRANCH), F32),
               pltpu.VMEM((N_HEADS, nbp + SUBLANES, S), F32),
               pltpu.VMEM((N_HEADS, LANES, S), BF16),
               pltpu.VMEM((N_HEADS, MOBA_VROWS, S), F32),
               pltpu.VMEM((N_HEADS, 1, S), F32),
               pltpu.VMEM((2, N_HEADS, 1, S), F32),
               pltpu.VMEM((2, N_HEADS, 1, S), F32),
               pltpu.VMEM((2, N_HEADS, S, S), F32),
               pltpu.VMEM((N_HEADS, S, S), BF16)]
    vmem = nb * N_HEADS * S * (LANES + MOBA_VROWS) * 2 + (24 << 20)
    cols = [(D_BRANCH, COL_MOBA + i) for i in range(4)]
    return _mixer_call(_moba_kernel, proj, cols, consts, scratch, n_batch, seq, "moba_mixer",
                       rows=S, vmem=vmem)


def _pack_w_in(w_in):
    depth, d, n_in = w_in.shape
    wt = jnp.transpose(w_in, (2, 0, 1)).reshape(n_in, depth, d // LANES, LANES)
    o = np.cumsum((0, D_BRANCH, SSM_XBC, N_HEADS) + (D_BRANCH,) * 12)
    z, xbc, dt, rest = wt[o[0]:o[1]], wt[o[1]:o[2]], wt[o[2]:o[3]], wt[o[3]:]
    used = SSM_XBC + D_BRANCH + rest.shape[0] + N_HEADS
    pad = jnp.zeros((N_PROJ - used,) + wt.shape[1:], wt.dtype)
    packed = jnp.concatenate([xbc, z, rest, dt, pad], axis=0)
    return jnp.transpose(packed, (1, 0, 2, 3)).reshape(depth, N_PROJ * (d // LANES), LANES)


def kernel(x, emb_ln_g, emb_ln_b, w_in, ssd_conv_w, ssd_conv_b, ssd_dt_bias, ssd_a_log, ssd_d, ssd_norm_w,
           hgrn_lb_logits, hgrn_norm_w, w_out, ln_g, ln_b):
    n_batch, seq, d_model = x.shape
    depth = w_in.shape[0]
    alpha = (2.0 * depth) ** 0.25
    assert seq % MIX_ROWS == 0 and seq % MOBA_BLOCK == 0

    lbs = jnp.cumsum(jax.nn.softmax(hgrn_lb_logits.astype(F32), axis=0), axis=0)
    lbs = lbs - lbs[0]

    wt_all = _pack_w_in(w_in.astype(F32))
    h = _layernorm(x.reshape(n_batch * seq, d_model), emb_ln_g, emb_ln_b)
    for l in range(depth):
        proj = _inproj(h, wt_all, l)
        ys = [_ssd(proj, ssd_conv_w[l], ssd_conv_b[l], ssd_dt_bias[l], ssd_a_log[l], ssd_d[l],
                   ssd_norm_w[l], n_batch, seq),
              _hgrn2(proj, lbs[l], hgrn_norm_w[l], n_batch, seq),
              _retention(proj, n_batch, seq),
              _moba(proj, n_batch, seq)]
        h = _outproj(h, ys, w_out[l].astype(BF16), ln_g[l], ln_b[l], alpha)
    return h.reshape(n_batch, seq, d_model)
```

```python
import functools
import math

import numpy as np
import jax
import jax.numpy as jnp
from jax import lax
from jax.experimental import pallas as pl
from jax.experimental.pallas import tpu as pltpu

F32 = jnp.float32
BF16 = jnp.bfloat16

D_BRANCH = 256
HEAD_DIM = 64
N_HEADS = 4
SSM_STATE = 64
SSM_GROUPS = 2
SSM_CONV = 4
SSM_XBC = D_BRANCH + 2 * SSM_GROUPS * SSM_STATE
MOBA_BLOCK = 256
MOBA_TOPK = 3
LN_EPS = 1e-5
RMS_EPS = 1e-6
NEG_BIG = -1e30
LOG2E = 1.4426950408889634

LANES = 128
SUBLANES = 8
VMEM_BYTES_V7X = 64 * 1024 * 1024

SSD_CHUNK = 128
RET_CHUNK = 128
HGRN_CHUNK = 64
MIX_ROWS = 512

N_PROJ = 4096
COL_Z = SSM_XBC // D_BRANCH
COL_HGRN = COL_Z + 1
COL_RET = COL_HGRN + 4
COL_MOBA = COL_RET + 4
COL_DT = (COL_MOBA + 4) * D_BRANCH // LANES


def _dot(a, b):
    return jnp.dot(a, b, preferred_element_type=F32)


def _dot_nt(a, b):
    return lax.dot_general(a, b, (((1,), (1,)), ((), ())), preferred_element_type=F32)


def _dot_tn(a, b):
    return lax.dot_general(a, b, (((0,), (0,)), ((), ())), preferred_element_type=F32)


def _split3(x):
    hi = x.astype(BF16)
    r1 = x - hi.astype(F32)
    mid = r1.astype(BF16)
    lo = (r1 - mid.astype(F32)).astype(BF16)
    return hi, mid, lo


def _split2(x):
    hi = x.astype(BF16)
    return hi, (x - hi.astype(F32)).astype(BF16)


def _sel_dot(sel, x):
    hi, lo = _split2(x)
    return _dot(sel, hi) + _dot(sel, lo)


def _dot_sel(x, sel):
    hi, lo = _split2(x)
    return _dot(hi, sel) + _dot(lo, sel)


def _sigmoid(x):
    return 1.0 / (1.0 + jnp.exp(-x))


def _silu(x):
    return x * _sigmoid(x)


def _softplus(x):
    return jnp.maximum(x, 0.0) + jnp.log(1.0 + jnp.exp(-jnp.abs(x)))


def _pair_masks():
    pm = np.zeros((LANES // HEAD_DIM, LANES), np.float32)
    for p in range(LANES // HEAD_DIM):
        pm[p, p * HEAD_DIM:(p + 1) * HEAD_DIM] = 1.0
    return jnp.asarray(pm, BF16)


def _pair_stack(xb, g, pmask):
    xg = xb[:, g * LANES:(g + 1) * LANES]
    return jnp.concatenate([xg * pmask[p:p + 1, :] for p in range(pmask.shape[0])], axis=0)


def _row_stack(x, masks):
    return jnp.concatenate([x * masks[h:h + 1, :] for h in range(masks.shape[0])], axis=0)


def _params(n_axes, vmem_bytes):
    return pltpu.CompilerParams(dimension_semantics=("arbitrary",) * n_axes,
                                vmem_limit_bytes=int(vmem_bytes))


def _ln_rows(x, g, b):
    mu = jnp.mean(x, axis=-1, keepdims=True)
    xc = x - mu
    var = jnp.mean(xc * xc, axis=-1, keepdims=True)
    return xc * lax.rsqrt(var + LN_EPS) * g + b


def _ln_kernel(x_ref, g_ref, b_ref, o_ref):
    o_ref[...] = _ln_rows(x_ref[...], g_ref[...], b_ref[...])


def _layernorm(x2, g, b, tm=512):
    m, d = x2.shape
    return pl.pallas_call(
        _ln_kernel,
        grid=(m // tm,),
        in_specs=[pl.BlockSpec((tm, d), lambda i: (i, 0)),
                  pl.BlockSpec((1, d), lambda i: (0, 0)),
                  pl.BlockSpec((1, d), lambda i: (0, 0))],
        out_specs=pl.BlockSpec((tm, d), lambda i: (i, 0)),
        out_shape=jax.ShapeDtypeStruct((m, d), F32),
        compiler_params=_params(1, 4 * tm * d * 4 + (4 << 20)),
        name="embed_layernorm",
    )(x2, g.reshape(1, d), b.reshape(1, d))


def _inproj_kernel(x_ref, wt_ref, o_ref, xb_ref, wtb_ref, *, tn):
    i, j = pl.program_id(0), pl.program_id(1)
    k_tiles = x_ref.shape[1] // LANES
    cols = pl.ds(pl.multiple_of(j * tn, tn), tn)

    @pl.when(j == 0)
    def _():
        xb_ref[...] = x_ref[...].astype(BF16)

    @pl.when(i == 0)
    def _():
        wtb_ref[cols, :] = jnp.concatenate(
            [wt_ref[pl.ds(kt, tn, stride=k_tiles), :] for kt in range(k_tiles)], axis=1).astype(BF16)

    o_ref[...] = _dot_nt(xb_ref[...], wtb_ref[cols, :])


def _inproj(h2, wt_all, layer, tm=1024, tn=1024):
    m, d = h2.shape
    rows = wt_all.shape[1]
    n = rows * LANES // d
    n_col_tiles = n // tn
    tile_rows = tn * d // LANES
    vmem = 2 * tm * d * 4 + 2 * tile_rows * LANES * 4 + 2 * tm * tn * 4 + tm * d * 2 + n * d * 2 + (6 << 20)
    return pl.pallas_call(
        functools.partial(_inproj_kernel, tn=tn),
        grid=(m // tm, n_col_tiles),
        in_specs=[pl.BlockSpec((tm, d), lambda i, j: (i, 0)),
                  pl.BlockSpec((None, tile_rows, LANES),
                               lambda i, j: (layer, jnp.where(i == 0, j, n_col_tiles - 1), 0))],
        out_specs=pl.BlockSpec((tm, tn), lambda i, j: (i, j)),
        out_shape=jax.ShapeDtypeStruct((m, n), F32),
        scratch_shapes=[pltpu.VMEM((tm, d), BF16), pltpu.VMEM((n, d), BF16)],
        compiler_params=_params(2, vmem),
        name="in_projection",
    )(h2, wt_all)


def _outproj_kernel(h_ref, y0_ref, y1_ref, y2_ref, y3_ref, w_ref, g_ref, b_ref, o_ref, *, alpha):
    ycat = jnp.concatenate([y0_ref[...], y1_ref[...], y2_ref[...], y3_ref[...]], axis=1)
    y = _dot(ycat.astype(BF16), w_ref[...])
    o_ref[...] = _ln_rows(alpha * h_ref[...] + y, g_ref[...], b_ref[...])


def _outproj(h2, ys, w_bf16, g, b, alpha, tm=512):
    m, d = h2.shape
    dm = w_bf16.shape[0]
    yspec = pl.BlockSpec((tm, D_BRANCH), lambda i: (i, 0))
    vmem = 2 * (2 * tm * d * 4 + tm * dm * 4 + dm * d * 2) + tm * d * 8 + (4 << 20)
    return pl.pallas_call(
        functools.partial(_outproj_kernel, alpha=alpha),
        grid=(m // tm,),
        in_specs=[pl.BlockSpec((tm, d), lambda i: (i, 0)), yspec, yspec, yspec, yspec,
                  pl.BlockSpec((dm, d), lambda i: (0, 0)),
                  pl.BlockSpec((1, d), lambda i: (0, 0)),
                  pl.BlockSpec((1, d), lambda i: (0, 0))],
        out_specs=pl.BlockSpec((tm, d), lambda i: (i, 0)),
        out_shape=jax.ShapeDtypeStruct((m, d), F32),
        compiler_params=_params(1, vmem),
        name="out_projection_layernorm",
    )(h2, *ys, w_bf16, g.reshape(1, d), b.reshape(1, d))


def _head_lane_masks():
    m = np.zeros((N_HEADS, D_BRANCH), np.float32)
    for h in range(N_HEADS):
        m[h, h * HEAD_DIM:(h + 1) * HEAD_DIM] = 1.0
    return m


def _head_block_diag(value=1.0):
    hm = _head_lane_masks()
    return (hm.T @ hm) * value


def _mixer_call(kernel, proj, col_blocks, consts, scratch, n_batch, seq, name, rows=MIX_ROWS, vmem=24 << 20):
    steps = seq // rows
    in_specs = [pl.BlockSpec((rows, w), functools.partial(lambda b, t, c: (b * steps + t, c), c=c))
                for (w, c) in col_blocks]
    in_specs += [pl.BlockSpec(a.shape, functools.partial(lambda b, t, nd: (0,) * nd, nd=a.ndim))
                 for a in consts]
    return pl.pallas_call(
        kernel,
        grid=(n_batch, steps),
        in_specs=in_specs,
        out_specs=pl.BlockSpec((rows, D_BRANCH), lambda b, t: (b * steps + t, 0)),
        out_shape=jax.ShapeDtypeStruct((n_batch * seq, D_BRANCH), F32),
        scratch_shapes=scratch,
        compiler_params=_params(2, vmem),
        name=name,
    )(*([proj] * len(col_blocks)), *consts)


def _ssd_kernel(xbc_ref, z_ref, dt_ref, convw_ref, convb_ref, dtb_ref, aneg_ref, dskip_ref, nw_ref,
                tri_ref, negmask_ref, exph_ref, pmask_ref, smask_ref, shift_ref,
                o_ref, xp_ref, state_ref, u_ref, sb_ref, ea_ref, cm_ref):
    C = SSD_CHUNK
    rows = xbc_ref.shape[0]
    n_chunks = rows // C
    halves = D_BRANCH // LANES
    pad = SUBLANES
    sls = [slice(c * C, (c + 1) * C) for c in range(n_chunks)]

    @pl.when(pl.program_id(1) == 0)
    def _():
        xp_ref[0:pad, :] = jnp.zeros((pad, SSM_XBC), F32)
        state_ref[...] = jnp.zeros_like(state_ref)

    xp_ref[pad:pad + rows, :] = xbc_ref[...]

    convw = convw_ref[...]
    pmask = pmask_ref[...]
    exph = exph_ref[...]
    xs, bmb, lcat, dth, ach = [], [], [], [], []
    for c in range(n_chunks):
        r0 = c * C
        ext = xp_ref[r0:r0 + C + pad, :]
        shifted = _sel_dot(shift_ref[...], ext)
        acc = convb_ref[...] + convw[SSM_CONV - 1:SSM_CONV, :] * ext[pad:, :]
        for j in range(SSM_CONV - 1):
            acc = acc + convw[j:j + 1, :] * shifted[j * C:(j + 1) * C, :]
        xbc = _silu(acc)
        xs.append(xbc[:, :D_BRANCH])
        bmb.append(xbc[:, D_BRANCH:D_BRANCH + LANES].astype(BF16))
        cm_ref[c] = xbc[:, D_BRANCH + LANES:].astype(BF16)
    for c in range(n_chunks):
        dt = _softplus(dt_ref[sls[c], :] + dtb_ref[...])
        a_cum = _sel_dot(tri_ref[...], dt * aneg_ref[...])
        a_cum_t = a_cum.T
        col = jnp.concatenate(
            [jnp.broadcast_to(a_cum[:, h:h + 1], (C, C)) for h in range(N_HEADS)], axis=1)
        row = jnp.concatenate([a_cum_t[h:h + 1, :] for h in range(N_HEADS)], axis=1)
        lcat.append(jnp.exp(col - row + negmask_ref[...]))
        dth.append(_dot_sel(dt, exph))
        ach.append(_dot_sel(a_cum, exph))
    heads_per_group = N_HEADS // SSM_GROUPS
    for c in range(n_chunks):
        sg = _dot_nt(cm_ref[c], _pair_stack(bmb[c], 0, pmask))
        s4 = jnp.concatenate([sg[:, (h // heads_per_group) * C:(h // heads_per_group + 1) * C]
                              for h in range(N_HEADS)], axis=1)
        p = (s4 * lcat[c]).astype(BF16)
        xdt = xs[c] * dth[c]
        xdtb = xdt.astype(BF16)
        y = jnp.concatenate(
            [_dot(p[:, g * 2 * C:(g + 1) * 2 * C], _pair_stack(xdtb, g, pmask)) for g in range(halves)], axis=1)
        o_ref[sls[c], :] = y + xs[c] * dskip_ref[...]
        alast = ach[c][C - 1:C, :]
        u_ref[c] = smask_ref[...] * _dot_tn(bmb[c], (xdt * jnp.exp(alast - ach[c])).astype(BF16))
        ea_ref[c] = jnp.exp(ach[c])

    xp_ref[0:pad, :] = xbc_ref[rows - pad:rows, :]

    st = state_ref[...]
    for c in range(n_chunks):
        sb_ref[c] = st.astype(BF16)
        st = st * ea_ref[c, C - 1:C, :] + u_ref[c]
    state_ref[...] = st

    for c in range(n_chunks):
        y = o_ref[sls[c], :] + _dot(cm_ref[c], sb_ref[c]) * ea_ref[c]
        g = y * _silu(z_ref[sls[c], :])
        outs = []
        for grp in range(SSM_GROUPS):
            gg = g[:, grp * LANES:(grp + 1) * LANES]
            ms = jnp.mean(gg * gg, axis=-1, keepdims=True)
            outs.append(gg * lax.rsqrt(ms + RMS_EPS))
        o_ref[sls[c], :] = jnp.concatenate(outs, axis=1) * nw_ref[...]


def _ssd_consts():
    C = SSD_CHUNK
    idx = np.arange(C)
    tri = (idx[:, None] >= idx[None, :]).astype(np.float32)
    negmask = np.tile(np.where(tri > 0, 0.0, NEG_BIG).astype(np.float32), (1, N_HEADS))
    exph = np.zeros((LANES, D_BRANCH), np.float32)
    exph[:N_HEADS] = _head_lane_masks()
    heads_per_group = N_HEADS // SSM_GROUPS
    smask = np.zeros((LANES, D_BRANCH), np.float32)
    for h in range(N_HEADS):
        g = h // heads_per_group
        smask[g * SSM_STATE:(g + 1) * SSM_STATE, h * HEAD_DIM:(h + 1) * HEAD_DIM] = 1.0
    shift = np.zeros(((SSM_CONV - 1) * C, C + SUBLANES), np.float32)
    for j in range(SSM_CONV - 1):
        shift[j * C + idx, SUBLANES - (SSM_CONV - 1) + j + idx] = 1.0
    return (jnp.asarray(tri, BF16), jnp.asarray(negmask), jnp.asarray(exph, BF16),
            _pair_masks(), jnp.asarray(smask), jnp.asarray(shift, BF16))


def _ssd(proj, conv_w, conv_b, dt_bias, a_log, d_skip, norm_w, n_batch, seq):
    lane_pad = LANES - N_HEADS
    dtb = jnp.pad(dt_bias.astype(F32), (0, lane_pad)).reshape(1, LANES)
    aneg = jnp.pad(-jnp.exp(a_log.astype(F32)), (0, lane_pad)).reshape(1, LANES)
    dskip = jnp.repeat(d_skip.astype(F32), HEAD_DIM).reshape(1, D_BRANCH)
    consts = [conv_w.astype(F32), conv_b.reshape(1, SSM_XBC).astype(F32), dtb, aneg, dskip,
              norm_w.reshape(1, D_BRANCH).astype(F32), *_ssd_consts()]
    n_chunks = MIX_ROWS // SSD_CHUNK
    state_shape = (SSM_GROUPS * SSM_STATE, D_BRANCH)
    scratch = [pltpu.VMEM((MIX_ROWS + SUBLANES, SSM_XBC), F32),
               pltpu.VMEM(state_shape, F32),
               pltpu.VMEM((n_chunks,) + state_shape, F32),
               pltpu.VMEM((n_chunks,) + state_shape, BF16),
               pltpu.VMEM((n_chunks, SSD_CHUNK, D_BRANCH), F32),
               pltpu.VMEM((n_chunks, SSD_CHUNK, LANES), BF16)]
    return _mixer_call(_ssd_kernel, proj, [(SSM_XBC, 0), (D_BRANCH, COL_Z), (LANES, COL_DT)],
                       consts, scratch, n_batch, seq, "ssd_mixer")


def _seg_mean(x, avg):
    return _dot_sel(x, avg)


def _ret_kernel(q_ref, k_ref, v_ref, g_ref, dcat_ref, qdec_ref, kdec_ref, cg_ref, pmask_ref,
                bd_ref, avg_ref, o_ref, state_ref, u_ref, rb_ref):
    C = RET_CHUNK
    rows = q_ref.shape[0]
    n_chunks = rows // C
    halves = D_BRANCH // LANES
    sls = [slice(c * C, (c + 1) * C) for c in range(n_chunks)]

    @pl.when(pl.program_id(1) == 0)
    def _():
        state_ref[...] = jnp.zeros_like(state_ref)

    pmask = pmask_ref[...]
    for c in range(n_chunks):
        qb = q_ref[sls[c], :].astype(BF16)
        kb = k_ref[sls[c], :].astype(BF16)
        vb = v_ref[sls[c], :].astype(BF16)
        outs = []
        for g in range(halves):
            scores = _dot_nt(qb[:, g * LANES:(g + 1) * LANES], _pair_stack(kb, g, pmask))
            p = (scores * dcat_ref[:, g * 2 * C:(g + 1) * 2 * C]).astype(BF16)
            outs.append(_dot(p, _pair_stack(vb, g, pmask)))
        o_ref[sls[c], :] = jnp.concatenate(outs, axis=1)
        u_ref[c] = bd_ref[...] * _dot_tn((k_ref[sls[c], :] * kdec_ref[...]).astype(BF16), vb)

    r = state_ref[...]
    for c in range(n_chunks):
        rb_ref[c] = r.astype(BF16)
        r = r * cg_ref[...] + u_ref[c]
    state_ref[...] = r

    for c in range(n_chunks):
        y = o_ref[sls[c], :] + _dot((q_ref[sls[c], :] * qdec_ref[...]).astype(BF16), rb_ref[c])
        mu = _seg_mean(y, avg_ref[...])
        yc = y - mu
        var = _seg_mean(yc * yc, avg_ref[...])
        o_ref[sls[c], :] = yc * lax.rsqrt(var + LN_EPS) * _silu(g_ref[sls[c], :])


def _ret_consts():
    C = RET_CHUNK
    log_g = jnp.log(1.0 - 2.0 ** (-5.0 - jnp.arange(N_HEADS, dtype=F32)))
    pos = jnp.arange(C, dtype=F32)
    dist = pos[:, None] - pos[None, :]
    intra = jnp.where(dist >= 0, jnp.exp(log_g[:, None, None] * jnp.maximum(dist, 0.0)), 0.0)
    scale = HEAD_DIM ** -0.5
    dcat = jnp.concatenate([intra[h] for h in range(N_HEADS)], axis=1) * scale
    k_decay = jnp.exp(log_g[:, None] * (C - 1.0 - pos)[None, :]) * scale
    q_decay = jnp.exp(log_g[:, None] * (pos + 1.0)[None, :])
    kdec = jnp.repeat(k_decay.T, HEAD_DIM, axis=1)
    qdec = jnp.repeat(q_decay.T, HEAD_DIM, axis=1)
    cg = jnp.repeat(jnp.exp(log_g * C), HEAD_DIM).reshape(1, D_BRANCH)
    return (dcat, qdec, kdec, cg, _pair_masks(),
            jnp.asarray(_head_block_diag()), jnp.asarray(_head_block_diag(1.0 / HEAD_DIM), BF16))


def _retention(proj, n_batch, seq):
    n_chunks = MIX_ROWS // RET_CHUNK
    scratch = [pltpu.VMEM((D_BRANCH, D_BRANCH), F32),
               pltpu.VMEM((n_chunks, D_BRANCH, D_BRANCH), F32),
               pltpu.VMEM((n_chunks, D_BRANCH, D_BRANCH), BF16)]
    cols = [(D_BRANCH, COL_RET + i) for i in range(4)]
    return _mixer_call(_ret_kernel, proj, cols, list(_ret_consts()), scratch, n_batch, seq, "retention_mixer")


def _hgrn_levels():
    return int(math.log2(HGRN_CHUNK))


def _hgrn_kernel(q_ref, f_ref, i_ref, g_ref, lb_ref, nw_ref, mstack_ref, lmask_ref, pmask_ref,
                 bd_ref, avg_ref, o_ref, state_ref, e_ref, u_ref, stb_ref):
    C = HGRN_CHUNK
    nl = _hgrn_levels()
    rows = q_ref.shape[0]
    halves = D_BRANCH // LANES

    @pl.when(pl.program_id(1) == 0)
    def _():
        state_ref[...] = jnp.zeros_like(state_ref)

    pmask = pmask_ref[...]
    lb = lb_ref[...]

    def head_scores(qb, kb):
        return jnp.concatenate(
            [_dot_nt(qb[:, g * LANES:(g + 1) * LANES], _pair_stack(kb, g, pmask)) for g in range(halves)], axis=1)

    n_chunks = rows // C
    sls = [slice(c * C, (c + 1) * C) for c in range(n_chunks)]

    k_rows = slice((nl + 2) * C, (nl + 3) * C)
    for c in range(n_chunks):
        f = lb + (1.0 - lb) * _sigmoid(f_ref[sls[c], :])
        e_ref[c, 0:(nl + 2) * C, :] = jnp.exp(_sel_dot(mstack_ref[...], jnp.log(f)))
        e_ref[c, k_rows, :] = 1.0 - f

    group = 2
    for c0 in range(0, n_chunks, group):
        cs = range(c0, min(c0 + group, n_chunks))
        att = {c: lmask_ref[nl] * head_scores(q_ref[sls[c], :].astype(BF16), e_ref[c, k_rows, :].astype(BF16))
               for c in cs}
        for j in range(nl):
            for c in cs:
                ej = e_ref[c, j * C:(j + 1) * C, :]
                att[c] = att[c] + lmask_ref[j] * head_scores((q_ref[sls[c], :] * ej).astype(BF16),
                                                             (e_ref[c, k_rows, :] * ej).astype(BF16))
        for c in cs:
            attb = att[c].astype(BF16)
            vb = i_ref[sls[c], :].astype(BF16)
            o_ref[sls[c], :] = jnp.concatenate(
                [_dot(attb[:, g * 2 * C:(g + 1) * 2 * C], _pair_stack(vb, g, pmask)) for g in range(halves)], axis=1)

    for c in range(n_chunks):
        er = e_ref[c, (nl + 1) * C:(nl + 2) * C, :]
        u_ref[c] = bd_ref[...] * _dot_tn(i_ref[sls[c], :].astype(BF16), (e_ref[c, k_rows, :] * er).astype(BF16))

    st = state_ref[...]
    for c in range(n_chunks):
        stb_ref[c] = st.astype(BF16)
        st = st * e_ref[c, (nl + 1) * C - 1:(nl + 1) * C, :] + u_ref[c]
    state_ref[...] = st

    for c in range(n_chunks):
        eb = e_ref[c, nl * C:(nl + 1) * C, :]
        o = o_ref[sls[c], :] + _dot_nt((q_ref[sls[c], :] * eb).astype(BF16), stb_ref[c])
        ms = _seg_mean(o * o, avg_ref[...])
        o_ref[sls[c], :] = o * lax.rsqrt(ms + RMS_EPS) * nw_ref[...] * _silu(g_ref[sls[c], :])


def _hgrn_consts():
    C = HGRN_CHUNK
    nl = _hgrn_levels()
    t = np.arange(C)[:, None]
    u = np.arange(C)[None, :]
    blocks, masks = [], []
    for j in range(nl):
        s = 1 << j
        bnd = (t // (2 * s)) * (2 * s) + s - 1
        upper = (t % (2 * s)) >= s
        m = np.where(upper, (u > bnd) & (u <= t), (u > t) & (u <= bnd))
        blocks.append(m.astype(np.float32))
        pair = (t // (2 * s)) == (u // (2 * s))
        masks.append((upper & pair & ((u % (2 * s)) < s)).astype(np.float32))
    blocks.append((u <= t).astype(np.float32))
    blocks.append((u > t).astype(np.float32))
    masks.append((u == t).astype(np.float32))
    mstack = np.concatenate(blocks, axis=0)
    lmask = np.stack([np.tile(m, (1, N_HEADS)) for m in masks])
    return (jnp.asarray(mstack, BF16), jnp.asarray(lmask), _pair_masks(),
            jnp.asarray(_head_block_diag()), jnp.asarray(_head_block_diag(1.0 / HEAD_DIM), BF16))


def _hgrn2(proj, lb, norm_w, n_batch, seq):
    consts = [lb.reshape(1, D_BRANCH).astype(F32), norm_w.reshape(1, D_BRANCH).astype(F32), *_hgrn_consts()]
    scratch = [pltpu.VMEM((D_BRANCH, D_BRANCH), F32),
               pltpu.VMEM((MIX_ROWS // HGRN_CHUNK, (_hgrn_levels() + 3) * HGRN_CHUNK, D_BRANCH), F32),
               pltpu.VMEM((MIX_ROWS // HGRN_CHUNK, D_BRANCH, D_BRANCH), F32),
               pltpu.VMEM((MIX_ROWS // HGRN_CHUNK, D_BRANCH, D_BRANCH), BF16)]
    cols = [(D_BRANCH, COL_HGRN + i) for i in range(4)]
    return _mixer_call(_hgrn_kernel, proj, cols, consts, scratch, n_batch, seq, "hgrn2_mixer")


MOBA_VROWS = 80


def _moba_kernel(q_ref, k_ref, v_ref, g_ref, ksel_ref, kpos_ref, kblk_ref, qaug_ref, o_ref,
                 kaug_ref, vt_ref, kmean_ref, sel_ref, qt_ref, acc_ref, m_ref, mcur_ref, alpha_ref,
                 s_ref, p_ref):
    S = MOBA_BLOCK
    i = pl.program_id(1)
    nbp = kmean_ref.shape[0]

    @pl.when(i == 0)
    def _():
        kmean_ref[...] = jnp.zeros_like(kmean_ref)

    k = k_ref[...]
    kb = k.astype(BF16)
    kmean_ref[pl.ds(i, 1), :] = jnp.mean(k, axis=0, keepdims=True)
    v_t = v_ref[...].T
    q_t = (q_ref[...] * (HEAD_DIM ** -0.5)).T
    ones_pad = (lax.broadcasted_iota(jnp.int32, (MOBA_VROWS - HEAD_DIM, S), 0) == 0).astype(F32)
    blk = lax.broadcasted_iota(jnp.int32, (nbp, S), 0)
    blk_f = blk.astype(F32)
    past = blk < i
    kmean = kmean_ref[...]
    own_rows = (lax.broadcasted_iota(jnp.int32, (SUBLANES, S), 0) == 0).astype(F32)
    kextra = kpos_ref[...] + i.astype(F32) * kblk_ref[...]
    for h in range(N_HEADS):
        hs = slice(h * HEAD_DIM, (h + 1) * HEAD_DIM)
        kaug_ref[i, h] = (_dot(kb, ksel_ref[h]) + kextra).astype(BF16)
        vt_ref[i, h] = jnp.concatenate([v_t[hs, :], ones_pad], axis=0).astype(BF16)
        qh_t = q_t[hs, :]
        qt_ref[h] = jnp.concatenate([qh_t * LOG2E, qaug_ref[h]], axis=0).astype(BF16)
        km_hi, km_mid, _ = _split3(kmean[:, hs])
        q_hi, q_mid, _ = _split3(qh_t)
        gate = _dot(km_hi, q_hi) + _dot(km_hi, q_mid) + _dot(km_mid, q_hi)
        gate = jnp.where(past, gate, -jnp.inf)
        sel = jnp.zeros((nbp, S), F32)
        for _ in range(MOBA_TOPK):
            best = jnp.max(gate, axis=0, keepdims=True)
            first = jnp.min(jnp.where(gate == best, blk_f, float(nbp)), axis=0, keepdims=True)
            hit = blk_f == first
            sel = jnp.where(hit, 1.0, sel)
            gate = jnp.where(hit, -jnp.inf, gate)
        sel_ref[h, 0:nbp, :] = jnp.where(past, sel, 0.0)
        sel_ref[h, nbp:nbp + SUBLANES, :] = own_rows

    def block_of(t):
        return jnp.where(t == 0, i, jnp.minimum(t - 1, i))

    def sel_row_of(t):
        return jnp.where(t == 0, nbp, jnp.where(t <= i, t - 1, nbp + 1))

    def track_max(h, slot, s_t, chosen):
        m_old = m_ref[h]
        m_new = jnp.where(chosen, jnp.maximum(m_old, jnp.max(s_t, axis=0, keepdims=True)), m_old)
        m_ref[h] = m_new
        mcur_ref[slot, h] = m_new
        alpha_ref[slot, h] = jnp.exp2(m_old - m_new)

    def score_phase(t, slot):
        n, row = block_of(t), sel_row_of(t)
        for h in range(N_HEADS):
            s_t = _dot(kaug_ref[n, h], qt_ref[h])
            s_ref[slot, h] = s_t
            track_max(h, slot, s_t, sel_ref[h, pl.ds(row, 1), :] > 0.0)

    def value_phase(t, slot):
        n, row = block_of(t), sel_row_of(t)
        for h in range(N_HEADS):
            p_ref[h] = jnp.exp2(s_ref[slot, h] - mcur_ref[slot, h]).astype(BF16)
        for h in range(N_HEADS):
            chosen = sel_ref[h, pl.ds(row, 1), :] > 0.0
            upd = _dot(vt_ref[n, h], p_ref[h])
            acc_ref[h] = alpha_ref[slot, h] * acc_ref[h] + jnp.where(chosen, upd, 0.0)

    key_pos = lax.broadcasted_iota(jnp.int32, (S, S), 0)
    qry_pos = lax.broadcasted_iota(jnp.int32, (S, S), 1)
    for h in range(N_HEADS):
        acc_ref[h] = jnp.zeros((MOBA_VROWS, S), F32)
        m_ref[h] = jnp.full((1, S), NEG_BIG, F32)
        s_t = jnp.where(key_pos <= qry_pos, _dot(kaug_ref[i, h], qt_ref[h]), NEG_BIG)
        s_ref[0, h] = s_t
        track_max(h, 0, s_t, True)

    def position_pair(j, carry):
        t0 = 2 * j
        score_phase(t0 + 1, 1)
        value_phase(t0, 0)
        score_phase(t0 + 2, 0)
        value_phase(t0 + 1, 1)
        return carry

    lax.fori_loop(0, (i + 2) // 2, position_pair, 0)

    outs = []
    for h in range(N_HEADS):
        acc = acc_ref[h]
        outs.append(acc[:HEAD_DIM, :] / acc[HEAD_DIM:HEAD_DIM + 1, :])
    o_ref[...] = jnp.concatenate(outs, axis=0).T * _silu(g_ref[...])


def _moba(proj, n_batch, seq):
    S = MOBA_BLOCK
    assert seq % S == 0
    nb = seq // S
    nbp = -(-nb // SUBLANES) * SUBLANES
    ksel = np.zeros((N_HEADS, D_BRANCH, LANES), np.float32)
    for h in range(N_HEADS):
        ksel[h, h * HEAD_DIM + np.arange(HEAD_DIM), np.arange(HEAD_DIM)] = 1.0
    n_split = 3
    kpos = np.zeros((S, LANES), np.float32)
    kblk = np.zeros((S, LANES), np.float32)
    kpos[:, HEAD_DIM:HEAD_DIM + n_split] = np.arange(S, dtype=np.float32)[:, None]
    kblk[:, HEAD_DIM + n_split:HEAD_DIM + 2 * n_split] = 1.0
    slopes = 2.0 ** (-8.0 * np.arange(1, N_HEADS + 1, dtype=np.float32) / N_HEADS)
    qaug = np.zeros((N_HEADS, LANES - HEAD_DIM, S), np.float32)
    for h in range(N_HEADS):
        for base, coef in ((0, slopes[h] * LOG2E), (n_split, slopes[h] * LOG2E * S)):
            rem = np.float32(coef)
            for r in range(n_split):
                piece = np.float32(np.asarray(rem).astype(BF16))
                qaug[h, base + r, :] = piece
                rem = np.float32(rem - piece)
    consts = [jnp.asarray(ksel, BF16), jnp.asarray(kpos), jnp.asarray(kblk), jnp.asarray(qaug)]
    scratch = [pltpu.VMEM((nb, N_HEADS, S, LANES), BF16),
               pltpu.VMEM((nb, N_HEADS, MOBA_VROWS, S), BF16),
               pltpu.VMEM((nbp, D_BRANCH), F32),
               pltpu.VMEM((N_HEADS, nbp + SUBLANES, S), F32),
               pltpu.VMEM((N_HEADS, LANES, S), BF16),
               pltpu.VMEM((N_HEADS, MOBA_VROWS, S), F32),
               pltpu.VMEM((N_HEADS, 1, S), F32),
               pltpu.VMEM((2, N_HEADS, 1, S), F32),
               pltpu.VMEM((2, N_HEADS, 1, S), F32),
               pltpu.VMEM((2, N_HEADS, S, S), F32),
               pltpu.VMEM((N_HEADS, S, S), BF16)]
    vmem = nb * N_HEADS * S * (LANES + MOBA_VROWS) * 2 + (24 << 20)
    cols = [(D_BRANCH, COL_MOBA + i) for i in range(4)]
    return _mixer_call(_moba_kernel, proj, cols, consts, scratch, n_batch, seq, "moba_mixer",
                       rows=S, vmem=vmem)


def _pack_w_in(w_in):
    depth, d, n_in = w_in.shape
    wt = jnp.transpose(w_in, (2, 0, 1)).reshape(n_in, depth, d // LANES, LANES)
    o = np.cumsum((0, D_BRANCH, SSM_XBC, N_HEADS) + (D_BRANCH,) * 12)
    z, xbc, dt, rest = wt[o[0]:o[1]], wt[o[1]:o[2]], wt[o[2]:o[3]], wt[o[3]:]
    used = SSM_XBC + D_BRANCH + rest.shape[0] + N_HEADS
    pad = jnp.zeros((N_PROJ - used,) + wt.shape[1:], wt.dtype)
    packed = jnp.concatenate([xbc, z, rest, dt, pad], axis=0)
    return jnp.transpose(packed, (1, 0, 2, 3)).reshape(depth, N_PROJ * (d // LANES), LANES)


def kernel(x, emb_ln_g, emb_ln_b, w_in, ssd_conv_w, ssd_conv_b, ssd_dt_bias, ssd_a_log, ssd_d, ssd_norm_w,
           hgrn_lb_logits, hgrn_norm_w, w_out, ln_g, ln_b):
    n_batch, seq, d_model = x.shape
    depth = w_in.shape[0]
    alpha = (2.0 * depth) ** 0.25
    assert seq % MIX_ROWS == 0 and seq % MOBA_BLOCK == 0

    lbs = jnp.cumsum(jax.nn.softmax(hgrn_lb_logits.astype(F32), axis=0), axis=0)
    lbs = lbs - lbs[0]

    wt_all = _pack_w_in(w_in.astype(F32))
    h = _layernorm(x.reshape(n_batch * seq, d_model), emb_ln_g, emb_ln_b)
    for l in range(depth):
        proj = _inproj(h, wt_all, l)
        ys = [_ssd(proj, ssd_conv_w[l], ssd_conv_b[l], ssd_dt_bias[l], ssd_a_log[l], ssd_d[l],
                   ssd_norm_w[l], n_batch, seq),
              _hgrn2(proj, lbs[l], hgrn_norm_w[l], n_batch, seq),
              _retention(proj, n_batch, seq),
              _moba(proj, n_batch, seq)]
        h = _outproj(h, ys, w_out[l].astype(BF16), ln_g[l], ln_b[l], alpha)
    return h.reshape(n_batch, seq, d_model)
```

```python
import functools
import math

import numpy as np
import jax
import jax.numpy as jnp
from jax import lax
from jax.experimental import pallas as pl
from jax.experimental.pallas import tpu as pltpu

F32 = jnp.float32
BF16 = jnp.bfloat16

D_BRANCH = 256
HEAD_DIM = 64
N_HEADS = 4
SSM_STATE = 64
SSM_GROUPS = 2
SSM_CONV = 4
SSM_XBC = D_BRANCH + 2 * SSM_GROUPS * SSM_STATE
MOBA_BLOCK = 256
MOBA_TOPK = 3
LN_EPS = 1e-5
RMS_EPS = 1e-6
NEG_BIG = -1e30
LOG2E = 1.4426950408889634

LANES = 128
SUBLANES = 8
BF16_ROWS = 16
VMEM_BYTES_V7X = 64 * 1024 * 1024

SSD_CHUNK = 128
RET_CHUNK = 128
HGRN_CHUNK = 64
MIX_ROWS = 512

N_PROJ = 4096
N_AUX = 512
COL_Z = SSM_XBC // D_BRANCH
COL_HGRN = COL_Z + 1
COL_RET = COL_HGRN + 3
COL_MOBA = COL_RET + 4
AUX_COL_F = 0
AUX_COL_DT = D_BRANCH // LANES


def _dot(a, b):
    return jnp.dot(a, b, preferred_element_type=F32)


def _dot_nt(a, b):
    return lax.dot_general(a, b, (((1,), (1,)), ((), ())), preferred_element_type=F32)


def _dot_tn(a, b):
    return lax.dot_general(a, b, (((0,), (0,)), ((), ())), preferred_element_type=F32)


def _split3(x):
    hi = x.astype(BF16)
    r1 = x - hi.astype(F32)
    mid = r1.astype(BF16)
    lo = (r1 - mid.astype(F32)).astype(BF16)
    return hi, mid, lo


def _split2(x):
    hi = x.astype(BF16)
    return hi, (x - hi.astype(F32)).astype(BF16)


def _sel_dot(sel, x):
    hi, lo = _split2(x)
    return _dot(sel, hi) + _dot(sel, lo)


def _dot_sel(x, sel):
    hi, lo = _split2(x)
    return _dot(hi, sel) + _dot(lo, sel)


def _sigmoid(x):
    return 1.0 / (1.0 + jnp.exp(-x))


def _silu(x):
    return x * _sigmoid(x)


def _softplus(x):
    return jnp.maximum(x, 0.0) + jnp.log(1.0 + jnp.exp(-jnp.abs(x)))


def _pair_masks():
    pm = np.zeros((LANES // HEAD_DIM, LANES), np.float32)
    for p in range(LANES // HEAD_DIM):
        pm[p, p * HEAD_DIM:(p + 1) * HEAD_DIM] = 1.0
    return jnp.asarray(pm, BF16)


def _pair_stack(xb, g, pmask):
    xg = xb[:, g * LANES:(g + 1) * LANES]
    return jnp.concatenate([xg * pmask[p:p + 1, :] for p in range(pmask.shape[0])], axis=0)


def _row_stack(x, masks):
    return jnp.concatenate([x * masks[h:h + 1, :] for h in range(masks.shape[0])], axis=0)


def _params(n_axes, vmem_bytes):
    return pltpu.CompilerParams(dimension_semantics=("arbitrary",) * n_axes,
                                vmem_limit_bytes=int(vmem_bytes))


def _ln_rows(x, g, b):
    mu = jnp.mean(x, axis=-1, keepdims=True)
    xc = x - mu
    var = jnp.mean(xc * xc, axis=-1, keepdims=True)
    return xc * lax.rsqrt(var + LN_EPS) * g + b


def _ln_kernel(x_ref, g_ref, b_ref, o_ref):
    o_ref[...] = _ln_rows(x_ref[...], g_ref[...], b_ref[...])


def _layernorm(x2, g, b, tm=512):
    m, d = x2.shape
    return pl.pallas_call(
        _ln_kernel,
        grid=(m // tm,),
        in_specs=[pl.BlockSpec((tm, d), lambda i: (i, 0)),
                  pl.BlockSpec((1, d), lambda i: (0, 0)),
                  pl.BlockSpec((1, d), lambda i: (0, 0))],
        out_specs=pl.BlockSpec((tm, d), lambda i: (i, 0)),
        out_shape=jax.ShapeDtypeStruct((m, d), F32),
        compiler_params=_params(1, 4 * tm * d * 4 + (4 << 20)),
        name="embed_layernorm",
    )(x2, g.reshape(1, d), b.reshape(1, d))


def _inproj_kernel(x_ref, wt_ref, o_ref, aux_ref, xb_ref, wtb_ref, *, tn):
    i, j = pl.program_id(0), pl.program_id(1)
    k_tiles = x_ref.shape[1] // LANES
    cols = pl.ds(pl.multiple_of(j * tn, tn), tn)

    @pl.when(j == 0)
    def _():
        xb_ref[...] = x_ref[...].astype(BF16)

    @pl.when(i == 0)
    def _():
        wtb_ref[cols, :] = jnp.concatenate(
            [wt_ref[pl.ds(kt, tn, stride=k_tiles), :] for kt in range(k_tiles)], axis=1).astype(BF16)

    acc = _dot_nt(xb_ref[...], wtb_ref[cols, :])
    o_ref[...] = acc.astype(BF16)

    @pl.when(j == pl.num_programs(1) - 1)
    def _():
        aux_ref[...] = acc[:, tn - aux_ref.shape[1]:]


def _inproj(h2, wt_all, layer, tm=1024, tn=1024):
    m, d = h2.shape
    rows = wt_all.shape[1]
    n = rows * LANES // d
    n_col_tiles = n // tn
    tile_rows = tn * d // LANES
    assert N_AUX <= tn
    vmem = (2 * tm * d * 4 + 2 * tile_rows * LANES * 4 + 2 * tm * tn * 2 + 2 * tm * N_AUX * 4 + tm * tn * 4
            + tm * d * 2 + n * d * 2 + (6 << 20))
    return pl.pallas_call(
        functools.partial(_inproj_kernel, tn=tn),
        grid=(m // tm, n_col_tiles),
        in_specs=[pl.BlockSpec((tm, d), lambda i, j: (i, 0)),
                  pl.BlockSpec((None, tile_rows, LANES),
                               lambda i, j: (layer, jnp.where(i == 0, j, n_col_tiles - 1), 0))],
        out_specs=[pl.BlockSpec((tm, tn), lambda i, j: (i, j)),
                   pl.BlockSpec((tm, N_AUX), lambda i, j: (i, 0))],
        out_shape=[jax.ShapeDtypeStruct((m, n), BF16), jax.ShapeDtypeStruct((m, N_AUX), F32)],
        scratch_shapes=[pltpu.VMEM((tm, d), BF16), pltpu.VMEM((n, d), BF16)],
        compiler_params=_params(2, vmem),
        name="in_projection",
    )(h2, wt_all)


def _outproj_kernel(h_ref, y0_ref, y1_ref, y2_ref, y3_ref, w_ref, g_ref, b_ref, o_ref, *, alpha):
    ycat = jnp.concatenate([y0_ref[...], y1_ref[...], y2_ref[...], y3_ref[...]], axis=1)
    y = _dot(ycat.astype(BF16), w_ref[...])
    o_ref[...] = _ln_rows(alpha * h_ref[...] + y, g_ref[...], b_ref[...])


def _outproj(h2, ys, w_bf16, g, b, alpha, tm=512):
    m, d = h2.shape
    dm = w_bf16.shape[0]
    yspec = pl.BlockSpec((tm, D_BRANCH), lambda i: (i, 0))
    vmem = 2 * (2 * tm * d * 4 + tm * dm * 4 + dm * d * 2) + tm * d * 8 + (4 << 20)
    return pl.pallas_call(
        functools.partial(_outproj_kernel, alpha=alpha),
        grid=(m // tm,),
        in_specs=[pl.BlockSpec((tm, d), lambda i: (i, 0)), yspec, yspec, yspec, yspec,
                  pl.BlockSpec((dm, d), lambda i: (0, 0)),
                  pl.BlockSpec((1, d), lambda i: (0, 0)),
                  pl.BlockSpec((1, d), lambda i: (0, 0))],
        out_specs=pl.BlockSpec((tm, d), lambda i: (i, 0)),
        out_shape=jax.ShapeDtypeStruct((m, d), F32),
        compiler_params=_params(1, vmem),
        name="out_projection_layernorm",
    )(h2, *ys, w_bf16, g.reshape(1, d), b.reshape(1, d))


def _head_lane_masks():
    m = np.zeros((N_HEADS, D_BRANCH), np.float32)
    for h in range(N_HEADS):
        m[h, h * HEAD_DIM:(h + 1) * HEAD_DIM] = 1.0
    return m


def _head_block_diag(value=1.0):
    hm = _head_lane_masks()
    return (hm.T @ hm) * value


def _mixer_call(kernel, col_blocks, consts, scratch, n_batch, seq, name, rows=MIX_ROWS, vmem=24 << 20):
    steps = seq // rows
    in_specs = [pl.BlockSpec((rows, w), functools.partial(lambda b, t, c: (b * steps + t, c), c=c))
                for (_, w, c) in col_blocks]
    in_specs += [pl.BlockSpec(a.shape, functools.partial(lambda b, t, nd: (0,) * nd, nd=a.ndim))
                 for a in consts]
    return pl.pallas_call(
        kernel,
        grid=(n_batch, steps),
        in_specs=in_specs,
        out_specs=pl.BlockSpec((rows, D_BRANCH), lambda b, t: (b * steps + t, 0)),
        out_shape=jax.ShapeDtypeStruct((n_batch * seq, D_BRANCH), F32),
        scratch_shapes=scratch,
        compiler_params=_params(2, vmem),
        name=name,
    )(*[a for (a, _, _) in col_blocks], *consts)


def _ssd_kernel(xbc_ref, z_ref, dt_ref, convw_ref, convb_ref, dtb_ref, aneg_ref, dskip_ref, nw_ref,
                tri_ref, negmask_ref, exph_ref, pmask_ref, smask_ref, shift_ref,
                o_ref, xp_ref, state_ref, u_ref, sb_ref, ea_ref, cm_ref):
    C = SSD_CHUNK
    rows = xbc_ref.shape[0]
    n_chunks = rows // C
    halves = D_BRANCH // LANES
    pad = BF16_ROWS
    sls = [slice(c * C, (c + 1) * C) for c in range(n_chunks)]

    @pl.when(pl.program_id(1) == 0)
    def _():
        xp_ref[0:pad, :] = jnp.zeros((pad, SSM_XBC), BF16)
        state_ref[...] = jnp.zeros_like(state_ref)

    xp_ref[pad:pad + rows, :] = xbc_ref[...]

    convw = convw_ref[...]
    pmask = pmask_ref[...]
    exph = exph_ref[...]
    xs, bmb, lcat, dth, ach = [], [], [], [], []
    for c in range(n_chunks):
        r0 = c * C
        ext = xp_ref[r0:r0 + C + pad, :]
        shifted = _dot(shift_ref[...], ext)
        acc = convb_ref[...] + convw[SSM_CONV - 1:SSM_CONV, :] * ext[pad:, :].astype(F32)
        for j in range(SSM_CONV - 1):
            acc = acc + convw[j:j + 1, :] * shifted[j * C:(j + 1) * C, :]
        xbc = _silu(acc)
        xs.append(xbc[:, :D_BRANCH])
        bmb.append(xbc[:, D_BRANCH:D_BRANCH + LANES].astype(BF16))
        cm_ref[c] = xbc[:, D_BRANCH + LANES:].astype(BF16)
    for c in range(n_chunks):
        dt = _softplus(dt_ref[sls[c], :] + dtb_ref[...])
        a_cum = _sel_dot(tri_ref[...], dt * aneg_ref[...])
        a_cum_t = a_cum.T
        col = jnp.concatenate(
            [jnp.broadcast_to(a_cum[:, h:h + 1], (C, C)) for h in range(N_HEADS)], axis=1)
        row = jnp.concatenate([a_cum_t[h:h + 1, :] for h in range(N_HEADS)], axis=1)
        lcat.append(jnp.exp(col - row + negmask_ref[...]))
        dth.append(_dot_sel(dt, exph))
        ach.append(_dot_sel(a_cum, exph))
    heads_per_group = N_HEADS // SSM_GROUPS
    for c in range(n_chunks):
        sg = _dot_nt(cm_ref[c], _pair_stack(bmb[c], 0, pmask))
        s4 = jnp.concatenate([sg[:, (h // heads_per_group) * C:(h // heads_per_group + 1) * C]
                              for h in range(N_HEADS)], axis=1)
        p = (s4 * lcat[c]).astype(BF16)
        xdt = xs[c] * dth[c]
        xdtb = xdt.astype(BF16)
        y = jnp.concatenate(
            [_dot(p[:, g * 2 * C:(g + 1) * 2 * C], _pair_stack(xdtb, g, pmask)) for g in range(halves)], axis=1)
        o_ref[sls[c], :] = y + xs[c] * dskip_ref[...]
        alast = ach[c][C - 1:C, :]
        u_ref[c] = smask_ref[...] * _dot_tn(bmb[c], (xdt * jnp.exp(alast - ach[c])).astype(BF16))
        ea_ref[c] = jnp.exp(ach[c])

    xp_ref[0:pad, :] = xbc_ref[rows - pad:rows, :]

    st = state_ref[...]
    for c in range(n_chunks):
        sb_ref[c] = st.astype(BF16)
        st = st * ea_ref[c, C - 1:C, :] + u_ref[c]
    state_ref[...] = st

    for c in range(n_chunks):
        y = o_ref[sls[c], :] + _dot(cm_ref[c], sb_ref[c]) * ea_ref[c]
        g = y * _silu(z_ref[sls[c], :].astype(F32))
        outs = []
        for grp in range(SSM_GROUPS):
            gg = g[:, grp * LANES:(grp + 1) * LANES]
            ms = jnp.mean(gg * gg, axis=-1, keepdims=True)
            outs.append(gg * lax.rsqrt(ms + RMS_EPS))
        o_ref[sls[c], :] = jnp.concatenate(outs, axis=1) * nw_ref[...]


def _ssd_consts():
    C = SSD_CHUNK
    idx = np.arange(C)
    tri = (idx[:, None] >= idx[None, :]).astype(np.float32)
    negmask = np.tile(np.where(tri > 0, 0.0, NEG_BIG).astype(np.float32), (1, N_HEADS))
    exph = np.zeros((LANES, D_BRANCH), np.float32)
    exph[:N_HEADS] = _head_lane_masks()
    heads_per_group = N_HEADS // SSM_GROUPS
    smask = np.zeros((LANES, D_BRANCH), np.float32)
    for h in range(N_HEADS):
        g = h // heads_per_group
        smask[g * SSM_STATE:(g + 1) * SSM_STATE, h * HEAD_DIM:(h + 1) * HEAD_DIM] = 1.0
    shift = np.zeros(((SSM_CONV - 1) * C, C + BF16_ROWS), np.float32)
    for j in range(SSM_CONV - 1):
        shift[j * C + idx, BF16_ROWS - (SSM_CONV - 1) + j + idx] = 1.0
    return (jnp.asarray(tri, BF16), jnp.asarray(negmask), jnp.asarray(exph, BF16),
            _pair_masks(), jnp.asarray(smask), jnp.asarray(shift, BF16))


def _ssd(proj, aux, conv_w, conv_b, dt_bias, a_log, d_skip, norm_w, n_batch, seq):
    lane_pad = LANES - N_HEADS
    dtb = jnp.pad(dt_bias.astype(F32), (0, lane_pad)).reshape(1, LANES)
    aneg = jnp.pad(-jnp.exp(a_log.astype(F32)), (0, lane_pad)).reshape(1, LANES)
    dskip = jnp.repeat(d_skip.astype(F32), HEAD_DIM).reshape(1, D_BRANCH)
    consts = [conv_w.astype(F32), conv_b.reshape(1, SSM_XBC).astype(F32), dtb, aneg, dskip,
              norm_w.reshape(1, D_BRANCH).astype(F32), *_ssd_consts()]
    n_chunks = MIX_ROWS // SSD_CHUNK
    state_shape = (SSM_GROUPS * SSM_STATE, D_BRANCH)
    scratch = [pltpu.VMEM((MIX_ROWS + BF16_ROWS, SSM_XBC), BF16),
               pltpu.VMEM(state_shape, F32),
               pltpu.VMEM((n_chunks,) + state_shape, F32),
               pltpu.VMEM((n_chunks,) + state_shape, BF16),
               pltpu.VMEM((n_chunks, SSD_CHUNK, D_BRANCH), F32),
               pltpu.VMEM((n_chunks, SSD_CHUNK, LANES), BF16)]
    cols = [(proj, SSM_XBC, 0), (proj, D_BRANCH, COL_Z), (aux, LANES, AUX_COL_DT)]
    return _mixer_call(_ssd_kernel, cols, consts, scratch, n_batch, seq, "ssd_mixer")


def _seg_mean(x, avg):
    return _dot_sel(x, avg)


def _ret_kernel(q_ref, k_ref, v_ref, g_ref, dcat_ref, qdec_ref, kdec_ref, cg_ref, pmask_ref,
                bd_ref, avg_ref, o_ref, state_ref, u_ref, rb_ref):
    C = RET_CHUNK
    rows = q_ref.shape[0]
    n_chunks = rows // C
    halves = D_BRANCH // LANES
    sls = [slice(c * C, (c + 1) * C) for c in range(n_chunks)]

    @pl.when(pl.program_id(1) == 0)
    def _():
        state_ref[...] = jnp.zeros_like(state_ref)

    pmask = pmask_ref[...]
    for c in range(n_chunks):
        qb, kb, vb = q_ref[sls[c], :], k_ref[sls[c], :], v_ref[sls[c], :]
        outs = []
        for g in range(halves):
            scores = _dot_nt(qb[:, g * LANES:(g + 1) * LANES], _pair_stack(kb, g, pmask))
            p = (scores * dcat_ref[:, g * 2 * C:(g + 1) * 2 * C]).astype(BF16)
            outs.append(_dot(p, _pair_stack(vb, g, pmask)))
        o_ref[sls[c], :] = jnp.concatenate(outs, axis=1)
        u_ref[c] = bd_ref[...] * _dot_tn((kb.astype(F32) * kdec_ref[...]).astype(BF16), vb)

    r = state_ref[...]
    for c in range(n_chunks):
        rb_ref[c] = r.astype(BF16)
        r = r * cg_ref[...] + u_ref[c]
    state_ref[...] = r

    for c in range(n_chunks):
        y = o_ref[sls[c], :] + _dot((q_ref[sls[c], :].astype(F32) * qdec_ref[...]).astype(BF16), rb_ref[c])
        mu = _seg_mean(y, avg_ref[...])
        yc = y - mu
        var = _seg_mean(yc * yc, avg_ref[...])
        o_ref[sls[c], :] = yc * lax.rsqrt(var + LN_EPS) * _silu(g_ref[sls[c], :].astype(F32))


def _ret_consts():
    C = RET_CHUNK
    log_g = jnp.log(1.0 - 2.0 ** (-5.0 - jnp.arange(N_HEADS, dtype=F32)))
    pos = jnp.arange(C, dtype=F32)
    dist = pos[:, None] - pos[None, :]
    intra = jnp.where(dist >= 0, jnp.exp(log_g[:, None, None] * jnp.maximum(dist, 0.0)), 0.0)
    scale = HEAD_DIM ** -0.5
    dcat = jnp.concatenate([intra[h] for h in range(N_HEADS)], axis=1) * scale
    k_decay = jnp.exp(log_g[:, None] * (C - 1.0 - pos)[None, :]) * scale
    q_decay = jnp.exp(log_g[:, None] * (pos + 1.0)[None, :])
    kdec = jnp.repeat(k_decay.T, HEAD_DIM, axis=1)
    qdec = jnp.repeat(q_decay.T, HEAD_DIM, axis=1)
    cg = jnp.repeat(jnp.exp(log_g * C), HEAD_DIM).reshape(1, D_BRANCH)
    return (dcat, qdec, kdec, cg, _pair_masks(),
            jnp.asarray(_head_block_diag()), jnp.asarray(_head_block_diag(1.0 / HEAD_DIM), BF16))


def _retention(proj, n_batch, seq):
    n_chunks = MIX_ROWS // RET_CHUNK
    scratch = [pltpu.VMEM((D_BRANCH, D_BRANCH), F32),
               pltpu.VMEM((n_chunks, D_BRANCH, D_BRANCH), F32),
               pltpu.VMEM((n_chunks, D_BRANCH, D_BRANCH), BF16)]
    cols = [(proj, D_BRANCH, COL_RET + i) for i in range(4)]
    return _mixer_call(_ret_kernel, cols, list(_ret_consts()), scratch, n_batch, seq, "retention_mixer")


def _hgrn_levels():
    return int(math.log2(HGRN_CHUNK))


def _hgrn_kernel(q_ref, i_ref, g_ref, f_ref, lb_ref, nw_ref, mstack_ref, lmask_ref, pmask_ref,
                 bd_ref, avg_ref, o_ref, state_ref, e_ref, u_ref, stb_ref):
    C = HGRN_CHUNK
    nl = _hgrn_levels()
    rows = q_ref.shape[0]
    halves = D_BRANCH // LANES

    @pl.when(pl.program_id(1) == 0)
    def _():
        state_ref[...] = jnp.zeros_like(state_ref)

    pmask = pmask_ref[...]
    lb = lb_ref[...]

    def head_scores(qb, kb):
        return jnp.concatenate(
            [_dot_nt(qb[:, g * LANES:(g + 1) * LANES], _pair_stack(kb, g, pmask)) for g in range(halves)], axis=1)

    n_chunks = rows // C
    sls = [slice(c * C, (c + 1) * C) for c in range(n_chunks)]

    k_rows = slice((nl + 2) * C, (nl + 3) * C)
    for c in range(n_chunks):
        f = lb + (1.0 - lb) * _sigmoid(f_ref[sls[c], :])
        e_ref[c, 0:(nl + 2) * C, :] = jnp.exp(_sel_dot(mstack_ref[...], jnp.log(f)))
        e_ref[c, k_rows, :] = 1.0 - f

    group = 2
    for c0 in range(0, n_chunks, group):
        cs = range(c0, min(c0 + group, n_chunks))
        att = {c: lmask_ref[nl] * head_scores(q_ref[sls[c], :], e_ref[c, k_rows, :].astype(BF16))
               for c in cs}
        for j in range(nl):
            for c in cs:
                ej = e_ref[c, j * C:(j + 1) * C, :]
                att[c] = att[c] + lmask_ref[j] * head_scores((q_ref[sls[c], :].astype(F32) * ej).astype(BF16),
                                                             (e_ref[c, k_rows, :] * ej).astype(BF16))
        for c in cs:
            attb = att[c].astype(BF16)
            vb = i_ref[sls[c], :]
            o_ref[sls[c], :] = jnp.concatenate(
                [_dot(attb[:, g * 2 * C:(g + 1) * 2 * C], _pair_stack(vb, g, pmask)) for g in range(halves)], axis=1)

    for c in range(n_chunks):
        er = e_ref[c, (nl + 1) * C:(nl + 2) * C, :]
        u_ref[c] = bd_ref[...] * _dot_tn(i_ref[sls[c], :], (e_ref[c, k_rows, :] * er).astype(BF16))

    st = state_ref[...]
    for c in range(n_chunks):
        stb_ref[c] = st.astype(BF16)
        st = st * e_ref[c, (nl + 1) * C - 1:(nl + 1) * C, :] + u_ref[c]
    state_ref[...] = st

    for c in range(n_chunks):
        eb = e_ref[c, nl * C:(nl + 1) * C, :]
        o = o_ref[sls[c], :] + _dot_nt((q_ref[sls[c], :].astype(F32) * eb).astype(BF16), stb_ref[c])
        ms = _seg_mean(o * o, avg_ref[...])
        o_ref[sls[c], :] = o * lax.rsqrt(ms + RMS_EPS) * nw_ref[...] * _silu(g_ref[sls[c], :].astype(F32))


def _hgrn_consts():
    C = HGRN_CHUNK
    nl = _hgrn_levels()
    t = np.arange(C)[:, None]
    u = np.arange(C)[None, :]
    blocks, masks = [], []
    for j in range(nl):
        s = 1 << j
        bnd = (t // (2 * s)) * (2 * s) + s - 1
        upper = (t % (2 * s)) >= s
        m = np.where(upper, (u > bnd) & (u <= t), (u > t) & (u <= bnd))
        blocks.append(m.astype(np.float32))
        pair = (t // (2 * s)) == (u // (2 * s))
        masks.append((upper & pair & ((u % (2 * s)) < s)).astype(np.float32))
    blocks.append((u <= t).astype(np.float32))
    blocks.append((u > t).astype(np.float32))
    masks.append((u == t).astype(np.float32))
    mstack = np.concatenate(blocks, axis=0)
    lmask = np.stack([np.tile(m, (1, N_HEADS)) for m in masks])
    return (jnp.asarray(mstack, BF16), jnp.asarray(lmask), _pair_masks(),
            jnp.asarray(_head_block_diag()), jnp.asarray(_head_block_diag(1.0 / HEAD_DIM), BF16))


def _hgrn2(proj, aux, lb, norm_w, n_batch, seq):
    consts = [lb.reshape(1, D_BRANCH).astype(F32), norm_w.reshape(1, D_BRANCH).astype(F32), *_hgrn_consts()]
    scratch = [pltpu.VMEM((D_BRANCH, D_BRANCH), F32),
               pltpu.VMEM((MIX_ROWS // HGRN_CHUNK, (_hgrn_levels() + 3) * HGRN_CHUNK, D_BRANCH), F32),
               pltpu.VMEM((MIX_ROWS // HGRN_CHUNK, D_BRANCH, D_BRANCH), F32),
               pltpu.VMEM((MIX_ROWS // HGRN_CHUNK, D_BRANCH, D_BRANCH), BF16)]
    cols = [(proj, D_BRANCH, COL_HGRN + i) for i in range(3)] + [(aux, D_BRANCH, AUX_COL_F)]
    return _mixer_call(_hgrn_kernel, cols, consts, scratch, n_batch, seq, "hgrn2_mixer")


MOBA_VROWS = 80


def _moba_kernel(q_ref, k_ref, v_ref, g_ref, ksel_ref, kpos_ref, kblk_ref, qaug_ref, o_ref,
                 kaug_ref, vt_ref, kmean_ref, sel_ref, qt_ref, acc_ref, m_ref, mcur_ref, alpha_ref,
                 s_ref, p_ref):
    S = MOBA_BLOCK
    i = pl.program_id(1)
    nbp = kmean_ref.shape[0]

    @pl.when(i == 0)
    def _():
        kmean_ref[...] = jnp.zeros_like(kmean_ref)

    kb = k_ref[...]
    kmean_ref[pl.ds(i, 1), :] = jnp.mean(kb.astype(F32), axis=0, keepdims=True)
    v_t = v_ref[...].astype(F32).T
    q_t = (q_ref[...].astype(F32) * (HEAD_DIM ** -0.5)).T
    ones_pad = (lax.broadcasted_iota(jnp.int32, (MOBA_VROWS - HEAD_DIM, S), 0) == 0).astype(F32)
    blk = lax.broadcasted_iota(jnp.int32, (nbp, S), 0)
    blk_f = blk.astype(F32)
    past = blk < i
    kmean = kmean_ref[...]
    own_rows = (lax.broadcasted_iota(jnp.int32, (SUBLANES, S), 0) == 0).astype(F32)
    kextra = kpos_ref[...] + i.astype(F32) * kblk_ref[...]
    for h in range(N_HEADS):
        hs = slice(h * HEAD_DIM, (h + 1) * HEAD_DIM)
        kaug_ref[i, h] = (_dot(kb, ksel_ref[h]) + kextra).astype(BF16)
        vt_ref[i, h] = jnp.concatenate([v_t[hs, :], ones_pad], axis=0).astype(BF16)
        qh_t = q_t[hs, :]
        qt_ref[h] = jnp.concatenate([qh_t * LOG2E, qaug_ref[h]], axis=0).astype(BF16)
        km_hi, km_mid, _ = _split3(kmean[:, hs])
        q_hi, q_mid, _ = _split3(qh_t)
        gate = _dot(km_hi, q_hi) + _dot(km_hi, q_mid) + _dot(km_mid, q_hi)
        gate = jnp.where(past, gate, -jnp.inf)
        sel = jnp.zeros((nbp, S), F32)
        for _ in range(MOBA_TOPK):
            best = jnp.max(gate, axis=0, keepdims=True)
            first = jnp.min(jnp.where(gate == best, blk_f, float(nbp)), axis=0, keepdims=True)
            hit = blk_f == first
            sel = jnp.where(hit, 1.0, sel)
            gate = jnp.where(hit, -jnp.inf, gate)
        sel_ref[h, 0:nbp, :] = jnp.where(past, sel, 0.0)
        sel_ref[h, nbp:nbp + SUBLANES, :] = own_rows

    def block_of(t):
        return jnp.where(t == 0, i, jnp.minimum(t - 1, i))

    def sel_row_of(t):
        return jnp.where(t == 0, nbp, jnp.where(t <= i, t - 1, nbp + 1))

    def track_max(h, slot, s_t, chosen):
        m_old = m_ref[h]
        m_new = jnp.where(chosen, jnp.maximum(m_old, jnp.max(s_t, axis=0, keepdims=True)), m_old)
        m_ref[h] = m_new
        mcur_ref[slot, h] = m_new
        alpha_ref[slot, h] = jnp.exp2(m_old - m_new)

    def score_phase(t, slot):
        n, row = block_of(t), sel_row_of(t)
        for h in range(N_HEADS):
            s_t = _dot(kaug_ref[n, h], qt_ref[h])
            s_ref[slot, h] = s_t
            track_max(h, slot, s_t, sel_ref[h, pl.ds(row, 1), :] > 0.0)

    def value_phase(t, slot):
        n, row = block_of(t), sel_row_of(t)
        for h in range(N_HEADS):
            p_ref[h] = jnp.exp2(s_ref[slot, h] - mcur_ref[slot, h]).astype(BF16)
        for h in range(N_HEADS):
            chosen = sel_ref[h, pl.ds(row, 1), :] > 0.0
            upd = _dot(vt_ref[n, h], p_ref[h])
            acc_ref[h] = alpha_ref[slot, h] * acc_ref[h] + jnp.where(chosen, upd, 0.0)

    key_pos = lax.broadcasted_iota(jnp.int32, (S, S), 0)
    qry_pos = lax.broadcasted_iota(jnp.int32, (S, S), 1)
    for h in range(N_HEADS):
        acc_ref[h] = jnp.zeros((MOBA_VROWS, S), F32)
        m_ref[h] = jnp.full((1, S), NEG_BIG, F32)
        s_t = jnp.where(key_pos <= qry_pos, _dot(kaug_ref[i, h], qt_ref[h]), NEG_BIG)
        s_ref[0, h] = s_t
        track_max(h, 0, s_t, True)

    def position_pair(j, carry):
        t0 = 2 * j
        score_phase(t0 + 1, 1)
        value_phase(t0, 0)
        score_phase(t0 + 2, 0)
        value_phase(t0 + 1, 1)
        return carry

    lax.fori_loop(0, (i + 2) // 2, position_pair, 0)

    outs = []
    for h in range(N_HEADS):
        acc = acc_ref[h]
        outs.append(acc[:HEAD_DIM, :] / acc[HEAD_DIM:HEAD_DIM + 1, :])
    o_ref[...] = jnp.concatenate(outs, axis=0).T * _silu(g_ref[...].astype(F32))


def _moba(proj, n_batch, seq):
    S = MOBA_BLOCK
    assert seq % S == 0
    nb = seq // S
    nbp = -(-nb // SUBLANES) * SUBLANES
    ksel = np.zeros((N_HEADS, D_BRANCH, LANES), np.float32)
    for h in range(N_HEADS):
        ksel[h, h * HEAD_DIM + np.arange(HEAD_DIM), np.arange(HEAD_DIM)] = 1.0
    n_split = 3
    kpos = np.zeros((S, LANES), np.float32)
    kblk = np.zeros((S, LANES), np.float32)
    kpos[:, HEAD_DIM:HEAD_DIM + n_split] = np.arange(S, dtype=np.float32)[:, None]
    kblk[:, HEAD_DIM + n_split:HEAD_DIM + 2 * n_split] = 1.0
    slopes = 2.0 ** (-8.0 * np.arange(1, N_HEADS + 1, dtype=np.float32) / N_HEADS)
    qaug = np.zeros((N_HEADS, LANES - HEAD_DIM, S), np.float32)
    for h in range(N_HEADS):
        for base, coef in ((0, slopes[h] * LOG2E), (n_split, slopes[h] * LOG2E * S)):
            rem = np.float32(coef)
            for r in range(n_split):
                piece = np.float32(np.asarray(rem).astype(BF16))
                qaug[h, base + r, :] = piece
                rem = np.float32(rem - piece)
    consts = [jnp.asarray(ksel, BF16), jnp.asarray(kpos), jnp.asarray(kblk), jnp.asarray(qaug)]
    scratch = [pltpu.VMEM((nb, N_HEADS, S, LANES), BF16),
               pltpu.VMEM((nb, N_HEADS, MOBA_VROWS, S), BF16),
               pltpu.VMEM((nbp, D_BRANCH), F32),
               pltpu.VMEM((N_HEADS, nbp + SUBLANES, S), F32),
               pltpu.VMEM((N_HEADS, LANES, S), BF16),
               pltpu.VMEM((N_HEADS, MOBA_VROWS, S), F32),
               pltpu.VMEM((N_HEADS, 1, S), F32),
               pltpu.VMEM((2, N_HEADS, 1, S), F32),
               pltpu.VMEM((2, N_HEADS, 1, S), F32),
               pltpu.VMEM((2, N_HEADS, S, S), F32),
               pltpu.VMEM((N_HEADS, S, S), BF16)]
    vmem = nb * N_HEADS * S * (LANES + MOBA_VROWS) * 2 + (24 << 20)
    cols = [(proj, D_BRANCH, COL_MOBA + i) for i in range(4)]
    return _mixer_call(_moba_kernel, cols, consts, scratch, n_batch, seq, "moba_mixer", rows=S, vmem=vmem)


def _pack_w_in(w_in):
    depth, d, n_in = w_in.shape
    wt = jnp.transpose(w_in, (2, 0, 1)).reshape(n_in, depth, d // LANES, LANES)
    o = np.cumsum((0, D_BRANCH, SSM_XBC, N_HEADS) + (D_BRANCH,) * 12)
    z, xbc, dt = wt[o[0]:o[1]], wt[o[1]:o[2]], wt[o[2]:o[3]]
    hq, hf, hig, ret_moba = wt[o[3]:o[4]], wt[o[4]:o[5]], wt[o[5]:o[7]], wt[o[7]:]
    pad = jnp.zeros((N_PROJ - n_in,) + wt.shape[1:], wt.dtype)
    packed = jnp.concatenate([xbc, z, hq, hig, ret_moba, hf, dt, pad], axis=0)
    return jnp.transpose(packed, (1, 0, 2, 3)).reshape(depth, N_PROJ * (d // LANES), LANES)


def kernel(x, emb_ln_g, emb_ln_b, w_in, ssd_conv_w, ssd_conv_b, ssd_dt_bias, ssd_a_log, ssd_d, ssd_norm_w,
           hgrn_lb_logits, hgrn_norm_w, w_out, ln_g, ln_b):
    n_batch, seq, d_model = x.shape
    depth = w_in.shape[0]
    alpha = (2.0 * depth) ** 0.25
    assert seq % MIX_ROWS == 0 and seq % MOBA_BLOCK == 0

    lbs = jnp.cumsum(jax.nn.softmax(hgrn_lb_logits.astype(F32), axis=0), axis=0)
    lbs = lbs - lbs[0]

    wt_all = _pack_w_in(w_in.astype(F32))
    h = _layernorm(x.reshape(n_batch * seq, d_model), emb_ln_g, emb_ln_b)
    for l in range(depth):
        proj, aux = _inproj(h, wt_all, l)
        ys = [_ssd(proj, aux, ssd_conv_w[l], ssd_conv_b[l], ssd_dt_bias[l], ssd_a_log[l], ssd_d[l],
                   ssd_norm_w[l], n_batch, seq),
              _hgrn2(proj, aux, lbs[l], hgrn_norm_w[l], n_batch, seq),
              _retention(proj, n_batch, seq),
              _moba(proj, n_batch, seq)]
        h = _outproj(h, ys, w_out[l].astype(BF16), ln_g[l], ln_b[l], alpha)
    return h.reshape(n_batch, seq, d_model)
```

```python
import functools
import math

import numpy as np
import jax
import jax.numpy as jnp
from jax import lax
from jax.experimental import pallas as pl
from jax.experimental.pallas import tpu as pltpu

F32 = jnp.float32
BF16 = jnp.bfloat16

D_BRANCH = 256
HEAD_DIM = 64
N_HEADS = 4
SSM_STATE = 64
SSM_GROUPS = 2
SSM_CONV = 4
SSM_XBC = D_BRANCH + 2 * SSM_GROUPS * SSM_STATE
MOBA_BLOCK = 256
MOBA_TOPK = 3
LN_EPS = 1e-5
RMS_EPS = 1e-6
NEG_BIG = -1e30
LOG2E = 1.4426950408889634

LANES = 128
SUBLANES = 8
BF16_ROWS = 16
VMEM_BYTES_V7X = 64 * 1024 * 1024

SSD_CHUNK = 128
RET_CHUNK = 256
HGRN_CHUNK = 128
MIX_ROWS = 1024

N_PROJ = 4096
N_AUX = 512
COL_Z = SSM_XBC // D_BRANCH
COL_HGRN = COL_Z + 1
COL_RET = COL_HGRN + 3
COL_MOBA = COL_RET + 4
AUX_COL_F = 0
AUX_COL_DT = D_BRANCH // LANES


def _dot(a, b):
    return jnp.dot(a, b, preferred_element_type=F32)


def _dot_nt(a, b):
    return lax.dot_general(a, b, (((1,), (1,)), ((), ())), preferred_element_type=F32)


def _dot_tn(a, b):
    return lax.dot_general(a, b, (((0,), (0,)), ((), ())), preferred_element_type=F32)


def _split3(x):
    hi = x.astype(BF16)
    r1 = x - hi.astype(F32)
    mid = r1.astype(BF16)
    lo = (r1 - mid.astype(F32)).astype(BF16)
    return hi, mid, lo


def _split2(x):
    hi = x.astype(BF16)
    return hi, (x - hi.astype(F32)).astype(BF16)


def _sel_dot(sel, x):
    hi, lo = _split2(x)
    return _dot(sel, hi) + _dot(sel, lo)


def _dot_sel(x, sel):
    hi, lo = _split2(x)
    return _dot(hi, sel) + _dot(lo, sel)


def _sigmoid(x):
    return 1.0 / (1.0 + jnp.exp(-x))


def _silu(x):
    return x * _sigmoid(x)


def _softplus(x):
    return jnp.maximum(x, 0.0) + jnp.log(1.0 + jnp.exp(-jnp.abs(x)))


def _pair_masks():
    pm = np.zeros((LANES // HEAD_DIM, LANES), np.float32)
    for p in range(LANES // HEAD_DIM):
        pm[p, p * HEAD_DIM:(p + 1) * HEAD_DIM] = 1.0
    return jnp.asarray(pm, BF16)


def _pair_stack(xb, g, pmask):
    xg = xb[:, g * LANES:(g + 1) * LANES]
    return jnp.concatenate([xg * pmask[p:p + 1, :] for p in range(pmask.shape[0])], axis=0)


def _row_stack(x, masks):
    return jnp.concatenate([x * masks[h:h + 1, :] for h in range(masks.shape[0])], axis=0)


def _params(n_axes, vmem_bytes):
    return pltpu.CompilerParams(dimension_semantics=("arbitrary",) * n_axes,
                                vmem_limit_bytes=int(vmem_bytes))


def _ln_rows(x, g, b):
    mu = jnp.mean(x, axis=-1, keepdims=True)
    xc = x - mu
    var = jnp.mean(xc * xc, axis=-1, keepdims=True)
    return xc * lax.rsqrt(var + LN_EPS) * g + b


def _ln_kernel(x_ref, g_ref, b_ref, o_ref):
    o_ref[...] = _ln_rows(x_ref[...], g_ref[...], b_ref[...])


def _layernorm(x2, g, b, tm=512):
    m, d = x2.shape
    return pl.pallas_call(
        _ln_kernel,
        grid=(m // tm,),
        in_specs=[pl.BlockSpec((tm, d), lambda i: (i, 0)),
                  pl.BlockSpec((1, d), lambda i: (0, 0)),
                  pl.BlockSpec((1, d), lambda i: (0, 0))],
        out_specs=pl.BlockSpec((tm, d), lambda i: (i, 0)),
        out_shape=jax.ShapeDtypeStruct((m, d), F32),
        compiler_params=_params(1, 4 * tm * d * 4 + (4 << 20)),
        name="embed_layernorm",
    )(x2, g.reshape(1, d), b.reshape(1, d))


def _inproj_kernel(x_ref, wt_ref, o_ref, aux_ref, xb_ref, wtb_ref, *, tn):
    i, j = pl.program_id(0), pl.program_id(1)
    k_tiles = x_ref.shape[1] // LANES
    cols = pl.ds(pl.multiple_of(j * tn, tn), tn)

    @pl.when(j == 0)
    def _():
        xb_ref[...] = x_ref[...].astype(BF16)

    @pl.when(i == 0)
    def _():
        wtb_ref[cols, :] = jnp.concatenate(
            [wt_ref[pl.ds(kt, tn, stride=k_tiles), :] for kt in range(k_tiles)], axis=1).astype(BF16)

    acc = _dot_nt(xb_ref[...], wtb_ref[cols, :])
    o_ref[...] = acc.astype(BF16)

    @pl.when(j == pl.num_programs(1) - 1)
    def _():
        aux_ref[...] = acc[:, tn - aux_ref.shape[1]:]


def _inproj(h2, wt_all, layer, tm=1024, tn=1024):
    m, d = h2.shape
    rows = wt_all.shape[1]
    n = rows * LANES // d
    n_col_tiles = n // tn
    tile_rows = tn * d // LANES
    assert N_AUX <= tn
    vmem = (2 * tm * d * 4 + 2 * tile_rows * LANES * 4 + 2 * tm * tn * 2 + 2 * tm * N_AUX * 4 + tm * tn * 4
            + tm * d * 2 + n * d * 2 + (6 << 20))
    return pl.pallas_call(
        functools.partial(_inproj_kernel, tn=tn),
        grid=(m // tm, n_col_tiles),
        in_specs=[pl.BlockSpec((tm, d), lambda i, j: (i, 0)),
                  pl.BlockSpec((None, tile_rows, LANES),
                               lambda i, j: (layer, jnp.where(i == 0, j, n_col_tiles - 1), 0))],
        out_specs=[pl.BlockSpec((tm, tn), lambda i, j: (i, j)),
                   pl.BlockSpec((tm, N_AUX), lambda i, j: (i, 0))],
        out_shape=[jax.ShapeDtypeStruct((m, n), BF16), jax.ShapeDtypeStruct((m, N_AUX), F32)],
        scratch_shapes=[pltpu.VMEM((tm, d), BF16), pltpu.VMEM((n, d), BF16)],
        compiler_params=_params(2, vmem),
        name="in_projection",
    )(h2, wt_all)


def _outproj_kernel(h_ref, y0_ref, y1_ref, y2_ref, y3_ref, w_ref, g_ref, b_ref, o_ref, *, alpha):
    rows = h_ref.shape[0]
    for r in range(0, rows, rows // 2):
        sl = slice(r, r + rows // 2)
        ycat = jnp.concatenate([y0_ref[sl, :], y1_ref[sl, :], y2_ref[sl, :], y3_ref[sl, :]], axis=1)
        y = _dot(ycat.astype(BF16), w_ref[...])
        o_ref[sl, :] = _ln_rows(alpha * h_ref[sl, :] + y, g_ref[...], b_ref[...])


def _outproj(h2, ys, w_bf16, g, b, alpha, tm=512):
    m, d = h2.shape
    dm = w_bf16.shape[0]
    yspec = pl.BlockSpec((tm, D_BRANCH), lambda i: (i, 0))
    vmem = 2 * (2 * tm * d * 4 + tm * dm * 4 + dm * d * 2) + tm * d * 8 + (4 << 20)
    return pl.pallas_call(
        functools.partial(_outproj_kernel, alpha=alpha),
        grid=(m // tm,),
        in_specs=[pl.BlockSpec((tm, d), lambda i: (i, 0)), yspec, yspec, yspec, yspec,
                  pl.BlockSpec((dm, d), lambda i: (0, 0)),
                  pl.BlockSpec((1, d), lambda i: (0, 0)),
                  pl.BlockSpec((1, d), lambda i: (0, 0))],
        out_specs=pl.BlockSpec((tm, d), lambda i: (i, 0)),
        out_shape=jax.ShapeDtypeStruct((m, d), F32),
        compiler_params=_params(1, vmem),
        name="out_projection_layernorm",
    )(h2, *ys, w_bf16, g.reshape(1, d), b.reshape(1, d))


def _head_lane_masks():
    m = np.zeros((N_HEADS, D_BRANCH), np.float32)
    for h in range(N_HEADS):
        m[h, h * HEAD_DIM:(h + 1) * HEAD_DIM] = 1.0
    return m


def _head_block_diag(value=1.0):
    hm = _head_lane_masks()
    return (hm.T @ hm) * value


def _mixer_call(kernel, col_blocks, consts, scratch, n_batch, seq, name, rows=MIX_ROWS, vmem=40 << 20):
    steps = seq // rows
    in_specs = [pl.BlockSpec((rows, w), functools.partial(lambda b, t, c: (b * steps + t, c), c=c))
                for (_, w, c) in col_blocks]
    in_specs += [pl.BlockSpec(a.shape, functools.partial(lambda b, t, nd: (0,) * nd, nd=a.ndim))
                 for a in consts]
    return pl.pallas_call(
        kernel,
        grid=(n_batch, steps),
        in_specs=in_specs,
        out_specs=pl.BlockSpec((rows, D_BRANCH), lambda b, t: (b * steps + t, 0)),
        out_shape=jax.ShapeDtypeStruct((n_batch * seq, D_BRANCH), F32),
        scratch_shapes=scratch,
        compiler_params=_params(2, vmem),
        name=name,
    )(*[a for (a, _, _) in col_blocks], *consts)


def _ssd_kernel(xbc_ref, z_ref, dt_ref, convw_ref, convb_ref, dtb_ref, aneg_ref, dskip_ref, nw_ref,
                tri_ref, negmask_ref, exph_ref, pmask_ref, smask_ref, shift_ref,
                o_ref, xp_ref, state_ref, u_ref, sb_ref, ea_ref, cm_ref):
    C = SSD_CHUNK
    rows = xbc_ref.shape[0]
    n_chunks = rows // C
    halves = D_BRANCH // LANES
    pad = BF16_ROWS
    sls = [slice(c * C, (c + 1) * C) for c in range(n_chunks)]

    @pl.when(pl.program_id(1) == 0)
    def _():
        xp_ref[0:pad, :] = jnp.zeros((pad, SSM_XBC), BF16)
        state_ref[...] = jnp.zeros_like(state_ref)

    xp_ref[pad:pad + rows, :] = xbc_ref[...]

    convw = convw_ref[...]
    pmask = pmask_ref[...]
    exph = exph_ref[...]
    xs, bmb, lcat, dth, ach = [], [], [], [], []
    for c in range(n_chunks):
        r0 = c * C
        ext = xp_ref[r0:r0 + C + pad, :]
        shifted = _dot(shift_ref[...], ext)
        acc = convb_ref[...] + convw[SSM_CONV - 1:SSM_CONV, :] * ext[pad:, :].astype(F32)
        for j in range(SSM_CONV - 1):
            acc = acc + convw[j:j + 1, :] * shifted[j * C:(j + 1) * C, :]
        xbc = _silu(acc)
        xs.append(xbc[:, :D_BRANCH])
        bmb.append(xbc[:, D_BRANCH:D_BRANCH + LANES].astype(BF16))
        cm_ref[c] = xbc[:, D_BRANCH + LANES:].astype(BF16)
    for c in range(n_chunks):
        dt = _softplus(dt_ref[sls[c], :] + dtb_ref[...])
        a_cum = _sel_dot(tri_ref[...], dt * aneg_ref[...])
        a_cum_t = a_cum.T
        col = jnp.concatenate(
            [jnp.broadcast_to(a_cum[:, h:h + 1], (C, C)) for h in range(N_HEADS)], axis=1)
        row = jnp.concatenate([a_cum_t[h:h + 1, :] for h in range(N_HEADS)], axis=1)
        lcat.append(jnp.exp(col - row + negmask_ref[...]))
        dth.append(_dot_sel(dt, exph))
        ach.append(_dot_sel(a_cum, exph))
    heads_per_group = N_HEADS // SSM_GROUPS
    for c in range(n_chunks):
        sg = _dot_nt(cm_ref[c], _pair_stack(bmb[c], 0, pmask))
        s4 = jnp.concatenate([sg[:, (h // heads_per_group) * C:(h // heads_per_group + 1) * C]
                              for h in range(N_HEADS)], axis=1)
        p = (s4 * lcat[c]).astype(BF16)
        xdt = xs[c] * dth[c]
        xdtb = xdt.astype(BF16)
        y = jnp.concatenate(
            [_dot(p[:, g * 2 * C:(g + 1) * 2 * C], _pair_stack(xdtb, g, pmask)) for g in range(halves)], axis=1)
        o_ref[sls[c], :] = y + xs[c] * dskip_ref[...]
        alast = ach[c][C - 1:C, :]
        u_ref[c] = smask_ref[...] * _dot_tn(bmb[c], (xdt * jnp.exp(alast - ach[c])).astype(BF16))
        ea_ref[c] = jnp.exp(ach[c])

    xp_ref[0:pad, :] = xbc_ref[rows - pad:rows, :]

    st = state_ref[...]
    for c in range(n_chunks):
        sb_ref[c] = st.astype(BF16)
        st = st * ea_ref[c, C - 1:C, :] + u_ref[c]
    state_ref[...] = st

    for c in range(n_chunks):
        y = o_ref[sls[c], :] + _dot(cm_ref[c], sb_ref[c]) * ea_ref[c]
        g = y * _silu(z_ref[sls[c], :].astype(F32))
        outs = []
        for grp in range(SSM_GROUPS):
            gg = g[:, grp * LANES:(grp + 1) * LANES]
            ms = jnp.mean(gg * gg, axis=-1, keepdims=True)
            outs.append(gg * lax.rsqrt(ms + RMS_EPS))
        o_ref[sls[c], :] = jnp.concatenate(outs, axis=1) * nw_ref[...]


def _ssd_consts():
    C = SSD_CHUNK
    idx = np.arange(C)
    tri = (idx[:, None] >= idx[None, :]).astype(np.float32)
    negmask = np.tile(np.where(tri > 0, 0.0, NEG_BIG).astype(np.float32), (1, N_HEADS))
    exph = np.zeros((LANES, D_BRANCH), np.float32)
    exph[:N_HEADS] = _head_lane_masks()
    heads_per_group = N_HEADS // SSM_GROUPS
    smask = np.zeros((LANES, D_BRANCH), np.float32)
    for h in range(N_HEADS):
        g = h // heads_per_group
        smask[g * SSM_STATE:(g + 1) * SSM_STATE, h * HEAD_DIM:(h + 1) * HEAD_DIM] = 1.0
    shift = np.zeros(((SSM_CONV - 1) * C, C + BF16_ROWS), np.float32)
    for j in range(SSM_CONV - 1):
        shift[j * C + idx, BF16_ROWS - (SSM_CONV - 1) + j + idx] = 1.0
    return (jnp.asarray(tri, BF16), jnp.asarray(negmask), jnp.asarray(exph, BF16),
            _pair_masks(), jnp.asarray(smask), jnp.asarray(shift, BF16))


def _ssd(proj, aux, conv_w, conv_b, dt_bias, a_log, d_skip, norm_w, n_batch, seq):
    lane_pad = LANES - N_HEADS
    dtb = jnp.pad(dt_bias.astype(F32), (0, lane_pad)).reshape(1, LANES)
    aneg = jnp.pad(-jnp.exp(a_log.astype(F32)), (0, lane_pad)).reshape(1, LANES)
    dskip = jnp.repeat(d_skip.astype(F32), HEAD_DIM).reshape(1, D_BRANCH)
    consts = [conv_w.astype(F32), conv_b.reshape(1, SSM_XBC).astype(F32), dtb, aneg, dskip,
              norm_w.reshape(1, D_BRANCH).astype(F32), *_ssd_consts()]
    n_chunks = MIX_ROWS // SSD_CHUNK
    state_shape = (SSM_GROUPS * SSM_STATE, D_BRANCH)
    scratch = [pltpu.VMEM((MIX_ROWS + BF16_ROWS, SSM_XBC), BF16),
               pltpu.VMEM(state_shape, F32),
               pltpu.VMEM((n_chunks,) + state_shape, F32),
               pltpu.VMEM((n_chunks,) + state_shape, BF16),
               pltpu.VMEM((n_chunks, SSD_CHUNK, D_BRANCH), F32),
               pltpu.VMEM((n_chunks, SSD_CHUNK, LANES), BF16)]
    cols = [(proj, SSM_XBC, 0), (proj, D_BRANCH, COL_Z), (aux, LANES, AUX_COL_DT)]
    return _mixer_call(_ssd_kernel, cols, consts, scratch, n_batch, seq, "ssd_mixer")


def _seg_mean(x, avg):
    return _dot_sel(x, avg)


def _ret_kernel(q_ref, k_ref, v_ref, g_ref, dcat_ref, qdec_ref, kdec_ref, cg_ref, pmask_ref,
                bd_ref, avg_ref, o_ref, state_ref, u_ref, rb_ref):
    C = RET_CHUNK
    rows = q_ref.shape[0]
    n_chunks = rows // C
    halves = D_BRANCH // LANES
    sls = [slice(c * C, (c + 1) * C) for c in range(n_chunks)]

    @pl.when(pl.program_id(1) == 0)
    def _():
        state_ref[...] = jnp.zeros_like(state_ref)

    pmask = pmask_ref[...]
    for c in range(n_chunks):
        qb, kb, vb = q_ref[sls[c], :], k_ref[sls[c], :], v_ref[sls[c], :]
        outs = []
        for g in range(halves):
            scores = _dot_nt(qb[:, g * LANES:(g + 1) * LANES], _pair_stack(kb, g, pmask))
            p = (scores * dcat_ref[:, g * 2 * C:(g + 1) * 2 * C]).astype(BF16)
            outs.append(_dot(p, _pair_stack(vb, g, pmask)))
        o_ref[sls[c], :] = jnp.concatenate(outs, axis=1)
        u_ref[c] = bd_ref[...] * _dot_tn((kb.astype(F32) * kdec_ref[...]).astype(BF16), vb)

    r = state_ref[...]
    for c in range(n_chunks):
        rb_ref[c] = r.astype(BF16)
        r = r * cg_ref[...] + u_ref[c]
    state_ref[...] = r

    for c in range(n_chunks):
        y = o_ref[sls[c], :] + _dot((q_ref[sls[c], :].astype(F32) * qdec_ref[...]).astype(BF16), rb_ref[c])
        mu = _seg_mean(y, avg_ref[...])
        yc = y - mu
        var = _seg_mean(yc * yc, avg_ref[...])
        o_ref[sls[c], :] = yc * lax.rsqrt(var + LN_EPS) * _silu(g_ref[sls[c], :].astype(F32))


def _ret_consts():
    C = RET_CHUNK
    log_g = jnp.log(1.0 - 2.0 ** (-5.0 - jnp.arange(N_HEADS, dtype=F32)))
    pos = jnp.arange(C, dtype=F32)
    dist = pos[:, None] - pos[None, :]
    intra = jnp.where(dist >= 0, jnp.exp(log_g[:, None, None] * jnp.maximum(dist, 0.0)), 0.0)
    scale = HEAD_DIM ** -0.5
    dcat = jnp.concatenate([intra[h] for h in range(N_HEADS)], axis=1) * scale
    k_decay = jnp.exp(log_g[:, None] * (C - 1.0 - pos)[None, :]) * scale
    q_decay = jnp.exp(log_g[:, None] * (pos + 1.0)[None, :])
    kdec = jnp.repeat(k_decay.T, HEAD_DIM, axis=1)
    qdec = jnp.repeat(q_decay.T, HEAD_DIM, axis=1)
    cg = jnp.repeat(jnp.exp(log_g * C), HEAD_DIM).reshape(1, D_BRANCH)
    return (dcat, qdec, kdec, cg, _pair_masks(),
            jnp.asarray(_head_block_diag()), jnp.asarray(_head_block_diag(1.0 / HEAD_DIM), BF16))


def _retention(proj, n_batch, seq):
    n_chunks = MIX_ROWS // RET_CHUNK
    scratch = [pltpu.VMEM((D_BRANCH, D_BRANCH), F32),
               pltpu.VMEM((n_chunks, D_BRANCH, D_BRANCH), F32),
               pltpu.VMEM((n_chunks, D_BRANCH, D_BRANCH), BF16)]
    cols = [(proj, D_BRANCH, COL_RET + i) for i in range(4)]
    return _mixer_call(_ret_kernel, cols, list(_ret_consts()), scratch, n_batch, seq, "retention_mixer")


def _hgrn_levels():
    return int(math.log2(HGRN_CHUNK))


def _hgrn_kernel(q_ref, i_ref, g_ref, f_ref, lb_ref, nw_ref, mstack_ref, lmask_ref, pmask_ref,
                 bd_ref, avg_ref, o_ref, state_ref, e_ref, u_ref, stb_ref):
    C = HGRN_CHUNK
    nl = _hgrn_levels()
    rows = q_ref.shape[0]
    halves = D_BRANCH // LANES

    @pl.when(pl.program_id(1) == 0)
    def _():
        state_ref[...] = jnp.zeros_like(state_ref)

    pmask = pmask_ref[...]
    lb = lb_ref[...]

    def head_scores(qb, kb):
        return jnp.concatenate(
            [_dot_nt(qb[:, g * LANES:(g + 1) * LANES], _pair_stack(kb, g, pmask)) for g in range(halves)], axis=1)

    n_chunks = rows // C
    sls = [slice(c * C, (c + 1) * C) for c in range(n_chunks)]

    k_rows = slice((nl + 2) * C, (nl + 3) * C)
    for c in range(n_chunks):
        f = lb + (1.0 - lb) * _sigmoid(f_ref[sls[c], :])
        e_ref[c, 0:(nl + 2) * C, :] = jnp.exp(_sel_dot(mstack_ref[...], jnp.log(f)))
        e_ref[c, k_rows, :] = 1.0 - f

    group = 2
    for c0 in range(0, n_chunks, group):
        cs = range(c0, min(c0 + group, n_chunks))
        att = {c: lmask_ref[nl] * head_scores(q_ref[sls[c], :], e_ref[c, k_rows, :].astype(BF16))
               for c in cs}
        for j in range(nl):
            for c in cs:
                ej = e_ref[c, j * C:(j + 1) * C, :]
                att[c] = att[c] + lmask_ref[j] * head_scores((q_ref[sls[c], :].astype(F32) * ej).astype(BF16),
                                                             (e_ref[c, k_rows, :] * ej).astype(BF16))
        for c in cs:
            attb = att[c].astype(BF16)
            vb = i_ref[sls[c], :]
            o_ref[sls[c], :] = jnp.concatenate(
                [_dot(attb[:, g * 2 * C:(g + 1) * 2 * C], _pair_stack(vb, g, pmask)) for g in range(halves)], axis=1)

    for c in range(n_chunks):
        er = e_ref[c, (nl + 1) * C:(nl + 2) * C, :]
        u_ref[c] = bd_ref[...] * _dot_tn(i_ref[sls[c], :], (e_ref[c, k_rows, :] * er).astype(BF16))

    st = state_ref[...]
    for c in range(n_chunks):
        stb_ref[c] = st.astype(BF16)
        st = st * e_ref[c, (nl + 1) * C - 1:(nl + 1) * C, :] + u_ref[c]
    state_ref[...] = st

    for c in range(n_chunks):
        eb = e_ref[c, nl * C:(nl + 1) * C, :]
        o = o_ref[sls[c], :] + _dot_nt((q_ref[sls[c], :].astype(F32) * eb).astype(BF16), stb_ref[c])
        ms = _seg_mean(o * o, avg_ref[...])
        o_ref[sls[c], :] = o * lax.rsqrt(ms + RMS_EPS) * nw_ref[...] * _silu(g_ref[sls[c], :].astype(F32))


def _hgrn_consts():
    C = HGRN_CHUNK
    nl = _hgrn_levels()
    t = np.arange(C)[:, None]
    u = np.arange(C)[None, :]
    blocks, masks = [], []
    for j in range(nl):
        s = 1 << j
        bnd = (t // (2 * s)) * (2 * s) + s - 1
        upper = (t % (2 * s)) >= s
        m = np.where(upper, (u > bnd) & (u <= t), (u > t) & (u <= bnd))
        blocks.append(m.astype(np.float32))
        pair = (t // (2 * s)) == (u // (2 * s))
        masks.append((upper & pair & ((u % (2 * s)) < s)).astype(np.float32))
    blocks.append((u <= t).astype(np.float32))
    blocks.append((u > t).astype(np.float32))
    masks.append((u == t).astype(np.float32))
    mstack = np.concatenate(blocks, axis=0)
    lmask = np.stack([np.tile(m, (1, N_HEADS)) for m in masks])
    return (jnp.asarray(mstack, BF16), jnp.asarray(lmask), _pair_masks(),
            jnp.asarray(_head_block_diag()), jnp.asarray(_head_block_diag(1.0 / HEAD_DIM), BF16))


def _hgrn2(proj, aux, lb, norm_w, n_batch, seq):
    consts = [lb.reshape(1, D_BRANCH).astype(F32), norm_w.reshape(1, D_BRANCH).astype(F32), *_hgrn_consts()]
    scratch = [pltpu.VMEM((D_BRANCH, D_BRANCH), F32),
               pltpu.VMEM((MIX_ROWS // HGRN_CHUNK, (_hgrn_levels() + 3) * HGRN_CHUNK, D_BRANCH), F32),
               pltpu.VMEM((MIX_ROWS // HGRN_CHUNK, D_BRANCH, D_BRANCH), F32),
               pltpu.VMEM((MIX_ROWS // HGRN_CHUNK, D_BRANCH, D_BRANCH), BF16)]
    cols = [(proj, D_BRANCH, COL_HGRN + i) for i in range(3)] + [(aux, D_BRANCH, AUX_COL_F)]
    return _mixer_call(_hgrn_kernel, cols, consts, scratch, n_batch, seq, "hgrn2_mixer")


MOBA_VROWS = 80


def _moba_kernel(q_ref, k_ref, v_ref, g_ref, ksel_ref, kpos_ref, kblk_ref, qaug_ref, o_ref,
                 kaug_ref, vt_ref, kmean_ref, sel_ref, qt_ref, acc_ref, m_ref, mcur_ref, alpha_ref,
                 s_ref, p_ref):
    S = MOBA_BLOCK
    i = pl.program_id(1)
    nbp = kmean_ref.shape[0]

    @pl.when(i == 0)
    def _():
        kmean_ref[...] = jnp.zeros_like(kmean_ref)

    kb = k_ref[...]
    kmean_ref[pl.ds(i, 1), :] = jnp.mean(kb.astype(F32), axis=0, keepdims=True)
    v_t = v_ref[...].astype(F32).T
    q_t = (q_ref[...].astype(F32) * (HEAD_DIM ** -0.5)).T
    ones_pad = (lax.broadcasted_iota(jnp.int32, (MOBA_VROWS - HEAD_DIM, S), 0) == 0).astype(F32)
    blk = lax.broadcasted_iota(jnp.int32, (nbp, S), 0)
    blk_f = blk.astype(F32)
    past = blk < i
    kmean = kmean_ref[...]
    own_rows = (lax.broadcasted_iota(jnp.int32, (SUBLANES, S), 0) == 0).astype(F32)
    kextra = kpos_ref[...] + i.astype(F32) * kblk_ref[...]
    for h in range(N_HEADS):
        hs = slice(h * HEAD_DIM, (h + 1) * HEAD_DIM)
        kaug_ref[i, h] = (_dot(kb, ksel_ref[h]) + kextra).astype(BF16)
        vt_ref[i, h] = jnp.concatenate([v_t[hs, :], ones_pad], axis=0).astype(BF16)
        qh_t = q_t[hs, :]
        qt_ref[h] = jnp.concatenate([qh_t * LOG2E, qaug_ref[h]], axis=0).astype(BF16)
        km_hi, km_mid, _ = _split3(kmean[:, hs])
        q_hi, q_mid, _ = _split3(qh_t)
        gate = _dot(km_hi, q_hi) + _dot(km_hi, q_mid) + _dot(km_mid, q_hi)
        gate = jnp.where(past, gate, -jnp.inf)
        sel = jnp.zeros((nbp, S), F32)
        for _ in range(MOBA_TOPK):
            best = jnp.max(gate, axis=0, keepdims=True)
            first = jnp.min(jnp.where(gate == best, blk_f, float(nbp)), axis=0, keepdims=True)
            hit = blk_f == first
            sel = jnp.where(hit, 1.0, sel)
            gate = jnp.where(hit, -jnp.inf, gate)
        sel_ref[h, 0:nbp, :] = jnp.where(past, sel, 0.0)
        sel_ref[h, nbp:nbp + SUBLANES, :] = own_rows

    def block_of(t):
        return jnp.where(t == 0, i, jnp.minimum(t - 1, i))

    def sel_row_of(t):
        return jnp.where(t == 0, nbp, jnp.where(t <= i, t - 1, nbp + 1))

    def track_max(h, slot, s_t, chosen):
        m_old = m_ref[h]
        m_new = jnp.where(chosen, jnp.maximum(m_old, jnp.max(s_t, axis=0, keepdims=True)), m_old)
        m_ref[h] = m_new
        mcur_ref[slot, h] = m_new
        alpha_ref[slot, h] = jnp.exp2(m_old - m_new)

    def score_phase(t, slot):
        n, row = block_of(t), sel_row_of(t)
        for h in range(N_HEADS):
            s_t = _dot(kaug_ref[n, h], qt_ref[h])
            s_ref[slot, h] = s_t
            track_max(h, slot, s_t, sel_ref[h, pl.ds(row, 1), :] > 0.0)

    def value_phase(t, slot):
        n, row = block_of(t), sel_row_of(t)
        for h in range(N_HEADS):
            p_ref[h] = jnp.exp2(s_ref[slot, h] - mcur_ref[slot, h]).astype(BF16)
        for h in range(N_HEADS):
            chosen = sel_ref[h, pl.ds(row, 1), :] > 0.0
            upd = _dot(vt_ref[n, h], p_ref[h])
            acc_ref[h] = alpha_ref[slot, h] * acc_ref[h] + jnp.where(chosen, upd, 0.0)

    key_pos = lax.broadcasted_iota(jnp.int32, (S, S), 0)
    qry_pos = lax.broadcasted_iota(jnp.int32, (S, S), 1)
    for h in range(N_HEADS):
        acc_ref[h] = jnp.zeros((MOBA_VROWS, S), F32)
        m_ref[h] = jnp.full((1, S), NEG_BIG, F32)
        s_t = jnp.where(key_pos <= qry_pos, _dot(kaug_ref[i, h], qt_ref[h]), NEG_BIG)
        s_ref[0, h] = s_t
        track_max(h, 0, s_t, True)

    def position_pair(j, carry):
        t0 = 2 * j
        score_phase(t0 + 1, 1)
        value_phase(t0, 0)
        score_phase(t0 + 2, 0)
        value_phase(t0 + 1, 1)
        return carry

    lax.fori_loop(0, (i + 2) // 2, position_pair, 0)

    outs = []
    for h in range(N_HEADS):
        acc = acc_ref[h]
        outs.append(acc[:HEAD_DIM, :] / acc[HEAD_DIM:HEAD_DIM + 1, :])
    o_ref[...] = jnp.concatenate(outs, axis=0).T * _silu(g_ref[...].astype(F32))


def _moba(proj, n_batch, seq):
    S = MOBA_BLOCK
    assert seq % S == 0
    nb = seq // S
    nbp = -(-nb // SUBLANES) * SUBLANES
    ksel = np.zeros((N_HEADS, D_BRANCH, LANES), np.float32)
    for h in range(N_HEADS):
        ksel[h, h * HEAD_DIM + np.arange(HEAD_DIM), np.arange(HEAD_DIM)] = 1.0
    n_split = 3
    kpos = np.zeros((S, LANES), np.float32)
    kblk = np.zeros((S, LANES), np.float32)
    kpos[:, HEAD_DIM:HEAD_DIM + n_split] = np.arange(S, dtype=np.float32)[:, None]
    kblk[:, HEAD_DIM + n_split:HEAD_DIM + 2 * n_split] = 1.0
    slopes = 2.0 ** (-8.0 * np.arange(1, N_HEADS + 1, dtype=np.float32) / N_HEADS)
    qaug = np.zeros((N_HEADS, LANES - HEAD_DIM, S), np.float32)
    for h in range(N_HEADS):
        for base, coef in ((0, slopes[h] * LOG2E), (n_split, slopes[h] * LOG2E * S)):
            rem = np.float32(coef)
            for r in range(n_split):
                piece = np.float32(np.asarray(rem).astype(BF16))
                qaug[h, base + r, :] = piece
                rem = np.float32(rem - piece)
    consts = [jnp.asarray(ksel, BF16), jnp.asarray(kpos), jnp.asarray(kblk), jnp.asarray(qaug)]
    scratch = [pltpu.VMEM((nb, N_HEADS, S, LANES), BF16),
               pltpu.VMEM((nb, N_HEADS, MOBA_VROWS, S), BF16),
               pltpu.VMEM((nbp, D_BRANCH), F32),
               pltpu.VMEM((N_HEADS, nbp + SUBLANES, S), F32),
               pltpu.VMEM((N_HEADS, LANES, S), BF16),
               pltpu.VMEM((N_HEADS, MOBA_VROWS, S), F32),
               pltpu.VMEM((N_HEADS, 1, S), F32),
               pltpu.VMEM((2, N_HEADS, 1, S), F32),
               pltpu.VMEM((2, N_HEADS, 1, S), F32),
               pltpu.VMEM((2, N_HEADS, S, S), F32),
               pltpu.VMEM((N_HEADS, S, S), BF16)]
    vmem = nb * N_HEADS * S * (LANES + MOBA_VROWS) * 2 + (24 << 20)
    cols = [(proj, D_BRANCH, COL_MOBA + i) for i in range(4)]
    return _mixer_call(_moba_kernel, cols, consts, scratch, n_batch, seq, "moba_mixer", rows=S, vmem=vmem)


def _pack_w_in(w_in):
    depth, d, n_in = w_in.shape
    wt = jnp.transpose(w_in, (2, 0, 1)).reshape(n_in, depth, d // LANES, LANES)
    o = np.cumsum((0, D_BRANCH, SSM_XBC, N_HEADS) + (D_BRANCH,) * 12)
    z, xbc, dt = wt[o[0]:o[1]], wt[o[1]:o[2]], wt[o[2]:o[3]]
    hq, hf, hig, ret_moba = wt[o[3]:o[4]], wt[o[4]:o[5]], wt[o[5]:o[7]], wt[o[7]:]
    pad = jnp.zeros((N_PROJ - n_in,) + wt.shape[1:], wt.dtype)
    packed = jnp.concatenate([xbc, z, hq, hig, ret_moba, hf, dt, pad], axis=0)
    return jnp.transpose(packed, (1, 0, 2, 3)).reshape(depth, N_PROJ * (d // LANES), LANES)


def kernel(x, emb_ln_g, emb_ln_b, w_in, ssd_conv_w, ssd_conv_b, ssd_dt_bias, ssd_a_log, ssd_d, ssd_norm_w,
           hgrn_lb_logits, hgrn_norm_w, w_out, ln_g, ln_b):
    n_batch, seq, d_model = x.shape
    depth = w_in.shape[0]
    alpha = (2.0 * depth) ** 0.25
    assert seq % MIX_ROWS == 0 and seq % MOBA_BLOCK == 0

    lbs = jnp.cumsum(jax.nn.softmax(hgrn_lb_logits.astype(F32), axis=0), axis=0)
    lbs = lbs - lbs[0]

    wt_all = _pack_w_in(w_in.astype(F32))
    h = _layernorm(x.reshape(n_batch * seq, d_model), emb_ln_g, emb_ln_b)
    for l in range(depth):
        proj, aux = _inproj(h, wt_all, l)
        ys = [_ssd(proj, aux, ssd_conv_w[l], ssd_conv_b[l], ssd_dt_bias[l], ssd_a_log[l], ssd_d[l],
                   ssd_norm_w[l], n_batch, seq),
              _hgrn2(proj, aux, lbs[l], hgrn_norm_w[l], n_batch, seq),
              _retention(proj, n_batch, seq),
              _moba(proj, n_batch, seq)]
        h = _outproj(h, ys, w_out[l].astype(BF16), ln_g[l], ln_b[l], alpha)
    return h.reshape(n_batch, seq, d_model)
```

```python
import functools
import math

import numpy as np
import jax
import jax.numpy as jnp
from jax import lax
from jax.experimental import pallas as pl
from jax.experimental.pallas import tpu as pltpu

F32 = jnp.float32
BF16 = jnp.bfloat16

D_BRANCH = 256
HEAD_DIM = 64
N_HEADS = 4
SSM_STATE = 64
SSM_GROUPS = 2
SSM_CONV = 4
SSM_XBC = D_BRANCH + 2 * SSM_GROUPS * SSM_STATE
MOBA_BLOCK = 256
MOBA_TOPK = 3
LN_EPS = 1e-5
RMS_EPS = 1e-6
NEG_BIG = -1e30
LOG2E = 1.4426950408889634

LANES = 128
SUBLANES = 8
BF16_ROWS = 16
VMEM_BYTES_V7X = 64 * 1024 * 1024

SSD_CHUNK = 128
RET_CHUNK = 256
HGRN_CHUNK = 128
MIX_ROWS = 1024

N_PROJ = 4096
N_AUX = 512
COL_Z = SSM_XBC // D_BRANCH
COL_HGRN = COL_Z + 1
COL_RET = COL_HGRN + 3
COL_MOBA = COL_RET + 4
AUX_COL_F = 0
AUX_COL_DT = D_BRANCH // LANES


def _dot(a, b):
    return jnp.dot(a, b, preferred_element_type=F32)


def _dot_nt(a, b):
    return lax.dot_general(a, b, (((1,), (1,)), ((), ())), preferred_element_type=F32)


def _dot_tn(a, b):
    return lax.dot_general(a, b, (((0,), (0,)), ((), ())), preferred_element_type=F32)


def _split3(x):
    hi = x.astype(BF16)
    r1 = x - hi.astype(F32)
    mid = r1.astype(BF16)
    lo = (r1 - mid.astype(F32)).astype(BF16)
    return hi, mid, lo


def _split2(x):
    hi = x.astype(BF16)
    return hi, (x - hi.astype(F32)).astype(BF16)


def _sel_dot(sel, x):
    hi, lo = _split2(x)
    return _dot(sel, hi) + _dot(sel, lo)


def _dot_sel(x, sel):
    hi, lo = _split2(x)
    return _dot(hi, sel) + _dot(lo, sel)


def _sigmoid(x):
    return 1.0 / (1.0 + jnp.exp(-x))


def _silu(x):
    return x * _sigmoid(x)


def _softplus(x):
    return jnp.maximum(x, 0.0) + jnp.log(1.0 + jnp.exp(-jnp.abs(x)))


def _pair_masks():
    pm = np.zeros((LANES // HEAD_DIM, LANES), np.float32)
    for p in range(LANES // HEAD_DIM):
        pm[p, p * HEAD_DIM:(p + 1) * HEAD_DIM] = 1.0
    return jnp.asarray(pm, BF16)


def _pair_stack(xb, g, pmask):
    xg = xb[:, g * LANES:(g + 1) * LANES]
    return jnp.concatenate([xg * pmask[p:p + 1, :] for p in range(pmask.shape[0])], axis=0)


def _row_stack(x, masks):
    return jnp.concatenate([x * masks[h:h + 1, :] for h in range(masks.shape[0])], axis=0)


def _params(n_axes, vmem_bytes):
    return pltpu.CompilerParams(dimension_semantics=("arbitrary",) * n_axes,
                                vmem_limit_bytes=int(vmem_bytes))


def _ln_rows(x, g, b):
    mu = jnp.mean(x, axis=-1, keepdims=True)
    xc = x - mu
    var = jnp.mean(xc * xc, axis=-1, keepdims=True)
    return xc * lax.rsqrt(var + LN_EPS) * g + b


def _inproj_kernel(*refs, tn, norm_first):
    if norm_first:
        x_ref, wt_ref, g_ref, b_ref, o_ref, aux_ref, h_ref, xb_ref, wtb_ref = refs
    else:
        x_ref, wt_ref, o_ref, aux_ref, xb_ref, wtb_ref = refs
    i, j = pl.program_id(0), pl.program_id(1)
    k_tiles = x_ref.shape[1] // LANES
    cols = pl.ds(pl.multiple_of(j * tn, tn), tn)

    @pl.when(j == 0)
    def _():
        if norm_first:
            h = _ln_rows(x_ref[...], g_ref[...], b_ref[...])
            h_ref[...] = h
            xb_ref[...] = h.astype(BF16)
        else:
            xb_ref[...] = x_ref[...].astype(BF16)

    @pl.when(i == 0)
    def _():
        wtb_ref[cols, :] = jnp.concatenate(
            [wt_ref[pl.ds(kt, tn, stride=k_tiles), :] for kt in range(k_tiles)], axis=1).astype(BF16)

    acc = _dot_nt(xb_ref[...], wtb_ref[cols, :])
    o_ref[...] = acc.astype(BF16)

    @pl.when(j == pl.num_programs(1) - 1)
    def _():
        aux_ref[...] = acc[:, tn - aux_ref.shape[1]:]


def _inproj(x2, wt_all, layer, norm=None, tm=1024, tn=1024):
    m, d = x2.shape
    rows = wt_all.shape[1]
    n = rows * LANES // d
    n_col_tiles = n // tn
    tile_rows = tn * d // LANES
    assert N_AUX <= tn
    vmem = (2 * tm * d * 4 + 2 * tile_rows * LANES * 4 + 2 * tm * tn * 2 + 2 * tm * N_AUX * 4 + tm * tn * 4
            + tm * d * 2 + n * d * 2 + (6 << 20))
    row_spec = pl.BlockSpec((tm, d), lambda i, j: (i, 0))
    vec_spec = pl.BlockSpec((1, d), lambda i, j: (0, 0))
    in_specs = [row_spec,
                pl.BlockSpec((None, tile_rows, LANES),
                             lambda i, j: (layer, jnp.where(i == 0, j, n_col_tiles - 1), 0))]
    out_specs = [pl.BlockSpec((tm, tn), lambda i, j: (i, j)),
                 pl.BlockSpec((tm, N_AUX), lambda i, j: (i, 0))]
    out_shape = [jax.ShapeDtypeStruct((m, n), BF16), jax.ShapeDtypeStruct((m, N_AUX), F32)]
    operands = [x2, wt_all]
    if norm is not None:
        in_specs += [vec_spec, vec_spec]
        out_specs.append(row_spec)
        out_shape.append(jax.ShapeDtypeStruct((m, d), F32))
        operands += [norm[0].reshape(1, d), norm[1].reshape(1, d)]
        vmem += 2 * tm * d * 4
    return pl.pallas_call(
        functools.partial(_inproj_kernel, tn=tn, norm_first=norm is not None),
        grid=(m // tm, n_col_tiles),
        in_specs=in_specs,
        out_specs=out_specs,
        out_shape=out_shape,
        scratch_shapes=[pltpu.VMEM((tm, d), BF16), pltpu.VMEM((n, d), BF16)],
        compiler_params=_params(2, vmem),
        name="in_projection",
    )(*operands)


def _outproj_kernel(h_ref, y0_ref, y1_ref, y2_ref, y3_ref, w_ref, g_ref, b_ref, o_ref, *, alpha):
    rows = h_ref.shape[0]
    for r in range(0, rows, rows // 2):
        sl = slice(r, r + rows // 2)
        ycat = jnp.concatenate([y0_ref[sl, :], y1_ref[sl, :], y2_ref[sl, :], y3_ref[sl, :]], axis=1)
        y = _dot(ycat, w_ref[...])
        o_ref[sl, :] = _ln_rows(alpha * h_ref[sl, :] + y, g_ref[...], b_ref[...])


def _outproj(h2, ys, w_bf16, g, b, alpha, tm=512):
    m, d = h2.shape
    dm = w_bf16.shape[0]
    yspec = pl.BlockSpec((tm, D_BRANCH), lambda i: (i, 0))
    vmem = 2 * (2 * tm * d * 4 + tm * dm * 2 + dm * d * 2) + tm * d * 8 + (4 << 20)
    return pl.pallas_call(
        functools.partial(_outproj_kernel, alpha=alpha),
        grid=(m // tm,),
        in_specs=[pl.BlockSpec((tm, d), lambda i: (i, 0)), yspec, yspec, yspec, yspec,
                  pl.BlockSpec((dm, d), lambda i: (0, 0)),
                  pl.BlockSpec((1, d), lambda i: (0, 0)),
                  pl.BlockSpec((1, d), lambda i: (0, 0))],
        out_specs=pl.BlockSpec((tm, d), lambda i: (i, 0)),
        out_shape=jax.ShapeDtypeStruct((m, d), F32),
        compiler_params=_params(1, vmem),
        name="out_projection_layernorm",
    )(h2, *ys, w_bf16, g.reshape(1, d), b.reshape(1, d))


def _head_lane_masks():
    m = np.zeros((N_HEADS, D_BRANCH), np.float32)
    for h in range(N_HEADS):
        m[h, h * HEAD_DIM:(h + 1) * HEAD_DIM] = 1.0
    return m


def _head_block_diag(value=1.0):
    hm = _head_lane_masks()
    return (hm.T @ hm) * value


def _mixer_call(kernel, col_blocks, consts, scratch, n_batch, seq, name, rows=MIX_ROWS, vmem=40 << 20):
    steps = seq // rows
    in_specs = [pl.BlockSpec((rows, w), functools.partial(lambda b, t, c: (b * steps + t, c), c=c))
                for (_, w, c) in col_blocks]
    in_specs += [pl.BlockSpec(a.shape, functools.partial(lambda b, t, nd: (0,) * nd, nd=a.ndim))
                 for a in consts]
    return pl.pallas_call(
        kernel,
        grid=(n_batch, steps),
        in_specs=in_specs,
        out_specs=pl.BlockSpec((rows, D_BRANCH), lambda b, t: (b * steps + t, 0)),
        out_shape=jax.ShapeDtypeStruct((n_batch * seq, D_BRANCH), BF16),
        scratch_shapes=scratch,
        compiler_params=_params(2, vmem),
        name=name,
    )(*[a for (a, _, _) in col_blocks], *consts)


def _ssd_kernel(xbc_ref, z_ref, dt_ref, convw_ref, convb_ref, dtb_ref, aneg_ref, dskip_ref, nw_ref,
                tri_ref, negmask_ref, exph_ref, pmask_ref, smask_ref, shift_ref,
                o_ref, xp_ref, state_ref, u_ref, sb_ref, ea_ref, cm_ref, part_ref):
    C = SSD_CHUNK
    rows = xbc_ref.shape[0]
    n_chunks = rows // C
    halves = D_BRANCH // LANES
    pad = BF16_ROWS
    sls = [slice(c * C, (c + 1) * C) for c in range(n_chunks)]

    @pl.when(pl.program_id(1) == 0)
    def _():
        xp_ref[0:pad, :] = jnp.zeros((pad, SSM_XBC), BF16)
        state_ref[...] = jnp.zeros_like(state_ref)

    xp_ref[pad:pad + rows, :] = xbc_ref[...]

    convw = convw_ref[...]
    pmask = pmask_ref[...]
    exph = exph_ref[...]
    xs, bmb, lcat, dth, ach = [], [], [], [], []
    for c in range(n_chunks):
        r0 = c * C
        ext = xp_ref[r0:r0 + C + pad, :]
        shifted = _dot(shift_ref[...], ext)
        acc = convb_ref[...] + convw[SSM_CONV - 1:SSM_CONV, :] * ext[pad:, :].astype(F32)
        for j in range(SSM_CONV - 1):
            acc = acc + convw[j:j + 1, :] * shifted[j * C:(j + 1) * C, :]
        xbc = _silu(acc)
        xs.append(xbc[:, :D_BRANCH])
        bmb.append(xbc[:, D_BRANCH:D_BRANCH + LANES].astype(BF16))
        cm_ref[c] = xbc[:, D_BRANCH + LANES:].astype(BF16)
    for c in range(n_chunks):
        dt = _softplus(dt_ref[sls[c], :] + dtb_ref[...])
        a_cum = _sel_dot(tri_ref[...], dt * aneg_ref[...])
        a_cum_t = a_cum.T
        col = jnp.concatenate(
            [jnp.broadcast_to(a_cum[:, h:h + 1], (C, C)) for h in range(N_HEADS)], axis=1)
        row = jnp.concatenate([a_cum_t[h:h + 1, :] for h in range(N_HEADS)], axis=1)
        lcat.append(jnp.exp(col - row + negmask_ref[...]))
        dth.append(_dot_sel(dt, exph))
        ach.append(_dot_sel(a_cum, exph))
    heads_per_group = N_HEADS // SSM_GROUPS
    for c in range(n_chunks):
        sg = _dot_nt(cm_ref[c], _pair_stack(bmb[c], 0, pmask))
        s4 = jnp.concatenate([sg[:, (h // heads_per_group) * C:(h // heads_per_group + 1) * C]
                              for h in range(N_HEADS)], axis=1)
        p = (s4 * lcat[c]).astype(BF16)
        xdt = xs[c] * dth[c]
        xdtb = xdt.astype(BF16)
        y = jnp.concatenate(
            [_dot(p[:, g * 2 * C:(g + 1) * 2 * C], _pair_stack(xdtb, g, pmask)) for g in range(halves)], axis=1)
        part_ref[sls[c], :] = y + xs[c] * dskip_ref[...]
        alast = ach[c][C - 1:C, :]
        u_ref[c] = smask_ref[...] * _dot_tn(bmb[c], (xdt * jnp.exp(alast - ach[c])).astype(BF16))
        ea_ref[c] = jnp.exp(ach[c])

    xp_ref[0:pad, :] = xbc_ref[rows - pad:rows, :]

    st = state_ref[...]
    for c in range(n_chunks):
        sb_ref[c] = st.astype(BF16)
        st = st * ea_ref[c, C - 1:C, :] + u_ref[c]
    state_ref[...] = st

    for c in range(n_chunks):
        y = part_ref[sls[c], :] + _dot(cm_ref[c], sb_ref[c]) * ea_ref[c]
        g = y * _silu(z_ref[sls[c], :].astype(F32))
        outs = []
        for grp in range(SSM_GROUPS):
            gg = g[:, grp * LANES:(grp + 1) * LANES]
            ms = jnp.mean(gg * gg, axis=-1, keepdims=True)
            outs.append(gg * lax.rsqrt(ms + RMS_EPS))
        o_ref[sls[c], :] = (jnp.concatenate(outs, axis=1) * nw_ref[...]).astype(o_ref.dtype)


def _ssd_consts():
    C = SSD_CHUNK
    idx = np.arange(C)
    tri = (idx[:, None] >= idx[None, :]).astype(np.float32)
    negmask = np.tile(np.where(tri > 0, 0.0, NEG_BIG).astype(np.float32), (1, N_HEADS))
    exph = np.zeros((LANES, D_BRANCH), np.float32)
    exph[:N_HEADS] = _head_lane_masks()
    heads_per_group = N_HEADS // SSM_GROUPS
    smask = np.zeros((LANES, D_BRANCH), np.float32)
    for h in range(N_HEADS):
        g = h // heads_per_group
        smask[g * SSM_STATE:(g + 1) * SSM_STATE, h * HEAD_DIM:(h + 1) * HEAD_DIM] = 1.0
    shift = np.zeros(((SSM_CONV - 1) * C, C + BF16_ROWS), np.float32)
    for j in range(SSM_CONV - 1):
        shift[j * C + idx, BF16_ROWS - (SSM_CONV - 1) + j + idx] = 1.0
    return (jnp.asarray(tri, BF16), jnp.asarray(negmask), jnp.asarray(exph, BF16),
            _pair_masks(), jnp.asarray(smask), jnp.asarray(shift, BF16))


def _ssd(proj, aux, conv_w, conv_b, dt_bias, a_log, d_skip, norm_w, n_batch, seq):
    lane_pad = LANES - N_HEADS
    dtb = jnp.pad(dt_bias.astype(F32), (0, lane_pad)).reshape(1, LANES)
    aneg = jnp.pad(-jnp.exp(a_log.astype(F32)), (0, lane_pad)).reshape(1, LANES)
    dskip = jnp.repeat(d_skip.astype(F32), HEAD_DIM).reshape(1, D_BRANCH)
    consts = [conv_w.astype(F32), conv_b.reshape(1, SSM_XBC).astype(F32), dtb, aneg, dskip,
              norm_w.reshape(1, D_BRANCH).astype(F32), *_ssd_consts()]
    n_chunks = MIX_ROWS // SSD_CHUNK
    state_shape = (SSM_GROUPS * SSM_STATE, D_BRANCH)
    scratch = [pltpu.VMEM((MIX_ROWS + BF16_ROWS, SSM_XBC), BF16),
               pltpu.VMEM(state_shape, F32),
               pltpu.VMEM((n_chunks,) + state_shape, F32),
               pltpu.VMEM((n_chunks,) + state_shape, BF16),
               pltpu.VMEM((n_chunks, SSD_CHUNK, D_BRANCH), F32),
               pltpu.VMEM((n_chunks, SSD_CHUNK, LANES), BF16),
               pltpu.VMEM((MIX_ROWS, D_BRANCH), F32)]
    cols = [(proj, SSM_XBC, 0), (proj, D_BRANCH, COL_Z), (aux, LANES, AUX_COL_DT)]
    return _mixer_call(_ssd_kernel, cols, consts, scratch, n_batch, seq, "ssd_mixer")


def _seg_mean(x, avg):
    return _dot_sel(x, avg)


def _ret_kernel(q_ref, k_ref, v_ref, g_ref, dcat_ref, qdec_ref, kdec_ref, cg_ref, pmask_ref,
                bd_ref, avg_ref, o_ref, state_ref, u_ref, rb_ref, part_ref):
    C = RET_CHUNK
    rows = q_ref.shape[0]
    n_chunks = rows // C
    halves = D_BRANCH // LANES
    sls = [slice(c * C, (c + 1) * C) for c in range(n_chunks)]

    @pl.when(pl.program_id(1) == 0)
    def _():
        state_ref[...] = jnp.zeros_like(state_ref)

    pmask = pmask_ref[...]
    for c in range(n_chunks):
        qb, kb, vb = q_ref[sls[c], :], k_ref[sls[c], :], v_ref[sls[c], :]
        outs = []
        for g in range(halves):
            scores = _dot_nt(qb[:, g * LANES:(g + 1) * LANES], _pair_stack(kb, g, pmask))
            p = (scores * dcat_ref[:, g * 2 * C:(g + 1) * 2 * C]).astype(BF16)
            outs.append(_dot(p, _pair_stack(vb, g, pmask)))
        part_ref[sls[c], :] = jnp.concatenate(outs, axis=1)
        u_ref[c] = bd_ref[...] * _dot_tn((kb.astype(F32) * kdec_ref[...]).astype(BF16), vb)

    r = state_ref[...]
    for c in range(n_chunks):
        rb_ref[c] = r.astype(BF16)
        r = r * cg_ref[...] + u_ref[c]
    state_ref[...] = r

    for c in range(n_chunks):
        y = part_ref[sls[c], :] + _dot((q_ref[sls[c], :].astype(F32) * qdec_ref[...]).astype(BF16), rb_ref[c])
        mu = _seg_mean(y, avg_ref[...])
        yc = y - mu
        var = _seg_mean(yc * yc, avg_ref[...])
        o_ref[sls[c], :] = (yc * lax.rsqrt(var + LN_EPS) * _silu(g_ref[sls[c], :].astype(F32))).astype(o_ref.dtype)


def _ret_consts():
    C = RET_CHUNK
    log_g = jnp.log(1.0 - 2.0 ** (-5.0 - jnp.arange(N_HEADS, dtype=F32)))
    pos = jnp.arange(C, dtype=F32)
    dist = pos[:, None] - pos[None, :]
    intra = jnp.where(dist >= 0, jnp.exp(log_g[:, None, None] * jnp.maximum(dist, 0.0)), 0.0)
    scale = HEAD_DIM ** -0.5
    dcat = jnp.concatenate([intra[h] for h in range(N_HEADS)], axis=1) * scale
    k_decay = jnp.exp(log_g[:, None] * (C - 1.0 - pos)[None, :]) * scale
    q_decay = jnp.exp(log_g[:, None] * (pos + 1.0)[None, :])
    kdec = jnp.repeat(k_decay.T, HEAD_DIM, axis=1)
    qdec = jnp.repeat(q_decay.T, HEAD_DIM, axis=1)
    cg = jnp.repeat(jnp.exp(log_g * C), HEAD_DIM).reshape(1, D_BRANCH)
    return (dcat, qdec, kdec, cg, _pair_masks(),
            jnp.asarray(_head_block_diag()), jnp.asarray(_head_block_diag(1.0 / HEAD_DIM), BF16))


def _retention(proj, n_batch, seq):
    n_chunks = MIX_ROWS // RET_CHUNK
    scratch = [pltpu.VMEM((D_BRANCH, D_BRANCH), F32),
               pltpu.VMEM((n_chunks, D_BRANCH, D_BRANCH), F32),
               pltpu.VMEM((n_chunks, D_BRANCH, D_BRANCH), BF16),
               pltpu.VMEM((MIX_ROWS, D_BRANCH), F32)]
    cols = [(proj, D_BRANCH, COL_RET + i) for i in range(4)]
    return _mixer_call(_ret_kernel, cols, list(_ret_consts()), scratch, n_batch, seq, "retention_mixer")


def _hgrn_levels():
    return int(math.log2(HGRN_CHUNK))


def _hgrn_kernel(q_ref, i_ref, g_ref, f_ref, lb_ref, nw_ref, mstack_ref, lmask_ref, pmask_ref,
                 bd_ref, avg_ref, o_ref, state_ref, e_ref, u_ref, stb_ref, part_ref):
    C = HGRN_CHUNK
    nl = _hgrn_levels()
    rows = q_ref.shape[0]
    halves = D_BRANCH // LANES

    @pl.when(pl.program_id(1) == 0)
    def _():
        state_ref[...] = jnp.zeros_like(state_ref)

    pmask = pmask_ref[...]
    lb = lb_ref[...]

    def head_scores(qb, kb):
        return jnp.concatenate(
            [_dot_nt(qb[:, g * LANES:(g + 1) * LANES], _pair_stack(kb, g, pmask)) for g in range(halves)], axis=1)

    n_chunks = rows // C
    sls = [slice(c * C, (c + 1) * C) for c in range(n_chunks)]

    k_rows = slice((nl + 2) * C, (nl + 3) * C)
    for c in range(n_chunks):
        f = lb + (1.0 - lb) * _sigmoid(f_ref[sls[c], :])
        e_ref[c, 0:(nl + 2) * C, :] = jnp.exp(_sel_dot(mstack_ref[...], jnp.log(f)))
        e_ref[c, k_rows, :] = 1.0 - f

    group = 2
    for c0 in range(0, n_chunks, group):
        cs = range(c0, min(c0 + group, n_chunks))
        att = {c: lmask_ref[nl] * head_scores(q_ref[sls[c], :], e_ref[c, k_rows, :].astype(BF16))
               for c in cs}
        for j in range(nl):
            for c in cs:
                ej = e_ref[c, j * C:(j + 1) * C, :]
                att[c] = att[c] + lmask_ref[j] * head_scores((q_ref[sls[c], :].astype(F32) * ej).astype(BF16),
                                                             (e_ref[c, k_rows, :] * ej).astype(BF16))
        for c in cs:
            attb = att[c].astype(BF16)
            vb = i_ref[sls[c], :]
            part_ref[sls[c], :] = jnp.concatenate(
                [_dot(attb[:, g * 2 * C:(g + 1) * 2 * C], _pair_stack(vb, g, pmask)) for g in range(halves)], axis=1)

    for c in range(n_chunks):
        er = e_ref[c, (nl + 1) * C:(nl + 2) * C, :]
        u_ref[c] = bd_ref[...] * _dot_tn(i_ref[sls[c], :], (e_ref[c, k_rows, :] * er).astype(BF16))

    st = state_ref[...]
    for c in range(n_chunks):
        stb_ref[c] = st.astype(BF16)
        st = st * e_ref[c, (nl + 1) * C - 1:(nl + 1) * C, :] + u_ref[c]
    state_ref[...] = st

    for c in range(n_chunks):
        eb = e_ref[c, nl * C:(nl + 1) * C, :]
        o = part_ref[sls[c], :] + _dot_nt((q_ref[sls[c], :].astype(F32) * eb).astype(BF16), stb_ref[c])
        ms = _seg_mean(o * o, avg_ref[...])
        o_ref[sls[c], :] = (o * lax.rsqrt(ms + RMS_EPS) * nw_ref[...]
                            * _silu(g_ref[sls[c], :].astype(F32))).astype(o_ref.dtype)


def _hgrn_consts():
    C = HGRN_CHUNK
    nl = _hgrn_levels()
    t = np.arange(C)[:, None]
    u = np.arange(C)[None, :]
    blocks, masks = [], []
    for j in range(nl):
        s = 1 << j
        bnd = (t // (2 * s)) * (2 * s) + s - 1
        upper = (t % (2 * s)) >= s
        m = np.where(upper, (u > bnd) & (u <= t), (u > t) & (u <= bnd))
        blocks.append(m.astype(np.float32))
        pair = (t // (2 * s)) == (u // (2 * s))
        masks.append((upper & pair & ((u % (2 * s)) < s)).astype(np.float32))
    blocks.append((u <= t).astype(np.float32))
    blocks.append((u > t).astype(np.float32))
    masks.append((u == t).astype(np.float32))
    mstack = np.concatenate(blocks, axis=0)
    lmask = np.stack([np.tile(m, (1, N_HEADS)) for m in masks])
    return (jnp.asarray(mstack, BF16), jnp.asarray(lmask), _pair_masks(),
            jnp.asarray(_head_block_diag()), jnp.asarray(_head_block_diag(1.0 / HEAD_DIM), BF16))


def _hgrn2(proj, aux, lb, norm_w, n_batch, seq):
    consts = [lb.reshape(1, D_BRANCH).astype(F32), norm_w.reshape(1, D_BRANCH).astype(F32), *_hgrn_consts()]
    scratch = [pltpu.VMEM((D_BRANCH, D_BRANCH), F32),
               pltpu.VMEM((MIX_ROWS // HGRN_CHUNK, (_hgrn_levels() + 3) * HGRN_CHUNK, D_BRANCH), F32),
               pltpu.VMEM((MIX_ROWS // HGRN_CHUNK, D_BRANCH, D_BRANCH), F32),
               pltpu.VMEM((MIX_ROWS // HGRN_CHUNK, D_BRANCH, D_BRANCH), BF16),
               pltpu.VMEM((MIX_ROWS, D_BRANCH), F32)]
    cols = [(proj, D_BRANCH, COL_HGRN + i) for i in range(3)] + [(aux, D_BRANCH, AUX_COL_F)]
    return _mixer_call(_hgrn_kernel, cols, consts, scratch, n_batch, seq, "hgrn2_mixer")


MOBA_VROWS = 80
MOBA_BODY_POSITIONS = 4


def _moba_kernel(q_ref, k_ref, v_ref, g_ref, ksel_ref, kpos_ref, kblk_ref, qaug_ref, o_ref,
                 kaug_ref, vt_ref, kmean_ref, sel_ref, qt_ref, acc_ref, m_ref, mcur_ref, alpha_ref,
                 s_ref, p_ref):
    S = MOBA_BLOCK
    i = pl.program_id(1)
    nbp = kmean_ref.shape[0]

    @pl.when(i == 0)
    def _():
        kmean_ref[...] = jnp.zeros_like(kmean_ref)

    kb = k_ref[...]
    kmean_ref[pl.ds(i, 1), :] = jnp.mean(kb.astype(F32), axis=0, keepdims=True)
    v_t = v_ref[...].astype(F32).T
    q_t = (q_ref[...].astype(F32) * (HEAD_DIM ** -0.5)).T
    ones_pad = (lax.broadcasted_iota(jnp.int32, (MOBA_VROWS - HEAD_DIM, S), 0) == 0).astype(F32)
    blk = lax.broadcasted_iota(jnp.int32, (nbp, S), 0)
    blk_f = blk.astype(F32)
    past = blk < i
    kmean = kmean_ref[...]
    own_rows = (lax.broadcasted_iota(jnp.int32, (SUBLANES, S), 0) == 0).astype(F32)
    kextra = kpos_ref[...] + i.astype(F32) * kblk_ref[...]
    for h in range(N_HEADS):
        hs = slice(h * HEAD_DIM, (h + 1) * HEAD_DIM)
        kaug_ref[i, h] = (_dot(kb, ksel_ref[h]) + kextra).astype(BF16)
        vt_ref[i, h] = jnp.concatenate([v_t[hs, :], ones_pad], axis=0).astype(BF16)
        qh_t = q_t[hs, :]
        qt_ref[h] = jnp.concatenate([qh_t * LOG2E, qaug_ref[h]], axis=0).astype(BF16)
        km_hi, km_mid, _ = _split3(kmean[:, hs])
        q_hi, q_mid, _ = _split3(qh_t)
        gate = _dot(km_hi, q_hi) + _dot(km_hi, q_mid) + _dot(km_mid, q_hi)
        gate = jnp.where(past, gate, -jnp.inf)
        sel = jnp.zeros((nbp, S), F32)
        for _ in range(MOBA_TOPK):
            best = jnp.max(gate, axis=0, keepdims=True)
            first = jnp.min(jnp.where(gate == best, blk_f, float(nbp)), axis=0, keepdims=True)
            hit = blk_f == first
            sel = jnp.where(hit, 1.0, sel)
            gate = jnp.where(hit, -jnp.inf, gate)
        sel_ref[h, 0:nbp, :] = jnp.where(past, sel, 0.0)
        sel_ref[h, nbp:nbp + SUBLANES, :] = own_rows

    def block_of(t):
        return jnp.where(t == 0, i, jnp.minimum(t - 1, i))

    def sel_row_of(t):
        return jnp.where(t == 0, nbp, jnp.where(t <= i, t - 1, nbp + 1))

    def track_max(h, slot, s_t, chosen):
        m_old = m_ref[h]
        m_new = jnp.where(chosen, jnp.maximum(m_old, jnp.max(s_t, axis=0, keepdims=True)), m_old)
        m_ref[h] = m_new
        mcur_ref[slot, h] = m_new
        alpha_ref[slot, h] = jnp.exp2(m_old - m_new)

    def score_phase(t, slot):
        n, row = block_of(t), sel_row_of(t)
        for h in range(N_HEADS):
            s_t = _dot(kaug_ref[n, h], qt_ref[h])
            s_ref[slot, h] = s_t
            track_max(h, slot, s_t, sel_ref[h, pl.ds(row, 1), :] > 0.0)

    def value_phase(t, slot):
        n, row = block_of(t), sel_row_of(t)
        for h in range(N_HEADS):
            p_ref[h] = jnp.exp2(s_ref[slot, h] - mcur_ref[slot, h]).astype(BF16)
        for h in range(N_HEADS):
            chosen = sel_ref[h, pl.ds(row, 1), :] > 0.0
            upd = _dot(vt_ref[n, h], p_ref[h])
            acc_ref[h] = alpha_ref[slot, h] * acc_ref[h] + jnp.where(chosen, upd, 0.0)

    key_pos = lax.broadcasted_iota(jnp.int32, (S, S), 0)
    qry_pos = lax.broadcasted_iota(jnp.int32, (S, S), 1)
    for h in range(N_HEADS):
        acc_ref[h] = jnp.zeros((MOBA_VROWS, S), F32)
        m_ref[h] = jnp.full((1, S), NEG_BIG, F32)
        s_t = jnp.where(key_pos <= qry_pos, _dot(kaug_ref[i, h], qt_ref[h]), NEG_BIG)
        s_ref[0, h] = s_t
        track_max(h, 0, s_t, True)

    def visit_positions(j, carry):
        t0 = MOBA_BODY_POSITIONS * j
        for r in range(0, MOBA_BODY_POSITIONS, 2):
            score_phase(t0 + r + 1, 1)
            value_phase(t0 + r, 0)
            score_phase(t0 + r + 2, 0)
            value_phase(t0 + r + 1, 1)
        return carry

    lax.fori_loop(0, (i + MOBA_BODY_POSITIONS) // MOBA_BODY_POSITIONS, visit_positions, 0)

    outs = []
    for h in range(N_HEADS):
        acc = acc_ref[h]
        outs.append(acc[:HEAD_DIM, :] / acc[HEAD_DIM:HEAD_DIM + 1, :])
    o_ref[...] = (jnp.concatenate(outs, axis=0).T * _silu(g_ref[...].astype(F32))).astype(o_ref.dtype)


def _moba(proj, n_batch, seq):
    S = MOBA_BLOCK
    assert seq % S == 0
    nb = seq // S
    nbp = -(-nb // SUBLANES) * SUBLANES
    ksel = np.zeros((N_HEADS, D_BRANCH, LANES), np.float32)
    for h in range(N_HEADS):
        ksel[h, h * HEAD_DIM + np.arange(HEAD_DIM), np.arange(HEAD_DIM)] = 1.0
    n_split = 3
    kpos = np.zeros((S, LANES), np.float32)
    kblk = np.zeros((S, LANES), np.float32)
    kpos[:, HEAD_DIM:HEAD_DIM + n_split] = np.arange(S, dtype=np.float32)[:, None]
    kblk[:, HEAD_DIM + n_split:HEAD_DIM + 2 * n_split] = 1.0
    slopes = 2.0 ** (-8.0 * np.arange(1, N_HEADS + 1, dtype=np.float32) / N_HEADS)
    qaug = np.zeros((N_HEADS, LANES - HEAD_DIM, S), np.float32)
    for h in range(N_HEADS):
        for base, coef in ((0, slopes[h] * LOG2E), (n_split, slopes[h] * LOG2E * S)):
            rem = np.float32(coef)
            for r in range(n_split):
                piece = np.float32(np.asarray(rem).astype(BF16))
                qaug[h, base + r, :] = piece
                rem = np.float32(rem - piece)
    consts = [jnp.asarray(ksel, BF16), jnp.asarray(kpos), jnp.asarray(kblk), jnp.asarray(qaug)]
    scratch = [pltpu.VMEM((nb, N_HEADS, S, LANES), BF16),
               pltpu.VMEM((nb, N_HEADS, MOBA_VROWS, S), BF16),
               pltpu.VMEM((nbp, D_BRANCH), F32),
               pltpu.VMEM((N_HEADS, nbp + SUBLANES, S), F32),
               pltpu.VMEM((N_HEADS, LANES, S), BF16),
               pltpu.VMEM((N_HEADS, MOBA_VROWS, S), F32),
               pltpu.VMEM((N_HEADS, 1, S), F32),
               pltpu.VMEM((2, N_HEADS, 1, S), F32),
               pltpu.VMEM((2, N_HEADS, 1, S), F32),
               pltpu.VMEM((2, N_HEADS, S, S), F32),
               pltpu.VMEM((N_HEADS, S, S), BF16)]
    vmem = nb * N_HEADS * S * (LANES + MOBA_VROWS) * 2 + (24 << 20)
    cols = [(proj, D_BRANCH, COL_MOBA + i) for i in range(4)]
    return _mixer_call(_moba_kernel, cols, consts, scratch, n_batch, seq, "moba_mixer", rows=S, vmem=vmem)


def _pack_w_in(w_in):
    depth, d, n_in = w_in.shape
    wt = jnp.transpose(w_in, (2, 0, 1)).reshape(n_in, depth, d // LANES, LANES)
    o = np.cumsum((0, D_BRANCH, SSM_XBC, N_HEADS) + (D_BRANCH,) * 12)
    z, xbc, dt = wt[o[0]:o[1]], wt[o[1]:o[2]], wt[o[2]:o[3]]
    hq, hf, hig, ret_moba = wt[o[3]:o[4]], wt[o[4]:o[5]], wt[o[5]:o[7]], wt[o[7]:]
    pad = jnp.zeros((N_PROJ - n_in,) + wt.shape[1:], wt.dtype)
    packed = jnp.concatenate([xbc, z, hq, hig, ret_moba, hf, dt, pad], axis=0)
    return jnp.transpose(packed, (1, 0, 2, 3)).reshape(depth, N_PROJ * (d // LANES), LANES)


def kernel(x, emb_ln_g, emb_ln_b, w_in, ssd_conv_w, ssd_conv_b, ssd_dt_bias, ssd_a_log, ssd_d, ssd_norm_w,
           hgrn_lb_logits, hgrn_norm_w, w_out, ln_g, ln_b):
    n_batch, seq, d_model = x.shape
    depth = w_in.shape[0]
    alpha = (2.0 * depth) ** 0.25
    assert seq % MIX_ROWS == 0 and seq % MOBA_BLOCK == 0

    lbs = jnp.cumsum(jax.nn.softmax(hgrn_lb_logits.astype(F32), axis=0), axis=0)
    lbs = lbs - lbs[0]

    wt_all = _pack_w_in(w_in.astype(F32))
    h = x.reshape(n_batch * seq, d_model)
    for l in range(depth):
        if l == 0:
            proj, aux, h = _inproj(h, wt_all, l, norm=(emb_ln_g, emb_ln_b))
        else:
            proj, aux = _inproj(h, wt_all, l)
        ys = [_ssd(proj, aux, ssd_conv_w[l], ssd_conv_b[l], ssd_dt_bias[l], ssd_a_log[l], ssd_d[l],
                   ssd_norm_w[l], n_batch, seq),
              _hgrn2(proj, aux, lbs[l], hgrn_norm_w[l], n_batch, seq),
              _retention(proj, n_batch, seq),
              _moba(proj, n_batch, seq)]
        h = _outproj(h, ys, w_out[l].astype(BF16), ln_g[l], ln_b[l], alpha)
    return h.reshape(n_batch, seq, d_model)
```

```python
import functools
import math

import numpy as np
import jax
import jax.numpy as jnp
from jax import lax
from jax.experimental import pallas as pl
from jax.experimental.pallas import tpu as pltpu

F32 = jnp.float32
BF16 = jnp.bfloat16

D_BRANCH = 256
HEAD_DIM = 64
N_HEADS = 4
SSM_STATE = 64
SSM_GROUPS = 2
SSM_CONV = 4
SSM_XBC = D_BRANCH + 2 * SSM_GROUPS * SSM_STATE
MOBA_BLOCK = 256
MOBA_TOPK = 3
LN_EPS = 1e-5
RMS_EPS = 1e-6
NEG_BIG = -1e30
LOG2E = 1.4426950408889634

LANES = 128
SUBLANES = 8
BF16_ROWS = 16
VMEM_BYTES_V7X = 64 * 1024 * 1024

SSD_CHUNK = 128
RET_CHUNK = 256
HGRN_CHUNK = 128
MIX_ROWS = 1024
LAYER_END_ROWS = 256

N_PROJ = 4096
N_AUX = 512
COL_Z = SSM_XBC // D_BRANCH
COL_HGRN = COL_Z + 1
COL_RET = COL_HGRN + 3
COL_MOBA = COL_RET + 4
AUX_COL_F = 0
AUX_COL_DT = D_BRANCH // LANES


def _dot(a, b):
    return jnp.dot(a, b, preferred_element_type=F32)


def _dot_nt(a, b):
    return lax.dot_general(a, b, (((1,), (1,)), ((), ())), preferred_element_type=F32)


def _dot_tn(a, b):
    return lax.dot_general(a, b, (((0,), (0,)), ((), ())), preferred_element_type=F32)


def _split3(x):
    hi = x.astype(BF16)
    r1 = x - hi.astype(F32)
    mid = r1.astype(BF16)
    lo = (r1 - mid.astype(F32)).astype(BF16)
    return hi, mid, lo


def _split2(x):
    hi = x.astype(BF16)
    return hi, (x - hi.astype(F32)).astype(BF16)


def _sel_dot(sel, x):
    hi, lo = _split2(x)
    return _dot(sel, hi) + _dot(sel, lo)


def _dot_sel(x, sel):
    hi, lo = _split2(x)
    return _dot(hi, sel) + _dot(lo, sel)


def _sigmoid(x):
    return 1.0 / (1.0 + jnp.exp2(x * -LOG2E))


def _silu(x):
    return x * _sigmoid(x)


def _softplus(x):
    return jnp.maximum(x, 0.0) + jnp.log(1.0 + jnp.exp(-jnp.abs(x)))


def _pair_masks():
    pm = np.zeros((LANES // HEAD_DIM, LANES), np.float32)
    for p in range(LANES // HEAD_DIM):
        pm[p, p * HEAD_DIM:(p + 1) * HEAD_DIM] = 1.0
    return jnp.asarray(pm, BF16)


def _pair_stack(xb, g, pmask):
    xg = xb[:, g * LANES:(g + 1) * LANES]
    return jnp.concatenate([xg * pmask[p:p + 1, :] for p in range(pmask.shape[0])], axis=0)


def _row_stack(x, masks):
    return jnp.concatenate([x * masks[h:h + 1, :] for h in range(masks.shape[0])], axis=0)


def _params(n_axes, vmem_bytes):
    return pltpu.CompilerParams(dimension_semantics=("arbitrary",) * n_axes,
                                vmem_limit_bytes=int(vmem_bytes))


def _ln_rows(x, g, b):
    mu = jnp.mean(x, axis=-1, keepdims=True)
    xc = x - mu
    var = jnp.mean(xc * xc, axis=-1, keepdims=True)
    return xc * lax.rsqrt(var + LN_EPS) * g + b


def _inproj_kernel(*refs, tn, lead, alpha):
    if lead == "plain":
        x_ref, wt_ref, o_ref, aux_ref, xb_ref, wtb_ref = refs
    elif lead == "norm":
        x_ref, wt_ref, g_ref, b_ref, o_ref, aux_ref, h_ref, xb_ref, wtb_ref = refs
    else:
        (x_ref, wt_ref, g_ref, b_ref, y0_ref, y1_ref, y2_ref, y3_ref, wo_ref,
         o_ref, aux_ref, h_ref, xb_ref, wtb_ref) = refs
    i, j = pl.program_id(0), pl.program_id(1)
    k_tiles = x_ref.shape[1] // LANES
    cols = pl.ds(pl.multiple_of(j * tn, tn), tn)

    @pl.when(j == 0)
    def _():
        if lead == "plain":
            xb_ref[...] = x_ref[...].astype(BF16)
        elif lead == "norm":
            h = _ln_rows(x_ref[...], g_ref[...], b_ref[...])
            h_ref[...] = h
            xb_ref[...] = h.astype(BF16)
        else:
            _layer_end_rows(x_ref, (y0_ref, y1_ref, y2_ref, y3_ref), wo_ref, g_ref, b_ref, h_ref, alpha)
            xb_ref[...] = h_ref[...].astype(BF16)

    @pl.when(i == 0)
    def _():
        wtb_ref[cols, :] = jnp.concatenate(
            [wt_ref[pl.ds(kt, tn, stride=k_tiles), :] for kt in range(k_tiles)], axis=1).astype(BF16)

    acc = _dot_nt(xb_ref[...], wtb_ref[cols, :])
    o_ref[...] = acc.astype(BF16)

    @pl.when(j == pl.num_programs(1) - 1)
    def _():
        aux_ref[...] = acc[:, tn - aux_ref.shape[1]:]


def _inproj(x2, wt_all, layer, norm=None, layer_end=None, tm=1024, tn=1024):
    m, d = x2.shape
    rows = wt_all.shape[1]
    n = rows * LANES // d
    n_col_tiles = n // tn
    tile_rows = tn * d // LANES
    assert N_AUX <= tn
    vmem = (2 * tm * d * 4 + 2 * tile_rows * LANES * 4 + 2 * tm * tn * 2 + 2 * tm * N_AUX * 4 + tm * tn * 4
            + tm * d * 2 + n * d * 2 + (6 << 20))
    row_spec = pl.BlockSpec((tm, d), lambda i, j: (i, 0))
    vec_spec = pl.BlockSpec((1, d), lambda i, j: (0, 0))
    in_specs = [row_spec,
                pl.BlockSpec((None, tile_rows, LANES),
                             lambda i, j: (layer, jnp.where(i == 0, j, n_col_tiles - 1), 0))]
    out_specs = [pl.BlockSpec((tm, tn), lambda i, j: (i, j)),
                 pl.BlockSpec((tm, N_AUX), lambda i, j: (i, 0))]
    out_shape = [jax.ShapeDtypeStruct((m, n), BF16), jax.ShapeDtypeStruct((m, N_AUX), F32)]
    operands = [x2, wt_all]
    lead, alpha = "plain", None
    if norm is not None:
        lead = "norm"
        in_specs += [vec_spec, vec_spec]
        out_specs.append(row_spec)
        out_shape.append(jax.ShapeDtypeStruct((m, d), F32))
        operands += [norm[0].reshape(1, d), norm[1].reshape(1, d)]
        vmem += 2 * tm * d * 4
    if layer_end is not None:
        lead = "layer_end"
        ys, w_out, alpha = layer_end
        dm = w_out.shape[0]
        in_specs += [pl.BlockSpec((tm, D_BRANCH), lambda i, j: (i, 0))] * len(ys)
        in_specs.append(pl.BlockSpec((dm, d), lambda i, j: (0, 0)))
        operands += list(ys) + [w_out]
        vmem += 2 * tm * dm * 2 + 2 * dm * d * 2
    return pl.pallas_call(
        functools.partial(_inproj_kernel, tn=tn, lead=lead, alpha=alpha),
        grid=(m // tm, n_col_tiles),
        in_specs=in_specs,
        out_specs=out_specs,
        out_shape=out_shape,
        scratch_shapes=[pltpu.VMEM((tm, d), BF16), pltpu.VMEM((n, d), BF16)],
        compiler_params=_params(2, vmem),
        name="in_projection",
    )(*operands)


def _layer_end_rows(h_ref, y_refs, w_ref, g_ref, b_ref, o_ref, alpha):
    rows = h_ref.shape[0]
    for r in range(0, rows, LAYER_END_ROWS):
        sl = slice(r, r + LAYER_END_ROWS)
        ycat = jnp.concatenate([y_ref[sl, :] for y_ref in y_refs], axis=1)
        y = _dot(ycat, w_ref[...])
        o_ref[sl, :] = _ln_rows(alpha * h_ref[sl, :] + y, g_ref[...], b_ref[...])


def _outproj_kernel(h_ref, y0_ref, y1_ref, y2_ref, y3_ref, w_ref, g_ref, b_ref, o_ref, *, alpha):
    _layer_end_rows(h_ref, (y0_ref, y1_ref, y2_ref, y3_ref), w_ref, g_ref, b_ref, o_ref, alpha)


def _outproj(h2, ys, w_bf16, g, b, alpha, tm=512):
    m, d = h2.shape
    dm = w_bf16.shape[0]
    yspec = pl.BlockSpec((tm, D_BRANCH), lambda i: (i, 0))
    vmem = 2 * (2 * tm * d * 4 + tm * dm * 2 + dm * d * 2) + tm * d * 8 + (4 << 20)
    return pl.pallas_call(
        functools.partial(_outproj_kernel, alpha=alpha),
        grid=(m // tm,),
        in_specs=[pl.BlockSpec((tm, d), lambda i: (i, 0)), yspec, yspec, yspec, yspec,
                  pl.BlockSpec((dm, d), lambda i: (0, 0)),
                  pl.BlockSpec((1, d), lambda i: (0, 0)),
                  pl.BlockSpec((1, d), lambda i: (0, 0))],
        out_specs=pl.BlockSpec((tm, d), lambda i: (i, 0)),
        out_shape=jax.ShapeDtypeStruct((m, d), F32),
        compiler_params=_params(1, vmem),
        name="out_projection_layernorm",
    )(h2, *ys, w_bf16, g.reshape(1, d), b.reshape(1, d))


def _head_lane_masks():
    m = np.zeros((N_HEADS, D_BRANCH), np.float32)
    for h in range(N_HEADS):
        m[h, h * HEAD_DIM:(h + 1) * HEAD_DIM] = 1.0
    return m


def _head_block_diag(value=1.0):
    hm = _head_lane_masks()
    return (hm.T @ hm) * value


def _mixer_call(kernel, col_blocks, consts, scratch, n_batch, seq, name, rows=MIX_ROWS, vmem=40 << 20):
    steps = seq // rows
    in_specs = [pl.BlockSpec((rows, w), functools.partial(lambda b, t, c: (b * steps + t, c), c=c))
                for (_, w, c) in col_blocks]
    in_specs += [pl.BlockSpec(a.shape, functools.partial(lambda b, t, nd: (0,) * nd, nd=a.ndim))
                 for a in consts]
    return pl.pallas_call(
        kernel,
        grid=(n_batch, steps),
        in_specs=in_specs,
        out_specs=pl.BlockSpec((rows, D_BRANCH), lambda b, t: (b * steps + t, 0)),
        out_shape=jax.ShapeDtypeStruct((n_batch * seq, D_BRANCH), BF16),
        scratch_shapes=scratch,
        compiler_params=_params(2, vmem),
        name=name,
    )(*[a for (a, _, _) in col_blocks], *consts)


def _ssd_kernel(xbc_ref, z_ref, dt_ref, convw_ref, convb_ref, dtb_ref, aneg_ref, dskip_ref, nw_ref,
                tri_ref, negmask_ref, exph_ref, pmask_ref, smask_ref, shift_ref,
                o_ref, xp_ref, state_ref, u_ref, sb_ref, ea_ref, cm_ref, part_ref):
    C = SSD_CHUNK
    rows = xbc_ref.shape[0]
    n_chunks = rows // C
    halves = D_BRANCH // LANES
    pad = BF16_ROWS
    sls = [slice(c * C, (c + 1) * C) for c in range(n_chunks)]

    @pl.when(pl.program_id(1) == 0)
    def _():
        xp_ref[0:pad, :] = jnp.zeros((pad, SSM_XBC), BF16)
        state_ref[...] = jnp.zeros_like(state_ref)

    xp_ref[pad:pad + rows, :] = xbc_ref[...]

    convw = convw_ref[...]
    pmask = pmask_ref[...]
    exph = exph_ref[...]
    xs, bmb, lcat, dth, ach = [], [], [], [], []
    for c in range(n_chunks):
        r0 = c * C
        ext = xp_ref[r0:r0 + C + pad, :]
        shifted = _dot(shift_ref[...], ext)
        acc = convb_ref[...] + convw[SSM_CONV - 1:SSM_CONV, :] * ext[pad:, :].astype(F32)
        for j in range(SSM_CONV - 1):
            acc = acc + convw[j:j + 1, :] * shifted[j * C:(j + 1) * C, :]
        xbc = _silu(acc)
        xs.append(xbc[:, :D_BRANCH])
        bmb.append(xbc[:, D_BRANCH:D_BRANCH + LANES].astype(BF16))
        cm_ref[c] = xbc[:, D_BRANCH + LANES:].astype(BF16)
    for c in range(n_chunks):
        dt = _softplus(dt_ref[sls[c], :] + dtb_ref[...])
        a_cum = _sel_dot(tri_ref[...], dt * aneg_ref[...])
        a_cum_t = a_cum.T
        col = jnp.concatenate(
            [jnp.broadcast_to(a_cum[:, h:h + 1], (C, C)) for h in range(N_HEADS)], axis=1)
        row = jnp.concatenate([a_cum_t[h:h + 1, :] for h in range(N_HEADS)], axis=1)
        lcat.append(jnp.exp(col - row + negmask_ref[...]))
        dth.append(_dot_sel(dt, exph))
        ach.append(_dot_sel(a_cum, exph))
    heads_per_group = N_HEADS // SSM_GROUPS
    for c in range(n_chunks):
        sg = _dot_nt(cm_ref[c], _pair_stack(bmb[c], 0, pmask))
        s4 = jnp.concatenate([sg[:, (h // heads_per_group) * C:(h // heads_per_group + 1) * C]
                              for h in range(N_HEADS)], axis=1)
        p = (s4 * lcat[c]).astype(BF16)
        xdt = xs[c] * dth[c]
        xdtb = xdt.astype(BF16)
        y = jnp.concatenate(
            [_dot(p[:, g * 2 * C:(g + 1) * 2 * C], _pair_stack(xdtb, g, pmask)) for g in range(halves)], axis=1)
        part_ref[sls[c], :] = y + xs[c] * dskip_ref[...]
        alast = ach[c][C - 1:C, :]
        u_ref[c] = smask_ref[...] * _dot_tn(bmb[c], (xdt * jnp.exp(alast - ach[c])).astype(BF16))
        ea_ref[c] = jnp.exp(ach[c])

    xp_ref[0:pad, :] = xbc_ref[rows - pad:rows, :]

    st = state_ref[...]
    for c in range(n_chunks):
        sb_ref[c] = st.astype(BF16)
        st = st * ea_ref[c, C - 1:C, :] + u_ref[c]
    state_ref[...] = st

    for c in range(n_chunks):
        y = part_ref[sls[c], :] + _dot(cm_ref[c], sb_ref[c]) * ea_ref[c]
        g = y * _silu(z_ref[sls[c], :].astype(F32))
        outs = []
        for grp in range(SSM_GROUPS):
            gg = g[:, grp * LANES:(grp + 1) * LANES]
            ms = jnp.mean(gg * gg, axis=-1, keepdims=True)
            outs.append(gg * lax.rsqrt(ms + RMS_EPS))
        o_ref[sls[c], :] = (jnp.concatenate(outs, axis=1) * nw_ref[...]).astype(o_ref.dtype)


def _ssd_consts():
    C = SSD_CHUNK
    idx = np.arange(C)
    tri = (idx[:, None] >= idx[None, :]).astype(np.float32)
    negmask = np.tile(np.where(tri > 0, 0.0, NEG_BIG).astype(np.float32), (1, N_HEADS))
    exph = np.zeros((LANES, D_BRANCH), np.float32)
    exph[:N_HEADS] = _head_lane_masks()
    heads_per_group = N_HEADS // SSM_GROUPS
    smask = np.zeros((LANES, D_BRANCH), np.float32)
    for h in range(N_HEADS):
        g = h // heads_per_group
        smask[g * SSM_STATE:(g + 1) * SSM_STATE, h * HEAD_DIM:(h + 1) * HEAD_DIM] = 1.0
    shift = np.zeros(((SSM_CONV - 1) * C, C + BF16_ROWS), np.float32)
    for j in range(SSM_CONV - 1):
        shift[j * C + idx, BF16_ROWS - (SSM_CONV - 1) + j + idx] = 1.0
    return (jnp.asarray(tri, BF16), jnp.asarray(negmask), jnp.asarray(exph, BF16),
            _pair_masks(), jnp.asarray(smask), jnp.asarray(shift, BF16))


def _ssd(proj, aux, conv_w, conv_b, dt_bias, a_log, d_skip, norm_w, n_batch, seq):
    lane_pad = LANES - N_HEADS
    dtb = jnp.pad(dt_bias.astype(F32), (0, lane_pad)).reshape(1, LANES)
    aneg = jnp.pad(-jnp.exp(a_log.astype(F32)), (0, lane_pad)).reshape(1, LANES)
    dskip = jnp.repeat(d_skip.astype(F32), HEAD_DIM).reshape(1, D_BRANCH)
    consts = [conv_w.astype(F32), conv_b.reshape(1, SSM_XBC).astype(F32), dtb, aneg, dskip,
              norm_w.reshape(1, D_BRANCH).astype(F32), *_ssd_consts()]
    n_chunks = MIX_ROWS // SSD_CHUNK
    state_shape = (SSM_GROUPS * SSM_STATE, D_BRANCH)
    scratch = [pltpu.VMEM((MIX_ROWS + BF16_ROWS, SSM_XBC), BF16),
               pltpu.VMEM(state_shape, F32),
               pltpu.VMEM((n_chunks,) + state_shape, F32),
               pltpu.VMEM((n_chunks,) + state_shape, BF16),
               pltpu.VMEM((n_chunks, SSD_CHUNK, D_BRANCH), F32),
               pltpu.VMEM((n_chunks, SSD_CHUNK, LANES), BF16),
               pltpu.VMEM((MIX_ROWS, D_BRANCH), F32)]
    cols = [(proj, SSM_XBC, 0), (proj, D_BRANCH, COL_Z), (aux, LANES, AUX_COL_DT)]
    return _mixer_call(_ssd_kernel, cols, consts, scratch, n_batch, seq, "ssd_mixer")


def _seg_mean(x, avg):
    return _dot_sel(x, avg)


def _ret_kernel(q_ref, k_ref, v_ref, g_ref, dcat_ref, qdec_ref, kdec_ref, cg_ref, pmask_ref,
                bd_ref, avg_ref, o_ref, state_ref, u_ref, rb_ref, part_ref):
    C = RET_CHUNK
    rows = q_ref.shape[0]
    n_chunks = rows // C
    halves = D_BRANCH // LANES
    sls = [slice(c * C, (c + 1) * C) for c in range(n_chunks)]

    @pl.when(pl.program_id(1) == 0)
    def _():
        state_ref[...] = jnp.zeros_like(state_ref)

    pmask = pmask_ref[...]
    for c in range(n_chunks):
        qb, kb, vb = q_ref[sls[c], :], k_ref[sls[c], :], v_ref[sls[c], :]
        outs = []
        for g in range(halves):
            scores = _dot_nt(qb[:, g * LANES:(g + 1) * LANES], _pair_stack(kb, g, pmask))
            p = (scores * dcat_ref[:, g * 2 * C:(g + 1) * 2 * C]).astype(BF16)
            outs.append(_dot(p, _pair_stack(vb, g, pmask)))
        part_ref[sls[c], :] = jnp.concatenate(outs, axis=1)
        u_ref[c] = bd_ref[...] * _dot_tn((kb.astype(F32) * kdec_ref[...]).astype(BF16), vb)

    r = state_ref[...]
    for c in range(n_chunks):
        rb_ref[c] = r.astype(BF16)
        r = r * cg_ref[...] + u_ref[c]
    state_ref[...] = r

    for c in range(n_chunks):
        y = part_ref[sls[c], :] + _dot((q_ref[sls[c], :].astype(F32) * qdec_ref[...]).astype(BF16), rb_ref[c])
        mu = _seg_mean(y, avg_ref[...])
        yc = y - mu
        var = _seg_mean(yc * yc, avg_ref[...])
        o_ref[sls[c], :] = (yc * lax.rsqrt(var + LN_EPS) * _silu(g_ref[sls[c], :].astype(F32))).astype(o_ref.dtype)


def _ret_consts():
    C = RET_CHUNK
    log_g = jnp.log(1.0 - 2.0 ** (-5.0 - jnp.arange(N_HEADS, dtype=F32)))
    pos = jnp.arange(C, dtype=F32)
    dist = pos[:, None] - pos[None, :]
    intra = jnp.where(dist >= 0, jnp.exp(log_g[:, None, None] * jnp.maximum(dist, 0.0)), 0.0)
    scale = HEAD_DIM ** -0.5
    dcat = jnp.concatenate([intra[h] for h in range(N_HEADS)], axis=1) * scale
    k_decay = jnp.exp(log_g[:, None] * (C - 1.0 - pos)[None, :]) * scale
    q_decay = jnp.exp(log_g[:, None] * (pos + 1.0)[None, :])
    kdec = jnp.repeat(k_decay.T, HEAD_DIM, axis=1)
    qdec = jnp.repeat(q_decay.T, HEAD_DIM, axis=1)
    cg = jnp.repeat(jnp.exp(log_g * C), HEAD_DIM).reshape(1, D_BRANCH)
    return (dcat, qdec, kdec, cg, _pair_masks(),
            jnp.asarray(_head_block_diag()), jnp.asarray(_head_block_diag(1.0 / HEAD_DIM), BF16))


def _retention(proj, n_batch, seq):
    n_chunks = MIX_ROWS // RET_CHUNK
    scratch = [pltpu.VMEM((D_BRANCH, D_BRANCH), F32),
               pltpu.VMEM((n_chunks, D_BRANCH, D_BRANCH), F32),
               pltpu.VMEM((n_chunks, D_BRANCH, D_BRANCH), BF16),
               pltpu.VMEM((MIX_ROWS, D_BRANCH), F32)]
    cols = [(proj, D_BRANCH, COL_RET + i) for i in range(4)]
    return _mixer_call(_ret_kernel, cols, list(_ret_consts()), scratch, n_batch, seq, "retention_mixer")


def _hgrn_levels():
    return int(math.log2(HGRN_CHUNK))


def _hgrn_kernel(q_ref, i_ref, g_ref, f_ref, lb_ref, nw_ref, mstack_ref, lmask_ref, pmask_ref,
                 bd_ref, avg_ref, o_ref, state_ref, e_ref, u_ref, stb_ref, part_ref):
    C = HGRN_CHUNK
    nl = _hgrn_levels()
    rows = q_ref.shape[0]
    halves = D_BRANCH // LANES

    @pl.when(pl.program_id(1) == 0)
    def _():
        state_ref[...] = jnp.zeros_like(state_ref)

    pmask = pmask_ref[...]
    lb = lb_ref[...]

    def head_scores(qb, kb):
        return jnp.concatenate(
            [_dot_nt(qb[:, g * LANES:(g + 1) * LANES], _pair_stack(kb, g, pmask)) for g in range(halves)], axis=1)

    n_chunks = rows // C
    sls = [slice(c * C, (c + 1) * C) for c in range(n_chunks)]

    k_rows = slice((nl + 2) * C, (nl + 3) * C)
    for c in range(n_chunks):
        f = lb + (1.0 - lb) * _sigmoid(f_ref[sls[c], :])
        e_ref[c, 0:(nl + 2) * C, :] = jnp.exp(_sel_dot(mstack_ref[...], jnp.log(f)))
        e_ref[c, k_rows, :] = 1.0 - f

    group = 2
    for c0 in range(0, n_chunks, group):
        cs = range(c0, min(c0 + group, n_chunks))
        att = {c: lmask_ref[nl] * head_scores(q_ref[sls[c], :], e_ref[c, k_rows, :].astype(BF16))
               for c in cs}
        for j in range(nl):
            for c in cs:
                ej = e_ref[c, j * C:(j + 1) * C, :]
                att[c] = att[c] + lmask_ref[j] * head_scores((q_ref[sls[c], :].astype(F32) * ej).astype(BF16),
                                                             (e_ref[c, k_rows, :] * ej).astype(BF16))
        for c in cs:
            attb = att[c].astype(BF16)
            vb = i_ref[sls[c], :]
            part_ref[sls[c], :] = jnp.concatenate(
                [_dot(attb[:, g * 2 * C:(g + 1) * 2 * C], _pair_stack(vb, g, pmask)) for g in range(halves)], axis=1)

    for c in range(n_chunks):
        er = e_ref[c, (nl + 1) * C:(nl + 2) * C, :]
        u_ref[c] = bd_ref[...] * _dot_tn(i_ref[sls[c], :], (e_ref[c, k_rows, :] * er).astype(BF16))

    st = state_ref[...]
    for c in range(n_chunks):
        stb_ref[c] = st.astype(BF16)
        st = st * e_ref[c, (nl + 1) * C - 1:(nl + 1) * C, :] + u_ref[c]
    state_ref[...] = st

    for c in range(n_chunks):
        eb = e_ref[c, nl * C:(nl + 1) * C, :]
        o = part_ref[sls[c], :] + _dot_nt((q_ref[sls[c], :].astype(F32) * eb).astype(BF16), stb_ref[c])
        ms = _seg_mean(o * o, avg_ref[...])
        o_ref[sls[c], :] = (o * lax.rsqrt(ms + RMS_EPS) * nw_ref[...]
                            * _silu(g_ref[sls[c], :].astype(F32))).astype(o_ref.dtype)


def _hgrn_consts():
    C = HGRN_CHUNK
    nl = _hgrn_levels()
    t = np.arange(C)[:, None]
    u = np.arange(C)[None, :]
    blocks, masks = [], []
    for j in range(nl):
        s = 1 << j
        bnd = (t // (2 * s)) * (2 * s) + s - 1
        upper = (t % (2 * s)) >= s
        m = np.where(upper, (u > bnd) & (u <= t), (u > t) & (u <= bnd))
        blocks.append(m.astype(np.float32))
        pair = (t // (2 * s)) == (u // (2 * s))
        masks.append((upper & pair & ((u % (2 * s)) < s)).astype(np.float32))
    blocks.append((u <= t).astype(np.float32))
    blocks.append((u > t).astype(np.float32))
    masks.append((u == t).astype(np.float32))
    mstack = np.concatenate(blocks, axis=0)
    lmask = np.stack([np.tile(m, (1, N_HEADS)) for m in masks])
    return (jnp.asarray(mstack, BF16), jnp.asarray(lmask), _pair_masks(),
            jnp.asarray(_head_block_diag()), jnp.asarray(_head_block_diag(1.0 / HEAD_DIM), BF16))


def _hgrn2(proj, aux, lb, norm_w, n_batch, seq):
    consts = [lb.reshape(1, D_BRANCH).astype(F32), norm_w.reshape(1, D_BRANCH).astype(F32), *_hgrn_consts()]
    scratch = [pltpu.VMEM((D_BRANCH, D_BRANCH), F32),
               pltpu.VMEM((MIX_ROWS // HGRN_CHUNK, (_hgrn_levels() + 3) * HGRN_CHUNK, D_BRANCH), F32),
               pltpu.VMEM((MIX_ROWS // HGRN_CHUNK, D_BRANCH, D_BRANCH), F32),
               pltpu.VMEM((MIX_ROWS // HGRN_CHUNK, D_BRANCH, D_BRANCH), BF16),
               pltpu.VMEM((MIX_ROWS, D_BRANCH), F32)]
    cols = [(proj, D_BRANCH, COL_HGRN + i) for i in range(3)] + [(aux, D_BRANCH, AUX_COL_F)]
    return _mixer_call(_hgrn_kernel, cols, consts, scratch, n_batch, seq, "hgrn2_mixer")


MOBA_VROWS = 80
MOBA_BODY_POSITIONS = 4


def _moba_kernel(q_ref, k_ref, v_ref, g_ref, ksel_ref, kpos_ref, kblk_ref, qaug_ref, o_ref,
                 kaug_ref, vt_ref, kmean_ref, sel_ref, qt_ref, acc_ref, m_ref, mcur_ref, alpha_ref,
                 s_ref, p_ref):
    S = MOBA_BLOCK
    i = pl.program_id(1)
    nbp = kmean_ref.shape[0]

    @pl.when(i == 0)
    def _():
        kmean_ref[...] = jnp.zeros_like(kmean_ref)

    kb = k_ref[...]
    kmean_ref[pl.ds(i, 1), :] = jnp.mean(kb.astype(F32), axis=0, keepdims=True)
    v_t = v_ref[...].astype(F32).T
    q_t = (q_ref[...].astype(F32) * (HEAD_DIM ** -0.5)).T
    ones_pad = (lax.broadcasted_iota(jnp.int32, (MOBA_VROWS - HEAD_DIM, S), 0) == 0).astype(F32)
    blk = lax.broadcasted_iota(jnp.int32, (nbp, S), 0)
    blk_f = blk.astype(F32)
    past = blk < i
    kmean = kmean_ref[...]
    own_rows = (lax.broadcasted_iota(jnp.int32, (SUBLANES, S), 0) == 0).astype(F32)
    kextra = kpos_ref[...] + i.astype(F32) * kblk_ref[...]
    for h in range(N_HEADS):
        hs = slice(h * HEAD_DIM, (h + 1) * HEAD_DIM)
        kaug_ref[i, h] = (_dot(kb, ksel_ref[h]) + kextra).astype(BF16)
        vt_ref[i, h] = jnp.concatenate([v_t[hs, :], ones_pad], axis=0).astype(BF16)
        qh_t = q_t[hs, :]
        qt_ref[h] = jnp.concatenate([qh_t * LOG2E, qaug_ref[h]], axis=0).astype(BF16)
        km_hi, km_mid, _ = _split3(kmean[:, hs])
        q_hi, q_mid, _ = _split3(qh_t)
        gate = _dot(km_hi, q_hi) + _dot(km_hi, q_mid) + _dot(km_mid, q_hi)
        gate = jnp.where(past, gate, -jnp.inf)
        sel = jnp.zeros((nbp, S), F32)
        for _ in range(MOBA_TOPK):
            best = jnp.max(gate, axis=0, keepdims=True)
            first = jnp.min(jnp.where(gate == best, blk_f, float(nbp)), axis=0, keepdims=True)
            hit = blk_f == first
            sel = jnp.where(hit, 1.0, sel)
            gate = jnp.where(hit, -jnp.inf, gate)
        sel_ref[h, 0:nbp, :] = jnp.where(past, sel, 0.0)
        sel_ref[h, nbp:nbp + SUBLANES, :] = own_rows

    def block_of(t):
        return jnp.where(t == 0, i, jnp.minimum(t - 1, i))

    def sel_row_of(t):
        return jnp.where(t == 0, nbp, jnp.where(t <= i, t - 1, nbp + 1))

    def track_max(h, slot, s_t, chosen):
        m_old = m_ref[h]
        m_new = jnp.where(chosen, jnp.maximum(m_old, jnp.max(s_t, axis=0, keepdims=True)), m_old)
        m_ref[h] = m_new
        mcur_ref[slot, h] = m_new
        alpha_ref[slot, h] = jnp.exp2(m_old - m_new)

    def score_phase(t, slot):
        n, row = block_of(t), sel_row_of(t)
        for h in range(N_HEADS):
            s_t = _dot(kaug_ref[n, h], qt_ref[h])
            s_ref[slot, h] = s_t
            track_max(h, slot, s_t, sel_ref[h, pl.ds(row, 1), :] > 0.0)

    def value_phase(t, slot):
        n, row = block_of(t), sel_row_of(t)
        for h in range(N_HEADS):
            p_ref[h] = jnp.exp2(s_ref[slot, h] - mcur_ref[slot, h]).astype(BF16)
        for h in range(N_HEADS):
            chosen = sel_ref[h, pl.ds(row, 1), :] > 0.0
            upd = _dot(vt_ref[n, h], p_ref[h])
            acc_ref[h] = alpha_ref[slot, h] * acc_ref[h] + jnp.where(chosen, upd, 0.0)

    key_pos = lax.broadcasted_iota(jnp.int32, (S, S), 0)
    qry_pos = lax.broadcasted_iota(jnp.int32, (S, S), 1)
    for h in range(N_HEADS):
        acc_ref[h] = jnp.zeros((MOBA_VROWS, S), F32)
        m_ref[h] = jnp.full((1, S), NEG_BIG, F32)
        s_t = jnp.where(key_pos <= qry_pos, _dot(kaug_ref[i, h], qt_ref[h]), NEG_BIG)
        s_ref[0, h] = s_t
        track_max(h, 0, s_t, True)

    def visit_positions(j, carry):
        t0 = MOBA_BODY_POSITIONS * j
        for r in range(0, MOBA_BODY_POSITIONS, 2):
            score_phase(t0 + r + 1, 1)
            value_phase(t0 + r, 0)
            score_phase(t0 + r + 2, 0)
            value_phase(t0 + r + 1, 1)
        return carry

    lax.fori_loop(0, (i + MOBA_BODY_POSITIONS) // MOBA_BODY_POSITIONS, visit_positions, 0)

    outs = []
    for h in range(N_HEADS):
        acc = acc_ref[h]
        outs.append(acc[:HEAD_DIM, :] / acc[HEAD_DIM:HEAD_DIM + 1, :])
    o_ref[...] = (jnp.concatenate(outs, axis=0).T * _silu(g_ref[...].astype(F32))).astype(o_ref.dtype)


def _moba(proj, n_batch, seq):
    S = MOBA_BLOCK
    assert seq % S == 0
    nb = seq // S
    nbp = -(-nb // SUBLANES) * SUBLANES
    ksel = np.zeros((N_HEADS, D_BRANCH, LANES), np.float32)
    for h in range(N_HEADS):
        ksel[h, h * HEAD_DIM + np.arange(HEAD_DIM), np.arange(HEAD_DIM)] = 1.0
    n_split = 3
    kpos = np.zeros((S, LANES), np.float32)
    kblk = np.zeros((S, LANES), np.float32)
    kpos[:, HEAD_DIM:HEAD_DIM + n_split] = np.arange(S, dtype=np.float32)[:, None]
    kblk[:, HEAD_DIM + n_split:HEAD_DIM + 2 * n_split] = 1.0
    slopes = 2.0 ** (-8.0 * np.arange(1, N_HEADS + 1, dtype=np.float32) / N_HEADS)
    qaug = np.zeros((N_HEADS, LANES - HEAD_DIM, S), np.float32)
    for h in range(N_HEADS):
        for base, coef in ((0, slopes[h] * LOG2E), (n_split, slopes[h] * LOG2E * S)):
            rem = np.float32(coef)
            for r in range(n_split):
                piece = np.float32(np.asarray(rem).astype(BF16))
                qaug[h, base + r, :] = piece
                rem = np.float32(rem - piece)
    consts = [jnp.asarray(ksel, BF16), jnp.asarray(kpos), jnp.asarray(kblk), jnp.asarray(qaug)]
    scratch = [pltpu.VMEM((nb, N_HEADS, S, LANES), BF16),
               pltpu.VMEM((nb, N_HEADS, MOBA_VROWS, S), BF16),
               pltpu.VMEM((nbp, D_BRANCH), F32),
               pltpu.VMEM((N_HEADS, nbp + SUBLANES, S), F32),
               pltpu.VMEM((N_HEADS, LANES, S), BF16),
               pltpu.VMEM((N_HEADS, MOBA_VROWS, S), F32),
               pltpu.VMEM((N_HEADS, 1, S), F32),
               pltpu.VMEM((2, N_HEADS, 1, S), F32),
               pltpu.VMEM((2, N_HEADS, 1, S), F32),
               pltpu.VMEM((2, N_HEADS, S, S), F32),
               pltpu.VMEM((N_HEADS, S, S), BF16)]
    vmem = nb * N_HEADS * S * (LANES + MOBA_VROWS) * 2 + (24 << 20)
    cols = [(proj, D_BRANCH, COL_MOBA + i) for i in range(4)]
    return _mixer_call(_moba_kernel, cols, consts, scratch, n_batch, seq, "moba_mixer", rows=S, vmem=vmem)


def _pack_w_in(w_in):
    depth, d, n_in = w_in.shape
    wt = jnp.transpose(w_in, (2, 0, 1)).reshape(n_in, depth, d // LANES, LANES)
    o = np.cumsum((0, D_BRANCH, SSM_XBC, N_HEADS) + (D_BRANCH,) * 12)
    z, xbc, dt = wt[o[0]:o[1]], wt[o[1]:o[2]], wt[o[2]:o[3]]
    hq, hf, hig, ret_moba = wt[o[3]:o[4]], wt[o[4]:o[5]], wt[o[5]:o[7]], wt[o[7]:]
    pad = jnp.zeros((N_PROJ - n_in,) + wt.shape[1:], wt.dtype)
    packed = jnp.concatenate([xbc, z, hq, hig, ret_moba, hf, dt, pad], axis=0)
    return jnp.transpose(packed, (1, 0, 2, 3)).reshape(depth, N_PROJ * (d // LANES), LANES)


def kernel(x, emb_ln_g, emb_ln_b, w_in, ssd_conv_w, ssd_conv_b, ssd_dt_bias, ssd_a_log, ssd_d, ssd_norm_w,
           hgrn_lb_logits, hgrn_norm_w, w_out, ln_g, ln_b):
    n_batch, seq, d_model = x.shape
    depth = w_in.shape[0]
    alpha = (2.0 * depth) ** 0.25
    assert seq % MIX_ROWS == 0 and seq % MOBA_BLOCK == 0

    lbs = jnp.cumsum(jax.nn.softmax(hgrn_lb_logits.astype(F32), axis=0), axis=0)
    lbs = lbs - lbs[0]

    wt_all = _pack_w_in(w_in.astype(F32))
    h = x.reshape(n_batch * seq, d_model)
    ys = None
    for l in range(depth):
        if l == 0:
            proj, aux, h = _inproj(h, wt_all, l, norm=(emb_ln_g, emb_ln_b))
        else:
            proj, aux, h = _inproj(h, wt_all, l, norm=(ln_g[l - 1], ln_b[l - 1]),
                                   layer_end=(ys, w_out[l - 1].astype(BF16), alpha))
        ys = [_ssd(proj, aux, ssd_conv_w[l], ssd_conv_b[l], ssd_dt_bias[l], ssd_a_log[l], ssd_d[l],
                   ssd_norm_w[l], n_batch, seq),
              _hgrn2(proj, aux, lbs[l], hgrn_norm_w[l], n_batch, seq),
              _retention(proj, n_batch, seq),
              _moba(proj, n_batch, seq)]
    h = _outproj(h, ys, w_out[depth - 1].astype(BF16), ln_g[depth - 1], ln_b[depth - 1], alpha)
    return h.reshape(n_batch, seq, d_model)
```

```python
import functools
import math

import numpy as np
import jax
import jax.numpy as jnp
from jax import lax
from jax.experimental import pallas as pl
from jax.experimental.pallas import tpu as pltpu

F32 = jnp.float32
BF16 = jnp.bfloat16

D_BRANCH = 256
HEAD_DIM = 64
N_HEADS = 4
SSM_STATE = 64
SSM_GROUPS = 2
SSM_CONV = 4
SSM_XBC = D_BRANCH + 2 * SSM_GROUPS * SSM_STATE
MOBA_BLOCK = 256
MOBA_TOPK = 3
LN_EPS = 1e-5
RMS_EPS = 1e-6
NEG_BIG = -1e30
LOG2E = 1.4426950408889634

LANES = 128
SUBLANES = 8
BF16_ROWS = 16
VMEM_BYTES_V7X = 64 * 1024 * 1024

SSD_CHUNK = 128
RET_CHUNK = 256
HGRN_CHUNK = 128
HGRN_MATMUL_LEVELS = 3
MIX_ROWS = 1024

N_PROJ = 4096
N_AUX = 512
COL_Z = SSM_XBC // D_BRANCH
COL_HGRN = COL_Z + 1
COL_RET = COL_HGRN + 3
COL_MOBA = COL_RET + 4
AUX_COL_F = 0
AUX_COL_DT = D_BRANCH // LANES


def _dot(a, b):
    return jnp.dot(a, b, preferred_element_type=F32)


def _dot_nt(a, b):
    return lax.dot_general(a, b, (((1,), (1,)), ((), ())), preferred_element_type=F32)


def _dot_tn(a, b):
    return lax.dot_general(a, b, (((0,), (0,)), ((), ())), preferred_element_type=F32)


def _split3(x):
    hi = x.astype(BF16)
    r1 = x - hi.astype(F32)
    mid = r1.astype(BF16)
    lo = (r1 - mid.astype(F32)).astype(BF16)
    return hi, mid, lo


def _split2(x):
    hi = x.astype(BF16)
    return hi, (x - hi.astype(F32)).astype(BF16)


def _sel_dot(sel, x):
    hi, lo = _split2(x)
    return _dot(sel, hi) + _dot(sel, lo)


def _dot_sel(x, sel):
    hi, lo = _split2(x)
    return _dot(hi, sel) + _dot(lo, sel)


def _sigmoid(x):
    return 1.0 / (1.0 + jnp.exp(-x))


def _silu(x):
    return x * _sigmoid(x)


def _softplus(x):
    return jnp.maximum(x, 0.0) + jnp.log(1.0 + jnp.exp(-jnp.abs(x)))


def _pair_masks():
    pm = np.zeros((LANES // HEAD_DIM, LANES), np.float32)
    for p in range(LANES // HEAD_DIM):
        pm[p, p * HEAD_DIM:(p + 1) * HEAD_DIM] = 1.0
    return jnp.asarray(pm, BF16)


def _pair_stack(xb, g, pmask):
    xg = xb[:, g * LANES:(g + 1) * LANES]
    return jnp.concatenate([xg * pmask[p:p + 1, :] for p in range(pmask.shape[0])], axis=0)


def _row_stack(x, masks):
    return jnp.concatenate([x * masks[h:h + 1, :] for h in range(masks.shape[0])], axis=0)


def _params(n_axes, vmem_bytes):
    return pltpu.CompilerParams(dimension_semantics=("arbitrary",) * n_axes,
                                vmem_limit_bytes=int(vmem_bytes))


def _ln_rows(x, g, b):
    mu = jnp.mean(x, axis=-1, keepdims=True)
    xc = x - mu
    var = jnp.mean(xc * xc, axis=-1, keepdims=True)
    return xc * lax.rsqrt(var + LN_EPS) * g + b


def _inproj_kernel(*refs, tn, norm_first):
    if norm_first:
        x_ref, wt_ref, g_ref, b_ref, o_ref, aux_ref, h_ref, xb_ref, wtb_ref = refs
    else:
        x_ref, wt_ref, o_ref, aux_ref, xb_ref, wtb_ref = refs
    i, j = pl.program_id(0), pl.program_id(1)
    k_tiles = x_ref.shape[1] // LANES
    cols = pl.ds(pl.multiple_of(j * tn, tn), tn)

    @pl.when(j == 0)
    def _():
        if norm_first:
            h = _ln_rows(x_ref[...], g_ref[...], b_ref[...])
            h_ref[...] = h
            xb_ref[...] = h.astype(BF16)
        else:
            xb_ref[...] = x_ref[...].astype(BF16)

    @pl.when(i == 0)
    def _():
        wtb_ref[cols, :] = jnp.concatenate(
            [wt_ref[pl.ds(kt, tn, stride=k_tiles), :] for kt in range(k_tiles)], axis=1).astype(BF16)

    acc = _dot_nt(xb_ref[...], wtb_ref[cols, :])
    o_ref[...] = acc.astype(BF16)

    @pl.when(j == pl.num_programs(1) - 1)
    def _():
        aux_ref[...] = acc[:, tn - aux_ref.shape[1]:]


def _inproj(x2, wt_all, layer, norm=None, tm=1024, tn=1024):
    m, d = x2.shape
    rows = wt_all.shape[1]
    n = rows * LANES // d
    n_col_tiles = n // tn
    tile_rows = tn * d // LANES
    assert N_AUX <= tn
    vmem = (2 * tm * d * 4 + 2 * tile_rows * LANES * 4 + 2 * tm * tn * 2 + 2 * tm * N_AUX * 4 + tm * tn * 4
            + tm * d * 2 + n * d * 2 + (6 << 20))
    row_spec = pl.BlockSpec((tm, d), lambda i, j: (i, 0))
    vec_spec = pl.BlockSpec((1, d), lambda i, j: (0, 0))
    in_specs = [row_spec,
                pl.BlockSpec((None, tile_rows, LANES),
                             lambda i, j: (layer, jnp.where(i == 0, j, n_col_tiles - 1), 0))]
    out_specs = [pl.BlockSpec((tm, tn), lambda i, j: (i, j)),
                 pl.BlockSpec((tm, N_AUX), lambda i, j: (i, 0))]
    out_shape = [jax.ShapeDtypeStruct((m, n), BF16), jax.ShapeDtypeStruct((m, N_AUX), F32)]
    operands = [x2, wt_all]
    if norm is not None:
        in_specs += [vec_spec, vec_spec]
        out_specs.append(row_spec)
        out_shape.append(jax.ShapeDtypeStruct((m, d), F32))
        operands += [norm[0].reshape(1, d), norm[1].reshape(1, d)]
        vmem += 2 * tm * d * 4
    return pl.pallas_call(
        functools.partial(_inproj_kernel, tn=tn, norm_first=norm is not None),
        grid=(m // tm, n_col_tiles),
        in_specs=in_specs,
        out_specs=out_specs,
        out_shape=out_shape,
        scratch_shapes=[pltpu.VMEM((tm, d), BF16), pltpu.VMEM((n, d), BF16)],
        compiler_params=_params(2, vmem),
        name="in_projection",
    )(*operands)


def _outproj_kernel(h_ref, y0_ref, y1_ref, y2_ref, y3_ref, w_ref, g_ref, b_ref, o_ref, *, alpha):
    rows = h_ref.shape[0]
    for r in range(0, rows, rows // 2):
        sl = slice(r, r + rows // 2)
        ycat = jnp.concatenate([y0_ref[sl, :], y1_ref[sl, :], y2_ref[sl, :], y3_ref[sl, :]], axis=1)
        y = _dot(ycat, w_ref[...])
        o_ref[sl, :] = _ln_rows(alpha * h_ref[sl, :] + y, g_ref[...], b_ref[...])


def _outproj(h2, ys, w_bf16, g, b, alpha, tm=512):
    m, d = h2.shape
    dm = w_bf16.shape[0]
    yspec = pl.BlockSpec((tm, D_BRANCH), lambda i: (i, 0))
    vmem = 2 * (2 * tm * d * 4 + tm * dm * 2 + dm * d * 2) + tm * d * 8 + (4 << 20)
    return pl.pallas_call(
        functools.partial(_outproj_kernel, alpha=alpha),
        grid=(m // tm,),
        in_specs=[pl.BlockSpec((tm, d), lambda i: (i, 0)), yspec, yspec, yspec, yspec,
                  pl.BlockSpec((dm, d), lambda i: (0, 0)),
                  pl.BlockSpec((1, d), lambda i: (0, 0)),
                  pl.BlockSpec((1, d), lambda i: (0, 0))],
        out_specs=pl.BlockSpec((tm, d), lambda i: (i, 0)),
        out_shape=jax.ShapeDtypeStruct((m, d), F32),
        compiler_params=_params(1, vmem),
        name="out_projection_layernorm",
    )(h2, *ys, w_bf16, g.reshape(1, d), b.reshape(1, d))


def _head_lane_masks():
    m = np.zeros((N_HEADS, D_BRANCH), np.float32)
    for h in range(N_HEADS):
        m[h, h * HEAD_DIM:(h + 1) * HEAD_DIM] = 1.0
    return m


def _head_block_diag(value=1.0):
    hm = _head_lane_masks()
    return (hm.T @ hm) * value


def _mixer_call(kernel, col_blocks, consts, scratch, n_batch, seq, name, rows=MIX_ROWS, vmem=40 << 20):
    steps = seq // rows
    in_specs = [pl.BlockSpec((rows, w), functools.partial(lambda b, t, c: (b * steps + t, c), c=c))
                for (_, w, c) in col_blocks]
    in_specs += [pl.BlockSpec(a.shape, functools.partial(lambda b, t, nd: (0,) * nd, nd=a.ndim))
                 for a in consts]
    return pl.pallas_call(
        kernel,
        grid=(n_batch, steps),
        in_specs=in_specs,
        out_specs=pl.BlockSpec((rows, D_BRANCH), lambda b, t: (b * steps + t, 0)),
        out_shape=jax.ShapeDtypeStruct((n_batch * seq, D_BRANCH), BF16),
        scratch_shapes=scratch,
        compiler_params=_params(2, vmem),
        name=name,
    )(*[a for (a, _, _) in col_blocks], *consts)


def _ssd_kernel(xbc_ref, z_ref, dt_ref, convw_ref, convb_ref, dtb_ref, aneg_ref, dskip_ref, nw_ref,
                tri_ref, negmask_ref, exph_ref, pmask_ref, smask_ref, shift_ref,
                o_ref, xp_ref, state_ref, u_ref, sb_ref, ea_ref, cm_ref, part_ref):
    C = SSD_CHUNK
    rows = xbc_ref.shape[0]
    n_chunks = rows // C
    halves = D_BRANCH // LANES
    pad = BF16_ROWS
    sls = [slice(c * C, (c + 1) * C) for c in range(n_chunks)]

    @pl.when(pl.program_id(1) == 0)
    def _():
        xp_ref[0:pad, :] = jnp.zeros((pad, SSM_XBC), BF16)
        state_ref[...] = jnp.zeros_like(state_ref)

    xp_ref[pad:pad + rows, :] = xbc_ref[...]

    convw = convw_ref[...]
    pmask = pmask_ref[...]
    exph = exph_ref[...]
    xs, bmb, lcat, dth, ach = [], [], [], [], []
    for c in range(n_chunks):
        r0 = c * C
        ext = xp_ref[r0:r0 + C + pad, :]
        shifted = _dot(shift_ref[...], ext)
        acc = convb_ref[...] + convw[SSM_CONV - 1:SSM_CONV, :] * ext[pad:, :].astype(F32)
        for j in range(SSM_CONV - 1):
            acc = acc + convw[j:j + 1, :] * shifted[j * C:(j + 1) * C, :]
        xbc = _silu(acc)
        xs.append(xbc[:, :D_BRANCH])
        bmb.append(xbc[:, D_BRANCH:D_BRANCH + LANES].astype(BF16))
        cm_ref[c] = xbc[:, D_BRANCH + LANES:].astype(BF16)
    for c in range(n_chunks):
        dt = _softplus(dt_ref[sls[c], :] + dtb_ref[...])
        a_cum = _sel_dot(tri_ref[...], dt * aneg_ref[...])
        a_cum_t = a_cum.T
        col = jnp.concatenate(
            [jnp.broadcast_to(a_cum[:, h:h + 1], (C, C)) for h in range(N_HEADS)], axis=1)
        row = jnp.concatenate([a_cum_t[h:h + 1, :] for h in range(N_HEADS)], axis=1)
        lcat.append(jnp.exp(col - row + negmask_ref[...]))
        dth.append(_dot_sel(dt, exph))
        ach.append(_dot_sel(a_cum, exph))
    heads_per_group = N_HEADS // SSM_GROUPS
    for c in range(n_chunks):
        sg = _dot_nt(cm_ref[c], _pair_stack(bmb[c], 0, pmask))
        s4 = jnp.concatenate([sg[:, (h // heads_per_group) * C:(h // heads_per_group + 1) * C]
                              for h in range(N_HEADS)], axis=1)
        p = (s4 * lcat[c]).astype(BF16)
        xdt = xs[c] * dth[c]
        xdtb = xdt.astype(BF16)
        y = jnp.concatenate(
            [_dot(p[:, g * 2 * C:(g + 1) * 2 * C], _pair_stack(xdtb, g, pmask)) for g in range(halves)], axis=1)
        part_ref[sls[c], :] = y + xs[c] * dskip_ref[...]
        alast = ach[c][C - 1:C, :]
        u_ref[c] = smask_ref[...] * _dot_tn(bmb[c], (xdt * jnp.exp(alast - ach[c])).astype(BF16))
        ea_ref[c] = jnp.exp(ach[c])

    xp_ref[0:pad, :] = xbc_ref[rows - pad:rows, :]

    st = state_ref[...]
    for c in range(n_chunks):
        sb_ref[c] = st.astype(BF16)
        st = st * ea_ref[c, C - 1:C, :] + u_ref[c]
    state_ref[...] = st

    for c in range(n_chunks):
        y = part_ref[sls[c], :] + _dot(cm_ref[c], sb_ref[c]) * ea_ref[c]
        g = y * _silu(z_ref[sls[c], :].astype(F32))
        outs = []
        for grp in range(SSM_GROUPS):
            gg = g[:, grp * LANES:(grp + 1) * LANES]
            ms = jnp.mean(gg * gg, axis=-1, keepdims=True)
            outs.append(gg * lax.rsqrt(ms + RMS_EPS))
        o_ref[sls[c], :] = (jnp.concatenate(outs, axis=1) * nw_ref[...]).astype(o_ref.dtype)


def _ssd_consts():
    C = SSD_CHUNK
    idx = np.arange(C)
    tri = (idx[:, None] >= idx[None, :]).astype(np.float32)
    negmask = np.tile(np.where(tri > 0, 0.0, NEG_BIG).astype(np.float32), (1, N_HEADS))
    exph = np.zeros((LANES, D_BRANCH), np.float32)
    exph[:N_HEADS] = _head_lane_masks()
    heads_per_group = N_HEADS // SSM_GROUPS
    smask = np.zeros((LANES, D_BRANCH), np.float32)
    for h in range(N_HEADS):
        g = h // heads_per_group
        smask[g * SSM_STATE:(g + 1) * SSM_STATE, h * HEAD_DIM:(h + 1) * HEAD_DIM] = 1.0
    shift = np.zeros(((SSM_CONV - 1) * C, C + BF16_ROWS), np.float32)
    for j in range(SSM_CONV - 1):
        shift[j * C + idx, BF16_ROWS - (SSM_CONV - 1) + j + idx] = 1.0
    return (jnp.asarray(tri, BF16), jnp.asarray(negmask), jnp.asarray(exph, BF16),
            _pair_masks(), jnp.asarray(smask), jnp.asarray(shift, BF16))


def _ssd(proj, aux, conv_w, conv_b, dt_bias, a_log, d_skip, norm_w, n_batch, seq):
    lane_pad = LANES - N_HEADS
    dtb = jnp.pad(dt_bias.astype(F32), (0, lane_pad)).reshape(1, LANES)
    aneg = jnp.pad(-jnp.exp(a_log.astype(F32)), (0, lane_pad)).reshape(1, LANES)
    dskip = jnp.repeat(d_skip.astype(F32), HEAD_DIM).reshape(1, D_BRANCH)
    consts = [conv_w.astype(F32), conv_b.reshape(1, SSM_XBC).astype(F32), dtb, aneg, dskip,
              norm_w.reshape(1, D_BRANCH).astype(F32), *_ssd_consts()]
    n_chunks = MIX_ROWS // SSD_CHUNK
    state_shape = (SSM_GROUPS * SSM_STATE, D_BRANCH)
    scratch = [pltpu.VMEM((MIX_ROWS + BF16_ROWS, SSM_XBC), BF16),
               pltpu.VMEM(state_shape, F32),
               pltpu.VMEM((n_chunks,) + state_shape, F32),
               pltpu.VMEM((n_chunks,) + state_shape, BF16),
               pltpu.VMEM((n_chunks, SSD_CHUNK, D_BRANCH), F32),
               pltpu.VMEM((n_chunks, SSD_CHUNK, LANES), BF16),
               pltpu.VMEM((MIX_ROWS, D_BRANCH), F32)]
    cols = [(proj, SSM_XBC, 0), (proj, D_BRANCH, COL_Z), (aux, LANES, AUX_COL_DT)]
    return _mixer_call(_ssd_kernel, cols, consts, scratch, n_batch, seq, "ssd_mixer")


def _seg_mean(x, avg):
    return _dot_sel(x, avg)


def _ret_kernel(q_ref, k_ref, v_ref, g_ref, dcat_ref, qdec_ref, kdec_ref, cg_ref, pmask_ref,
                bd_ref, avg_ref, o_ref, state_ref, u_ref, rb_ref, part_ref):
    C = RET_CHUNK
    rows = q_ref.shape[0]
    n_chunks = rows // C
    halves = D_BRANCH // LANES
    sls = [slice(c * C, (c + 1) * C) for c in range(n_chunks)]

    @pl.when(pl.program_id(1) == 0)
    def _():
        state_ref[...] = jnp.zeros_like(state_ref)

    pmask = pmask_ref[...]
    for c in range(n_chunks):
        qb, kb, vb = q_ref[sls[c], :], k_ref[sls[c], :], v_ref[sls[c], :]
        outs = []
        for g in range(halves):
            scores = _dot_nt(qb[:, g * LANES:(g + 1) * LANES], _pair_stack(kb, g, pmask))
            p = (scores * dcat_ref[:, g * 2 * C:(g + 1) * 2 * C]).astype(BF16)
            outs.append(_dot(p, _pair_stack(vb, g, pmask)))
        part_ref[sls[c], :] = jnp.concatenate(outs, axis=1)
        u_ref[c] = bd_ref[...] * _dot_tn((kb.astype(F32) * kdec_ref[...]).astype(BF16), vb)

    r = state_ref[...]
    for c in range(n_chunks):
        rb_ref[c] = r.astype(BF16)
        r = r * cg_ref[...] + u_ref[c]
    state_ref[...] = r

    for c in range(n_chunks):
        y = part_ref[sls[c], :] + _dot((q_ref[sls[c], :].astype(F32) * qdec_ref[...]).astype(BF16), rb_ref[c])
        mu = _seg_mean(y, avg_ref[...])
        yc = y - mu
        var = _seg_mean(yc * yc, avg_ref[...])
        o_ref[sls[c], :] = (yc * lax.rsqrt(var + LN_EPS) * _silu(g_ref[sls[c], :].astype(F32))).astype(o_ref.dtype)


def _ret_consts():
    C = RET_CHUNK
    log_g = jnp.log(1.0 - 2.0 ** (-5.0 - jnp.arange(N_HEADS, dtype=F32)))
    pos = jnp.arange(C, dtype=F32)
    dist = pos[:, None] - pos[None, :]
    intra = jnp.where(dist >= 0, jnp.exp(log_g[:, None, None] * jnp.maximum(dist, 0.0)), 0.0)
    scale = HEAD_DIM ** -0.5
    dcat = jnp.concatenate([intra[h] for h in range(N_HEADS)], axis=1) * scale
    k_decay = jnp.exp(log_g[:, None] * (C - 1.0 - pos)[None, :]) * scale
    q_decay = jnp.exp(log_g[:, None] * (pos + 1.0)[None, :])
    kdec = jnp.repeat(k_decay.T, HEAD_DIM, axis=1)
    qdec = jnp.repeat(q_decay.T, HEAD_DIM, axis=1)
    cg = jnp.repeat(jnp.exp(log_g * C), HEAD_DIM).reshape(1, D_BRANCH)
    return (dcat, qdec, kdec, cg, _pair_masks(),
            jnp.asarray(_head_block_diag()), jnp.asarray(_head_block_diag(1.0 / HEAD_DIM), BF16))


def _retention(proj, n_batch, seq):
    n_chunks = MIX_ROWS // RET_CHUNK
    scratch = [pltpu.VMEM((D_BRANCH, D_BRANCH), F32),
               pltpu.VMEM((n_chunks, D_BRANCH, D_BRANCH), F32),
               pltpu.VMEM((n_chunks, D_BRANCH, D_BRANCH), BF16),
               pltpu.VMEM((MIX_ROWS, D_BRANCH), F32)]
    cols = [(proj, D_BRANCH, COL_RET + i) for i in range(4)]
    return _mixer_call(_ret_kernel, cols, list(_ret_consts()), scratch, n_batch, seq, "retention_mixer")


def _hgrn_levels():
    return int(math.log2(HGRN_CHUNK))


def _hgrn_kernel(q_ref, i_ref, g_ref, f_ref, lb_ref, nw_ref, mstack_ref, sign_ref, lmask_ref, pmask_ref,
                 bd_ref, avg_ref, o_ref, state_ref, e_ref, u_ref, stb_ref, part_ref):
    C = HGRN_CHUNK
    nl = _hgrn_levels()
    rows = q_ref.shape[0]
    halves = D_BRANCH // LANES

    @pl.when(pl.program_id(1) == 0)
    def _():
        state_ref[...] = jnp.zeros_like(state_ref)

    pmask = pmask_ref[...]
    lb = lb_ref[...]

    def head_scores(qb, kb):
        return jnp.concatenate(
            [_dot_nt(qb[:, g * LANES:(g + 1) * LANES], _pair_stack(kb, g, pmask)) for g in range(halves)], axis=1)

    n_chunks = rows // C
    sls = [slice(c * C, (c + 1) * C) for c in range(n_chunks)]

    k_rows = slice((nl + 2) * C, (nl + 3) * C)
    nm = HGRN_MATMUL_LEVELS
    for c in range(n_chunks):
        f = lb + (1.0 - lb) * _sigmoid(f_ref[sls[c], :])
        x = _sel_dot(mstack_ref[...], jnp.log(f))
        b = x[nm * C:(nm + 1) * C, :]
        e_ref[c, 0:nm * C, :] = jnp.exp(x[0:nm * C, :])
        e_ref[c, nl * C:(nl + 1) * C, :] = jnp.exp(b)
        for j in range(nm, nl):
            s = 1 << j
            bnd = jnp.concatenate([jnp.broadcast_to(b[r + s - 1:r + s, :], (2 * s, D_BRANCH))
                                   for r in range(0, C, 2 * s)], axis=0)
            e_ref[c, j * C:(j + 1) * C, :] = jnp.exp((b - bnd) * sign_ref[j - nm])
        e_ref[c, (nl + 1) * C:(nl + 2) * C, :] = jnp.exp(jnp.broadcast_to(b[C - 1:C, :], (C, D_BRANCH)) - b)
        e_ref[c, k_rows, :] = 1.0 - f

    group = 2
    for c0 in range(0, n_chunks, group):
        cs = range(c0, min(c0 + group, n_chunks))
        att = {c: lmask_ref[nl] * head_scores(q_ref[sls[c], :], e_ref[c, k_rows, :].astype(BF16))
               for c in cs}
        for j in range(nl):
            for c in cs:
                ej = e_ref[c, j * C:(j + 1) * C, :]
                att[c] = att[c] + lmask_ref[j] * head_scores((q_ref[sls[c], :].astype(F32) * ej).astype(BF16),
                                                             (e_ref[c, k_rows, :] * ej).astype(BF16))
        for c in cs:
            attb = att[c].astype(BF16)
            vb = i_ref[sls[c], :]
            part_ref[sls[c], :] = jnp.concatenate(
                [_dot(attb[:, g * 2 * C:(g + 1) * 2 * C], _pair_stack(vb, g, pmask)) for g in range(halves)], axis=1)

    for c in range(n_chunks):
        er = e_ref[c, (nl + 1) * C:(nl + 2) * C, :]
        u_ref[c] = bd_ref[...] * _dot_tn(i_ref[sls[c], :], (e_ref[c, k_rows, :] * er).astype(BF16))

    st = state_ref[...]
    for c in range(n_chunks):
        stb_ref[c] = st.astype(BF16)
        st = st * e_ref[c, (nl + 1) * C - 1:(nl + 1) * C, :] + u_ref[c]
    state_ref[...] = st

    for c in range(n_chunks):
        eb = e_ref[c, nl * C:(nl + 1) * C, :]
        o = part_ref[sls[c], :] + _dot_nt((q_ref[sls[c], :].astype(F32) * eb).astype(BF16), stb_ref[c])
        ms = _seg_mean(o * o, avg_ref[...])
        o_ref[sls[c], :] = (o * lax.rsqrt(ms + RMS_EPS) * nw_ref[...]
                            * _silu(g_ref[sls[c], :].astype(F32))).astype(o_ref.dtype)


def _hgrn_consts():
    C = HGRN_CHUNK
    nl = _hgrn_levels()
    t = np.arange(C)[:, None]
    u = np.arange(C)[None, :]
    blocks, masks, signs = [], [], []
    for j in range(nl):
        s = 1 << j
        bnd = (t // (2 * s)) * (2 * s) + s - 1
        upper = (t % (2 * s)) >= s
        if j < HGRN_MATMUL_LEVELS:
            m = np.where(upper, (u > bnd) & (u <= t), (u > t) & (u <= bnd))
            blocks.append(m.astype(np.float32))
        else:
            signs.append(np.broadcast_to(np.where(upper, 1.0, -1.0), (C, D_BRANCH)).astype(np.float32))
        pair = (t // (2 * s)) == (u // (2 * s))
        masks.append((upper & pair & ((u % (2 * s)) < s)).astype(np.float32))
    blocks.append((u <= t).astype(np.float32))
    masks.append((u == t).astype(np.float32))
    mstack = np.concatenate(blocks, axis=0)
    lmask = np.stack([np.tile(m, (1, N_HEADS)) for m in masks])
    return (jnp.asarray(mstack, BF16), jnp.asarray(np.stack(signs)), jnp.asarray(lmask), _pair_masks(),
            jnp.asarray(_head_block_diag()), jnp.asarray(_head_block_diag(1.0 / HEAD_DIM), BF16))


def _hgrn2(proj, aux, lb, norm_w, n_batch, seq):
    consts = [lb.reshape(1, D_BRANCH).astype(F32), norm_w.reshape(1, D_BRANCH).astype(F32), *_hgrn_consts()]
    scratch = [pltpu.VMEM((D_BRANCH, D_BRANCH), F32),
               pltpu.VMEM((MIX_ROWS // HGRN_CHUNK, (_hgrn_levels() + 3) * HGRN_CHUNK, D_BRANCH), F32),
               pltpu.VMEM((MIX_ROWS // HGRN_CHUNK, D_BRANCH, D_BRANCH), F32),
               pltpu.VMEM((MIX_ROWS // HGRN_CHUNK, D_BRANCH, D_BRANCH), BF16),
               pltpu.VMEM((MIX_ROWS, D_BRANCH), F32)]
    cols = [(proj, D_BRANCH, COL_HGRN + i) for i in range(3)] + [(aux, D_BRANCH, AUX_COL_F)]
    return _mixer_call(_hgrn_kernel, cols, consts, scratch, n_batch, seq, "hgrn2_mixer")


MOBA_VROWS = 80
MOBA_BODY_POSITIONS = 4


def _moba_kernel(q_ref, k_ref, v_ref, g_ref, ksel_ref, kpos_ref, kblk_ref, qaug_ref, o_ref,
                 kaug_ref, vt_ref, kmean_ref, sel_ref, qt_ref, acc_ref, m_ref, mcur_ref, alpha_ref,
                 s_ref, p_ref):
    S = MOBA_BLOCK
    i = pl.program_id(1)
    nbp = kmean_ref.shape[0]

    @pl.when(i == 0)
    def _():
        kmean_ref[...] = jnp.zeros_like(kmean_ref)

    kb = k_ref[...]
    kmean_ref[pl.ds(i, 1), :] = jnp.mean(kb.astype(F32), axis=0, keepdims=True)
    v_t = v_ref[...].astype(F32).T
    q_t = (q_ref[...].astype(F32) * (HEAD_DIM ** -0.5)).T
    ones_pad = (lax.broadcasted_iota(jnp.int32, (MOBA_VROWS - HEAD_DIM, S), 0) == 0).astype(F32)
    blk = lax.broadcasted_iota(jnp.int32, (nbp, S), 0)
    blk_f = blk.astype(F32)
    past = blk < i
    kmean = kmean_ref[...]
    own_rows = (lax.broadcasted_iota(jnp.int32, (SUBLANES, S), 0) == 0).astype(F32)
    kextra = kpos_ref[...] + i.astype(F32) * kblk_ref[...]
    for h in range(N_HEADS):
        hs = slice(h * HEAD_DIM, (h + 1) * HEAD_DIM)
        kaug_ref[i, h] = (_dot(kb, ksel_ref[h]) + kextra).astype(BF16)
        vt_ref[i, h] = jnp.concatenate([v_t[hs, :], ones_pad], axis=0).astype(BF16)
        qh_t = q_t[hs, :]
        qt_ref[h] = jnp.concatenate([qh_t * LOG2E, qaug_ref[h]], axis=0).astype(BF16)
        km_hi, km_mid, _ = _split3(kmean[:, hs])
        q_hi, q_mid, _ = _split3(qh_t)
        gate = _dot(km_hi, q_hi) + _dot(km_hi, q_mid) + _dot(km_mid, q_hi)
        gate = jnp.where(past, gate, -jnp.inf)
        sel = jnp.zeros((nbp, S), F32)
        for _ in range(MOBA_TOPK):
            best = jnp.max(gate, axis=0, keepdims=True)
            first = jnp.min(jnp.where(gate == best, blk_f, float(nbp)), axis=0, keepdims=True)
            hit = blk_f == first
            sel = jnp.where(hit, 1.0, sel)
            gate = jnp.where(hit, -jnp.inf, gate)
        sel_ref[h, 0:nbp, :] = jnp.where(past, sel, 0.0)
        sel_ref[h, nbp:nbp + SUBLANES, :] = own_rows

    def block_of(t):
        return jnp.where(t == 0, i, jnp.minimum(t - 1, i))

    def sel_row_of(t):
        return jnp.where(t == 0, nbp, jnp.where(t <= i, t - 1, nbp + 1))

    def track_max(h, slot, s_t, chosen):
        m_old = m_ref[h]
        m_new = jnp.where(chosen, jnp.maximum(m_old, jnp.max(s_t, axis=0, keepdims=True)), m_old)
        m_ref[h] = m_new
        mcur_ref[slot, h] = m_new
        alpha_ref[slot, h] = jnp.exp2(m_old - m_new)

    def score_phase(t, slot):
        n, row = block_of(t), sel_row_of(t)
        for h in range(N_HEADS):
            s_t = _dot(kaug_ref[n, h], qt_ref[h])
            s_ref[slot, h] = s_t
            track_max(h, slot, s_t, sel_ref[h, pl.ds(row, 1), :] > 0.0)

    def value_phase(t, slot):
        n, row = block_of(t), sel_row_of(t)
        for h in range(N_HEADS):
            p_ref[h] = jnp.exp2(s_ref[slot, h] - mcur_ref[slot, h]).astype(BF16)
        for h in range(N_HEADS):
            chosen = sel_ref[h, pl.ds(row, 1), :] > 0.0
            upd = _dot(vt_ref[n, h], p_ref[h])
            acc_ref[h] = alpha_ref[slot, h] * acc_ref[h] + jnp.where(chosen, upd, 0.0)

    key_pos = lax.broadcasted_iota(jnp.int32, (S, S), 0)
    qry_pos = lax.broadcasted_iota(jnp.int32, (S, S), 1)
    for h in range(N_HEADS):
        acc_ref[h] = jnp.zeros((MOBA_VROWS, S), F32)
        m_ref[h] = jnp.full((1, S), NEG_BIG, F32)
        s_t = jnp.where(key_pos <= qry_pos, _dot(kaug_ref[i, h], qt_ref[h]), NEG_BIG)
        s_ref[0, h] = s_t
        track_max(h, 0, s_t, True)

    def visit_positions(j, carry):
        t0 = MOBA_BODY_POSITIONS * j
        for r in range(0, MOBA_BODY_POSITIONS, 2):
            score_phase(t0 + r + 1, 1)
            value_phase(t0 + r, 0)
            score_phase(t0 + r + 2, 0)
            value_phase(t0 + r + 1, 1)
        return carry

    lax.fori_loop(0, (i + MOBA_BODY_POSITIONS) // MOBA_BODY_POSITIONS, visit_positions, 0)

    outs = []
    for h in range(N_HEADS):
        acc = acc_ref[h]
        outs.append(acc[:HEAD_DIM, :] / acc[HEAD_DIM:HEAD_DIM + 1, :])
    o_ref[...] = (jnp.concatenate(outs, axis=0).T * _silu(g_ref[...].astype(F32))).astype(o_ref.dtype)


def _moba(proj, n_batch, seq):
    S = MOBA_BLOCK
    assert seq % S == 0
    nb = seq // S
    nbp = -(-nb // SUBLANES) * SUBLANES
    ksel = np.zeros((N_HEADS, D_BRANCH, LANES), np.float32)
    for h in range(N_HEADS):
        ksel[h, h * HEAD_DIM + np.arange(HEAD_DIM), np.arange(HEAD_DIM)] = 1.0
    n_split = 3
    kpos = np.zeros((S, LANES), np.float32)
    kblk = np.zeros((S, LANES), np.float32)
    kpos[:, HEAD_DIM:HEAD_DIM + n_split] = np.arange(S, dtype=np.float32)[:, None]
    kblk[:, HEAD_DIM + n_split:HEAD_DIM + 2 * n_split] = 1.0
    slopes = 2.0 ** (-8.0 * np.arange(1, N_HEADS + 1, dtype=np.float32) / N_HEADS)
    qaug = np.zeros((N_HEADS, LANES - HEAD_DIM, S), np.float32)
    for h in range(N_HEADS):
        for base, coef in ((0, slopes[h] * LOG2E), (n_split, slopes[h] * LOG2E * S)):
            rem = np.float32(coef)
            for r in range(n_split):
                piece = np.float32(np.asarray(rem).astype(BF16))
                qaug[h, base + r, :] = piece
                rem = np.float32(rem - piece)
    consts = [jnp.asarray(ksel, BF16), jnp.asarray(kpos), jnp.asarray(kblk), jnp.asarray(qaug)]
    scratch = [pltpu.VMEM((nb, N_HEADS, S, LANES), BF16),
               pltpu.VMEM((nb, N_HEADS, MOBA_VROWS, S), BF16),
               pltpu.VMEM((nbp, D_BRANCH), F32),
               pltpu.VMEM((N_HEADS, nbp + SUBLANES, S), F32),
               pltpu.VMEM((N_HEADS, LANES, S), BF16),
               pltpu.VMEM((N_HEADS, MOBA_VROWS, S), F32),
               pltpu.VMEM((N_HEADS, 1, S), F32),
               pltpu.VMEM((2, N_HEADS, 1, S), F32),
               pltpu.VMEM((2, N_HEADS, 1, S), F32),
               pltpu.VMEM((2, N_HEADS, S, S), F32),
               pltpu.VMEM((N_HEADS, S, S), BF16)]
    vmem = nb * N_HEADS * S * (LANES + MOBA_VROWS) * 2 + (24 << 20)
    cols = [(proj, D_BRANCH, COL_MOBA + i) for i in range(4)]
    return _mixer_call(_moba_kernel, cols, consts, scratch, n_batch, seq, "moba_mixer", rows=S, vmem=vmem)


def _pack_w_in(w_in):
    depth, d, n_in = w_in.shape
    wt = jnp.transpose(w_in, (2, 0, 1)).reshape(n_in, depth, d // LANES, LANES)
    o = np.cumsum((0, D_BRANCH, SSM_XBC, N_HEADS) + (D_BRANCH,) * 12)
    z, xbc, dt = wt[o[0]:o[1]], wt[o[1]:o[2]], wt[o[2]:o[3]]
    hq, hf, hig, ret_moba = wt[o[3]:o[4]], wt[o[4]:o[5]], wt[o[5]:o[7]], wt[o[7]:]
    pad = jnp.zeros((N_PROJ - n_in,) + wt.shape[1:], wt.dtype)
    packed = jnp.concatenate([xbc, z, hq, hig, ret_moba, hf, dt, pad], axis=0)
    return jnp.transpose(packed, (1, 0, 2, 3)).reshape(depth, N_PROJ * (d // LANES), LANES)


def kernel(x, emb_ln_g, emb_ln_b, w_in, ssd_conv_w, ssd_conv_b, ssd_dt_bias, ssd_a_log, ssd_d, ssd_norm_w,
           hgrn_lb_logits, hgrn_norm_w, w_out, ln_g, ln_b):
    n_batch, seq, d_model = x.shape
    depth = w_in.shape[0]
    alpha = (2.0 * depth) ** 0.25
    assert seq % MIX_ROWS == 0 and seq % MOBA_BLOCK == 0

    lbs = jnp.cumsum(jax.nn.softmax(hgrn_lb_logits.astype(F32), axis=0), axis=0)
    lbs = lbs - lbs[0]

    wt_all = _pack_w_in(w_in.astype(F32))
    h = x.reshape(n_batch * seq, d_model)
    for l in range(depth):
        if l == 0:
            proj, aux, h = _inproj(h, wt_all, l, norm=(emb_ln_g, emb_ln_b))
        else:
            proj, aux = _inproj(h, wt_all, l)
        ys = [_ssd(proj, aux, ssd_conv_w[l], ssd_conv_b[l], ssd_dt_bias[l], ssd_a_log[l], ssd_d[l],
                   ssd_norm_w[l], n_batch, seq),
              _hgrn2(proj, aux, lbs[l], hgrn_norm_w[l], n_batch, seq),
              _retention(proj, n_batch, seq),
              _moba(proj, n_batch, seq)]
        h = _outproj(h, ys, w_out[l].astype(BF16), ln_g[l], ln_b[l], alpha)
    return h.reshape(n_batch, seq, d_model)
```

```python
import functools
import math

import numpy as np
import jax
import jax.numpy as jnp
from jax import lax
from jax.experimental import pallas as pl
from jax.experimental.pallas import tpu as pltpu

F32 = jnp.float32
BF16 = jnp.bfloat16

D_BRANCH = 256
HEAD_DIM = 64
N_HEADS = 4
SSM_STATE = 64
SSM_GROUPS = 2
SSM_CONV = 4
SSM_XBC = D_BRANCH + 2 * SSM_GROUPS * SSM_STATE
MOBA_BLOCK = 256
MOBA_TOPK = 3
LN_EPS = 1e-5
RMS_EPS = 1e-6
NEG_BIG = -1e30
LOG2E = 1.4426950408889634

LANES = 128
SUBLANES = 8
BF16_ROWS = 16
VMEM_BYTES_V7X = 64 * 1024 * 1024

SSD_CHUNK = 128
RET_CHUNK = 256
HGRN_CHUNK = 128
HGRN_MATMUL_LEVELS = 3
MIX_ROWS = 2048
HGRN_ROWS = 1024

N_PROJ = 4096
N_AUX = 512
COL_Z = SSM_XBC // D_BRANCH
COL_HGRN = COL_Z + 1
COL_RET = COL_HGRN + 3
COL_MOBA = COL_RET + 4
AUX_COL_F = 0
AUX_COL_DT = D_BRANCH // LANES


def _dot(a, b):
    return jnp.dot(a, b, preferred_element_type=F32)


def _dot_nt(a, b):
    return lax.dot_general(a, b, (((1,), (1,)), ((), ())), preferred_element_type=F32)


def _dot_tn(a, b):
    return lax.dot_general(a, b, (((0,), (0,)), ((), ())), preferred_element_type=F32)


def _split3(x):
    hi = x.astype(BF16)
    r1 = x - hi.astype(F32)
    mid = r1.astype(BF16)
    lo = (r1 - mid.astype(F32)).astype(BF16)
    return hi, mid, lo


def _split2(x):
    hi = x.astype(BF16)
    return hi, (x - hi.astype(F32)).astype(BF16)


def _sel_dot(sel, x):
    hi, lo = _split2(x)
    return _dot(sel, hi) + _dot(sel, lo)


def _dot_sel(x, sel):
    hi, lo = _split2(x)
    return _dot(hi, sel) + _dot(lo, sel)


def _sigmoid(x):
    return 1.0 / (1.0 + jnp.exp(-x))


def _silu(x):
    return x * _sigmoid(x)


def _softplus(x):
    return jnp.maximum(x, 0.0) + jnp.log(1.0 + jnp.exp(-jnp.abs(x)))


def _pair_masks():
    pm = np.zeros((LANES // HEAD_DIM, LANES), np.float32)
    for p in range(LANES // HEAD_DIM):
        pm[p, p * HEAD_DIM:(p + 1) * HEAD_DIM] = 1.0
    return jnp.asarray(pm, BF16)


def _pair_stack(xb, g, pmask):
    xg = xb[:, g * LANES:(g + 1) * LANES]
    return jnp.concatenate([xg * pmask[p:p + 1, :] for p in range(pmask.shape[0])], axis=0)


def _row_stack(x, masks):
    return jnp.concatenate([x * masks[h:h + 1, :] for h in range(masks.shape[0])], axis=0)


def _params(n_axes, vmem_bytes):
    return pltpu.CompilerParams(dimension_semantics=("arbitrary",) * n_axes,
                                vmem_limit_bytes=int(vmem_bytes))


def _ln_rows(x, g, b):
    mu = jnp.mean(x, axis=-1, keepdims=True)
    xc = x - mu
    var = jnp.mean(xc * xc, axis=-1, keepdims=True)
    return xc * lax.rsqrt(var + LN_EPS) * g + b


def _inproj_kernel(*refs, tn, norm_first):
    if norm_first:
        x_ref, wt_ref, g_ref, b_ref, o_ref, aux_ref, h_ref, xb_ref, wtb_ref = refs
    else:
        x_ref, wt_ref, o_ref, aux_ref, xb_ref, wtb_ref = refs
    i, j = pl.program_id(0), pl.program_id(1)
    k_tiles = x_ref.shape[1] // LANES
    cols = pl.ds(pl.multiple_of(j * tn, tn), tn)

    @pl.when(j == 0)
    def _():
        if norm_first:
            h = _ln_rows(x_ref[...], g_ref[...], b_ref[...])
            h_ref[...] = h
            xb_ref[...] = h.astype(BF16)
        else:
            xb_ref[...] = x_ref[...].astype(BF16)

    @pl.when(i == 0)
    def _():
        wtb_ref[cols, :] = jnp.concatenate(
            [wt_ref[pl.ds(kt, tn, stride=k_tiles), :] for kt in range(k_tiles)], axis=1).astype(BF16)

    acc = _dot_nt(xb_ref[...], wtb_ref[cols, :])
    o_ref[...] = acc.astype(BF16)

    @pl.when(j == pl.num_programs(1) - 1)
    def _():
        aux_ref[...] = acc[:, tn - aux_ref.shape[1]:]


def _inproj(x2, wt_all, layer, norm=None, tm=1024, tn=1024):
    m, d = x2.shape
    rows = wt_all.shape[1]
    n = rows * LANES // d
    n_col_tiles = n // tn
    tile_rows = tn * d // LANES
    assert N_AUX <= tn
    vmem = (2 * tm * d * 4 + 2 * tile_rows * LANES * 4 + 2 * tm * tn * 2 + 2 * tm * N_AUX * 4 + tm * tn * 4
            + tm * d * 2 + n * d * 2 + (6 << 20))
    row_spec = pl.BlockSpec((tm, d), lambda i, j: (i, 0))
    vec_spec = pl.BlockSpec((1, d), lambda i, j: (0, 0))
    in_specs = [row_spec,
                pl.BlockSpec((None, tile_rows, LANES),
                             lambda i, j: (layer, jnp.where(i == 0, j, n_col_tiles - 1), 0))]
    out_specs = [pl.BlockSpec((tm, tn), lambda i, j: (i, j)),
                 pl.BlockSpec((tm, N_AUX), lambda i, j: (i, 0))]
    out_shape = [jax.ShapeDtypeStruct((m, n), BF16), jax.ShapeDtypeStruct((m, N_AUX), F32)]
    operands = [x2, wt_all]
    if norm is not None:
        in_specs += [vec_spec, vec_spec]
        out_specs.append(row_spec)
        out_shape.append(jax.ShapeDtypeStruct((m, d), F32))
        operands += [norm[0].reshape(1, d), norm[1].reshape(1, d)]
        vmem += 2 * tm * d * 4
    return pl.pallas_call(
        functools.partial(_inproj_kernel, tn=tn, norm_first=norm is not None),
        grid=(m // tm, n_col_tiles),
        in_specs=in_specs,
        out_specs=out_specs,
        out_shape=out_shape,
        scratch_shapes=[pltpu.VMEM((tm, d), BF16), pltpu.VMEM((n, d), BF16)],
        compiler_params=_params(2, vmem),
        name="in_projection",
    )(*operands)


def _outproj_kernel(h_ref, y0_ref, y1_ref, y2_ref, y3_ref, w_ref, g_ref, b_ref, o_ref, *, alpha):
    rows = h_ref.shape[0]
    for r in range(0, rows, rows // 2):
        sl = slice(r, r + rows // 2)
        ycat = jnp.concatenate([y0_ref[sl, :], y1_ref[sl, :], y2_ref[sl, :], y3_ref[sl, :]], axis=1)
        y = _dot(ycat, w_ref[...])
        o_ref[sl, :] = _ln_rows(alpha * h_ref[sl, :] + y, g_ref[...], b_ref[...])


def _outproj(h2, ys, w_bf16, g, b, alpha, tm=1024):
    m, d = h2.shape
    dm = w_bf16.shape[0]
    yspec = pl.BlockSpec((tm, D_BRANCH), lambda i: (i, 0))
    vmem = 2 * (2 * tm * d * 4 + tm * dm * 2 + dm * d * 2) + tm * d * 8 + (4 << 20)
    return pl.pallas_call(
        functools.partial(_outproj_kernel, alpha=alpha),
        grid=(m // tm,),
        in_specs=[pl.BlockSpec((tm, d), lambda i: (i, 0)), yspec, yspec, yspec, yspec,
                  pl.BlockSpec((dm, d), lambda i: (0, 0)),
                  pl.BlockSpec((1, d), lambda i: (0, 0)),
                  pl.BlockSpec((1, d), lambda i: (0, 0))],
        out_specs=pl.BlockSpec((tm, d), lambda i: (i, 0)),
        out_shape=jax.ShapeDtypeStruct((m, d), F32),
        compiler_params=_params(1, vmem),
        name="out_projection_layernorm",
    )(h2, *ys, w_bf16, g.reshape(1, d), b.reshape(1, d))


def _head_lane_masks():
    m = np.zeros((N_HEADS, D_BRANCH), np.float32)
    for h in range(N_HEADS):
        m[h, h * HEAD_DIM:(h + 1) * HEAD_DIM] = 1.0
    return m


def _head_block_diag(value=1.0):
    hm = _head_lane_masks()
    return (hm.T @ hm) * value


def _mixer_call(kernel, col_blocks, consts, scratch, n_batch, seq, name, rows=MIX_ROWS, vmem=40 << 20):
    steps = seq // rows
    in_specs = [pl.BlockSpec((rows, w), functools.partial(lambda b, t, c: (b * steps + t, c), c=c))
                for (_, w, c) in col_blocks]
    in_specs += [pl.BlockSpec(a.shape, functools.partial(lambda b, t, nd: (0,) * nd, nd=a.ndim))
                 for a in consts]
    return pl.pallas_call(
        kernel,
        grid=(n_batch, steps),
        in_specs=in_specs,
        out_specs=pl.BlockSpec((rows, D_BRANCH), lambda b, t: (b * steps + t, 0)),
        out_shape=jax.ShapeDtypeStruct((n_batch * seq, D_BRANCH), BF16),
        scratch_shapes=scratch,
        compiler_params=_params(2, vmem),
        name=name,
    )(*[a for (a, _, _) in col_blocks], *consts)


def _ssd_kernel(xbc_ref, z_ref, dt_ref, convw_ref, convb_ref, dtb_ref, aneg_ref, dskip_ref, nw_ref,
                tri_ref, negmask_ref, exph_ref, pmask_ref, smask_ref, shift_ref,
                o_ref, xp_ref, state_ref, u_ref, sb_ref, ea_ref, cm_ref, part_ref):
    C = SSD_CHUNK
    rows = xbc_ref.shape[0]
    n_chunks = rows // C
    halves = D_BRANCH // LANES
    pad = BF16_ROWS
    sls = [slice(c * C, (c + 1) * C) for c in range(n_chunks)]

    @pl.when(pl.program_id(1) == 0)
    def _():
        xp_ref[0:pad, :] = jnp.zeros((pad, SSM_XBC), BF16)
        state_ref[...] = jnp.zeros_like(state_ref)

    xp_ref[pad:pad + rows, :] = xbc_ref[...]

    convw = convw_ref[...]
    pmask = pmask_ref[...]
    exph = exph_ref[...]
    xs, bmb, lcat, dth, ach = [], [], [], [], []
    for c in range(n_chunks):
        r0 = c * C
        ext = xp_ref[r0:r0 + C + pad, :]
        shifted = _dot(shift_ref[...], ext)
        acc = convb_ref[...] + convw[SSM_CONV - 1:SSM_CONV, :] * ext[pad:, :].astype(F32)
        for j in range(SSM_CONV - 1):
            acc = acc + convw[j:j + 1, :] * shifted[j * C:(j + 1) * C, :]
        xbc = _silu(acc)
        xs.append(xbc[:, :D_BRANCH])
        bmb.append(xbc[:, D_BRANCH:D_BRANCH + LANES].astype(BF16))
        cm_ref[c] = xbc[:, D_BRANCH + LANES:].astype(BF16)
    for c in range(n_chunks):
        dt = _softplus(dt_ref[sls[c], :] + dtb_ref[...])
        a_cum = _sel_dot(tri_ref[...], dt * aneg_ref[...])
        a_cum_t = a_cum.T
        col = jnp.concatenate(
            [jnp.broadcast_to(a_cum[:, h:h + 1], (C, C)) for h in range(N_HEADS)], axis=1)
        row = jnp.concatenate([a_cum_t[h:h + 1, :] for h in range(N_HEADS)], axis=1)
        lcat.append(jnp.exp(col - row + negmask_ref[...]))
        dth.append(_dot_sel(dt, exph))
        ach.append(_dot_sel(a_cum, exph))
    heads_per_group = N_HEADS // SSM_GROUPS
    for c in range(n_chunks):
        sg = _dot_nt(cm_ref[c], _pair_stack(bmb[c], 0, pmask))
        s4 = jnp.concatenate([sg[:, (h // heads_per_group) * C:(h // heads_per_group + 1) * C]
                              for h in range(N_HEADS)], axis=1)
        p = (s4 * lcat[c]).astype(BF16)
        xdt = xs[c] * dth[c]
        xdtb = xdt.astype(BF16)
        y = jnp.concatenate(
            [_dot(p[:, g * 2 * C:(g + 1) * 2 * C], _pair_stack(xdtb, g, pmask)) for g in range(halves)], axis=1)
        part_ref[sls[c], :] = y + xs[c] * dskip_ref[...]
        alast = ach[c][C - 1:C, :]
        u_ref[c] = smask_ref[...] * _dot_tn(bmb[c], (xdt * jnp.exp(alast - ach[c])).astype(BF16))
        ea_ref[c] = jnp.exp(ach[c])

    xp_ref[0:pad, :] = xbc_ref[rows - pad:rows, :]

    st = state_ref[...]
    for c in range(n_chunks):
        sb_ref[c] = st.astype(BF16)
        st = st * ea_ref[c, C - 1:C, :] + u_ref[c]
    state_ref[...] = st

    for c in range(n_chunks):
        y = part_ref[sls[c], :] + _dot(cm_ref[c], sb_ref[c]) * ea_ref[c]
        g = y * _silu(z_ref[sls[c], :].astype(F32))
        outs = []
        for grp in range(SSM_GROUPS):
            gg = g[:, grp * LANES:(grp + 1) * LANES]
            ms = jnp.mean(gg * gg, axis=-1, keepdims=True)
            outs.append(gg * lax.rsqrt(ms + RMS_EPS))
        o_ref[sls[c], :] = (jnp.concatenate(outs, axis=1) * nw_ref[...]).astype(o_ref.dtype)


def _ssd_consts():
    C = SSD_CHUNK
    idx = np.arange(C)
    tri = (idx[:, None] >= idx[None, :]).astype(np.float32)
    negmask = np.tile(np.where(tri > 0, 0.0, NEG_BIG).astype(np.float32), (1, N_HEADS))
    exph = np.zeros((LANES, D_BRANCH), np.float32)
    exph[:N_HEADS] = _head_lane_masks()
    heads_per_group = N_HEADS // SSM_GROUPS
    smask = np.zeros((LANES, D_BRANCH), np.float32)
    for h in range(N_HEADS):
        g = h // heads_per_group
        smask[g * SSM_STATE:(g + 1) * SSM_STATE, h * HEAD_DIM:(h + 1) * HEAD_DIM] = 1.0
    shift = np.zeros(((SSM_CONV - 1) * C, C + BF16_ROWS), np.float32)
    for j in range(SSM_CONV - 1):
        shift[j * C + idx, BF16_ROWS - (SSM_CONV - 1) + j + idx] = 1.0
    return (jnp.asarray(tri, BF16), jnp.asarray(negmask), jnp.asarray(exph, BF16),
            _pair_masks(), jnp.asarray(smask), jnp.asarray(shift, BF16))


def _ssd(proj, aux, conv_w, conv_b, dt_bias, a_log, d_skip, norm_w, n_batch, seq):
    lane_pad = LANES - N_HEADS
    dtb = jnp.pad(dt_bias.astype(F32), (0, lane_pad)).reshape(1, LANES)
    aneg = jnp.pad(-jnp.exp(a_log.astype(F32)), (0, lane_pad)).reshape(1, LANES)
    dskip = jnp.repeat(d_skip.astype(F32), HEAD_DIM).reshape(1, D_BRANCH)
    consts = [conv_w.astype(F32), conv_b.reshape(1, SSM_XBC).astype(F32), dtb, aneg, dskip,
              norm_w.reshape(1, D_BRANCH).astype(F32), *_ssd_consts()]
    n_chunks = MIX_ROWS // SSD_CHUNK
    state_shape = (SSM_GROUPS * SSM_STATE, D_BRANCH)
    scratch = [pltpu.VMEM((MIX_ROWS + BF16_ROWS, SSM_XBC), BF16),
               pltpu.VMEM(state_shape, F32),
               pltpu.VMEM((n_chunks,) + state_shape, F32),
               pltpu.VMEM((n_chunks,) + state_shape, BF16),
               pltpu.VMEM((n_chunks, SSD_CHUNK, D_BRANCH), F32),
               pltpu.VMEM((n_chunks, SSD_CHUNK, LANES), BF16),
               pltpu.VMEM((MIX_ROWS, D_BRANCH), F32)]
    cols = [(proj, SSM_XBC, 0), (proj, D_BRANCH, COL_Z), (aux, LANES, AUX_COL_DT)]
    return _mixer_call(_ssd_kernel, cols, consts, scratch, n_batch, seq, "ssd_mixer")


def _seg_mean(x, avg):
    return _dot_sel(x, avg)


def _ret_kernel(q_ref, k_ref, v_ref, g_ref, dcat_ref, qdec_ref, kdec_ref, cg_ref, pmask_ref,
                bd_ref, avg_ref, o_ref, state_ref, u_ref, rb_ref, part_ref):
    C = RET_CHUNK
    rows = q_ref.shape[0]
    n_chunks = rows // C
    halves = D_BRANCH // LANES
    sls = [slice(c * C, (c + 1) * C) for c in range(n_chunks)]

    @pl.when(pl.program_id(1) == 0)
    def _():
        state_ref[...] = jnp.zeros_like(state_ref)

    pmask = pmask_ref[...]
    for c in range(n_chunks):
        qb, kb, vb = q_ref[sls[c], :], k_ref[sls[c], :], v_ref[sls[c], :]
        outs = []
        for g in range(halves):
            scores = _dot_nt(qb[:, g * LANES:(g + 1) * LANES], _pair_stack(kb, g, pmask))
            p = (scores * dcat_ref[:, g * 2 * C:(g + 1) * 2 * C]).astype(BF16)
            outs.append(_dot(p, _pair_stack(vb, g, pmask)))
        part_ref[sls[c], :] = jnp.concatenate(outs, axis=1)
        u_ref[c] = bd_ref[...] * _dot_tn((kb.astype(F32) * kdec_ref[...]).astype(BF16), vb)

    r = state_ref[...]
    for c in range(n_chunks):
        rb_ref[c] = r.astype(BF16)
        r = r * cg_ref[...] + u_ref[c]
    state_ref[...] = r

    for c in range(n_chunks):
        y = part_ref[sls[c], :] + _dot((q_ref[sls[c], :].astype(F32) * qdec_ref[...]).astype(BF16), rb_ref[c])
        mu = _seg_mean(y, avg_ref[...])
        yc = y - mu
        var = _seg_mean(yc * yc, avg_ref[...])
        o_ref[sls[c], :] = (yc * lax.rsqrt(var + LN_EPS) * _silu(g_ref[sls[c], :].astype(F32))).astype(o_ref.dtype)


def _ret_consts():
    C = RET_CHUNK
    log_g = jnp.log(1.0 - 2.0 ** (-5.0 - jnp.arange(N_HEADS, dtype=F32)))
    pos = jnp.arange(C, dtype=F32)
    dist = pos[:, None] - pos[None, :]
    intra = jnp.where(dist >= 0, jnp.exp(log_g[:, None, None] * jnp.maximum(dist, 0.0)), 0.0)
    scale = HEAD_DIM ** -0.5
    dcat = jnp.concatenate([intra[h] for h in range(N_HEADS)], axis=1) * scale
    k_decay = jnp.exp(log_g[:, None] * (C - 1.0 - pos)[None, :]) * scale
    q_decay = jnp.exp(log_g[:, None] * (pos + 1.0)[None, :])
    kdec = jnp.repeat(k_decay.T, HEAD_DIM, axis=1)
    qdec = jnp.repeat(q_decay.T, HEAD_DIM, axis=1)
    cg = jnp.repeat(jnp.exp(log_g * C), HEAD_DIM).reshape(1, D_BRANCH)
    return (dcat, qdec, kdec, cg, _pair_masks(),
            jnp.asarray(_head_block_diag()), jnp.asarray(_head_block_diag(1.0 / HEAD_DIM), BF16))


def _retention(proj, n_batch, seq):
    n_chunks = MIX_ROWS // RET_CHUNK
    scratch = [pltpu.VMEM((D_BRANCH, D_BRANCH), F32),
               pltpu.VMEM((n_chunks, D_BRANCH, D_BRANCH), F32),
               pltpu.VMEM((n_chunks, D_BRANCH, D_BRANCH), BF16),
               pltpu.VMEM((MIX_ROWS, D_BRANCH), F32)]
    cols = [(proj, D_BRANCH, COL_RET + i) for i in range(4)]
    return _mixer_call(_ret_kernel, cols, list(_ret_consts()), scratch, n_batch, seq, "retention_mixer")


def _hgrn_levels():
    return int(math.log2(HGRN_CHUNK))


def _hgrn_kernel(q_ref, i_ref, g_ref, f_ref, lb_ref, nw_ref, mstack_ref, sign_ref, lmask_ref, pmask_ref,
                 bd_ref, avg_ref, o_ref, state_ref, e_ref, u_ref, stb_ref, part_ref):
    C = HGRN_CHUNK
    nl = _hgrn_levels()
    rows = q_ref.shape[0]
    halves = D_BRANCH // LANES

    @pl.when(pl.program_id(1) == 0)
    def _():
        state_ref[...] = jnp.zeros_like(state_ref)

    pmask = pmask_ref[...]
    lb = lb_ref[...]

    def head_scores(qb, kb):
        return jnp.concatenate(
            [_dot_nt(qb[:, g * LANES:(g + 1) * LANES], _pair_stack(kb, g, pmask)) for g in range(halves)], axis=1)

    n_chunks = rows // C
    sls = [slice(c * C, (c + 1) * C) for c in range(n_chunks)]

    k_rows = slice((nl + 2) * C, (nl + 3) * C)
    nm = HGRN_MATMUL_LEVELS
    for c in range(n_chunks):
        f = lb + (1.0 - lb) * _sigmoid(f_ref[sls[c], :])
        x = _sel_dot(mstack_ref[...], jnp.log(f))
        b = x[nm * C:(nm + 1) * C, :]
        e_ref[c, 0:nm * C, :] = jnp.exp(x[0:nm * C, :])
        e_ref[c, nl * C:(nl + 1) * C, :] = jnp.exp(b)
        for j in range(nm, nl):
            s = 1 << j
            bnd = jnp.concatenate([jnp.broadcast_to(b[r + s - 1:r + s, :], (2 * s, D_BRANCH))
                                   for r in range(0, C, 2 * s)], axis=0)
            e_ref[c, j * C:(j + 1) * C, :] = jnp.exp((b - bnd) * sign_ref[j - nm])
        e_ref[c, (nl + 1) * C:(nl + 2) * C, :] = jnp.exp(jnp.broadcast_to(b[C - 1:C, :], (C, D_BRANCH)) - b)
        e_ref[c, k_rows, :] = 1.0 - f

    group = 2
    for c0 in range(0, n_chunks, group):
        cs = range(c0, min(c0 + group, n_chunks))
        att = {c: lmask_ref[nl] * head_scores(q_ref[sls[c], :], e_ref[c, k_rows, :].astype(BF16))
               for c in cs}
        for j in range(nl):
            for c in cs:
                ej = e_ref[c, j * C:(j + 1) * C, :]
                att[c] = att[c] + lmask_ref[j] * head_scores((q_ref[sls[c], :].astype(F32) * ej).astype(BF16),
                                                             (e_ref[c, k_rows, :] * ej).astype(BF16))
        for c in cs:
            attb = att[c].astype(BF16)
            vb = i_ref[sls[c], :]
            part_ref[sls[c], :] = jnp.concatenate(
                [_dot(attb[:, g * 2 * C:(g + 1) * 2 * C], _pair_stack(vb, g, pmask)) for g in range(halves)], axis=1)

    for c in range(n_chunks):
        er = e_ref[c, (nl + 1) * C:(nl + 2) * C, :]
        u_ref[c] = bd_ref[...] * _dot_tn(i_ref[sls[c], :], (e_ref[c, k_rows, :] * er).astype(BF16))

    st = state_ref[...]
    for c in range(n_chunks):
        stb_ref[c] = st.astype(BF16)
        st = st * e_ref[c, (nl + 1) * C - 1:(nl + 1) * C, :] + u_ref[c]
    state_ref[...] = st

    for c in range(n_chunks):
        eb = e_ref[c, nl * C:(nl + 1) * C, :]
        o = part_ref[sls[c], :] + _dot_nt((q_ref[sls[c], :].astype(F32) * eb).astype(BF16), stb_ref[c])
        ms = _seg_mean(o * o, avg_ref[...])
        o_ref[sls[c], :] = (o * lax.rsqrt(ms + RMS_EPS) * nw_ref[...]
                            * _silu(g_ref[sls[c], :].astype(F32))).astype(o_ref.dtype)


def _hgrn_consts():
    C = HGRN_CHUNK
    nl = _hgrn_levels()
    t = np.arange(C)[:, None]
    u = np.arange(C)[None, :]
    blocks, masks, signs = [], [], []
    for j in range(nl):
        s = 1 << j
        bnd = (t // (2 * s)) * (2 * s) + s - 1
        upper = (t % (2 * s)) >= s
        if j < HGRN_MATMUL_LEVELS:
            m = np.where(upper, (u > bnd) & (u <= t), (u > t) & (u <= bnd))
            blocks.append(m.astype(np.float32))
        else:
            signs.append(np.broadcast_to(np.where(upper, 1.0, -1.0), (C, D_BRANCH)).astype(np.float32))
        pair = (t // (2 * s)) == (u // (2 * s))
        masks.append((upper & pair & ((u % (2 * s)) < s)).astype(np.float32))
    blocks.append((u <= t).astype(np.float32))
    masks.append((u == t).astype(np.float32))
    mstack = np.concatenate(blocks, axis=0)
    lmask = np.stack([np.tile(m, (1, N_HEADS)) for m in masks])
    return (jnp.asarray(mstack, BF16), jnp.asarray(np.stack(signs)), jnp.asarray(lmask), _pair_masks(),
            jnp.asarray(_head_block_diag()), jnp.asarray(_head_block_diag(1.0 / HEAD_DIM), BF16))


def _hgrn2(proj, aux, lb, norm_w, n_batch, seq):
    consts = [lb.reshape(1, D_BRANCH).astype(F32), norm_w.reshape(1, D_BRANCH).astype(F32), *_hgrn_consts()]
    n_chunks = HGRN_ROWS // HGRN_CHUNK
    scratch = [pltpu.VMEM((D_BRANCH, D_BRANCH), F32),
               pltpu.VMEM((n_chunks, (_hgrn_levels() + 3) * HGRN_CHUNK, D_BRANCH), F32),
               pltpu.VMEM((n_chunks, D_BRANCH, D_BRANCH), F32),
               pltpu.VMEM((n_chunks, D_BRANCH, D_BRANCH), BF16),
               pltpu.VMEM((HGRN_ROWS, D_BRANCH), F32)]
    cols = [(proj, D_BRANCH, COL_HGRN + i) for i in range(3)] + [(aux, D_BRANCH, AUX_COL_F)]
    return _mixer_call(_hgrn_kernel, cols, consts, scratch, n_batch, seq, "hgrn2_mixer", rows=HGRN_ROWS)


MOBA_VROWS = 80
MOBA_BODY_POSITIONS = 4


def _moba_kernel(q_ref, k_ref, v_ref, g_ref, ksel_ref, kpos_ref, kblk_ref, qaug_ref, o_ref,
                 kaug_ref, vt_ref, kmean_ref, sel_ref, qt_ref, acc_ref, m_ref, mcur_ref, alpha_ref,
                 s_ref, p_ref):
    S = MOBA_BLOCK
    i = pl.program_id(1)
    nbp = kmean_ref.shape[0]

    @pl.when(i == 0)
    def _():
        kmean_ref[...] = jnp.zeros_like(kmean_ref)

    kb = k_ref[...]
    kmean_ref[pl.ds(i, 1), :] = jnp.mean(kb.astype(F32), axis=0, keepdims=True)
    v_t = v_ref[...].astype(F32).T
    q_t = (q_ref[...].astype(F32) * (HEAD_DIM ** -0.5)).T
    ones_pad = (lax.broadcasted_iota(jnp.int32, (MOBA_VROWS - HEAD_DIM, S), 0) == 0).astype(F32)
    blk = lax.broadcasted_iota(jnp.int32, (nbp, S), 0)
    blk_f = blk.astype(F32)
    past = blk < i
    kmean = kmean_ref[...]
    own_rows = (lax.broadcasted_iota(jnp.int32, (SUBLANES, S), 0) == 0).astype(F32)
    kextra = kpos_ref[...] + i.astype(F32) * kblk_ref[...]
    for h in range(N_HEADS):
        hs = slice(h * HEAD_DIM, (h + 1) * HEAD_DIM)
        kaug_ref[i, h] = (_dot(kb, ksel_ref[h]) + kextra).astype(BF16)
        vt_ref[i, h] = jnp.concatenate([v_t[hs, :], ones_pad], axis=0).astype(BF16)
        qh_t = q_t[hs, :]
        qt_ref[h] = jnp.concatenate([qh_t * LOG2E, qaug_ref[h]], axis=0).astype(BF16)
        km_hi, km_mid, _ = _split3(kmean[:, hs])
        q_hi, q_mid, _ = _split3(qh_t)
        gate = _dot(km_hi, q_hi) + _dot(km_hi, q_mid) + _dot(km_mid, q_hi)
        gate = jnp.where(past, gate, -jnp.inf)
        sel = jnp.zeros((nbp, S), F32)
        for _ in range(MOBA_TOPK):
            best = jnp.max(gate, axis=0, keepdims=True)
            first = jnp.min(jnp.where(gate == best, blk_f, float(nbp)), axis=0, keepdims=True)
            hit = blk_f == first
            sel = jnp.where(hit, 1.0, sel)
            gate = jnp.where(hit, -jnp.inf, gate)
        sel_ref[h, 0:nbp, :] = jnp.where(past, sel, 0.0)
        sel_ref[h, nbp:nbp + SUBLANES, :] = own_rows

    def block_of(t):
        return jnp.where(t == 0, i, jnp.minimum(t - 1, i))

    def sel_row_of(t):
        return jnp.where(t == 0, nbp, jnp.where(t <= i, t - 1, nbp + 1))

    def track_max(h, slot, s_t, chosen):
        m_old = m_ref[h]
        m_new = jnp.where(chosen, jnp.maximum(m_old, jnp.max(s_t, axis=0, keepdims=True)), m_old)
        m_ref[h] = m_new
        mcur_ref[slot, h] = m_new
        alpha_ref[slot, h] = jnp.exp2(m_old - m_new)

    def score_phase(t, slot):
        n, row = block_of(t), sel_row_of(t)
        for h in range(N_HEADS):
            s_t = _dot(kaug_ref[n, h], qt_ref[h])
            s_ref[slot, h] = s_t
            track_max(h, slot, s_t, sel_ref[h, pl.ds(row, 1), :] > 0.0)

    def value_phase(t, slot):
        n, row = block_of(t), sel_row_of(t)
        for h in range(N_HEADS):
            p_ref[h] = jnp.exp2(s_ref[slot, h] - mcur_ref[slot, h]).astype(BF16)
        for h in range(N_HEADS):
            chosen = sel_ref[h, pl.ds(row, 1), :] > 0.0
            upd = _dot(vt_ref[n, h], p_ref[h])
            acc_ref[h] = alpha_ref[slot, h] * acc_ref[h] + jnp.where(chosen, upd, 0.0)

    key_pos = lax.broadcasted_iota(jnp.int32, (S, S), 0)
    qry_pos = lax.broadcasted_iota(jnp.int32, (S, S), 1)
    for h in range(N_HEADS):
        acc_ref[h] = jnp.zeros((MOBA_VROWS, S), F32)
        m_ref[h] = jnp.full((1, S), NEG_BIG, F32)
        s_t = jnp.where(key_pos <= qry_pos, _dot(kaug_ref[i, h], qt_ref[h]), NEG_BIG)
        s_ref[0, h] = s_t
        track_max(h, 0, s_t, True)

    def visit_positions(j, carry):
        t0 = MOBA_BODY_POSITIONS * j
        for r in range(0, MOBA_BODY_POSITIONS, 2):
            score_phase(t0 + r + 1, 1)
            value_phase(t0 + r, 0)
            score_phase(t0 + r + 2, 0)
            value_phase(t0 + r + 1, 1)
        return carry

    lax.fori_loop(0, (i + MOBA_BODY_POSITIONS) // MOBA_BODY_POSITIONS, visit_positions, 0)

    outs = []
    for h in range(N_HEADS):
        acc = acc_ref[h]
        outs.append(acc[:HEAD_DIM, :] / acc[HEAD_DIM:HEAD_DIM + 1, :])
    o_ref[...] = (jnp.concatenate(outs, axis=0).T * _silu(g_ref[...].astype(F32))).astype(o_ref.dtype)


def _moba(proj, n_batch, seq):
    S = MOBA_BLOCK
    assert seq % S == 0
    nb = seq // S
    nbp = -(-nb // SUBLANES) * SUBLANES
    ksel = np.zeros((N_HEADS, D_BRANCH, LANES), np.float32)
    for h in range(N_HEADS):
        ksel[h, h * HEAD_DIM + np.arange(HEAD_DIM), np.arange(HEAD_DIM)] = 1.0
    n_split = 3
    kpos = np.zeros((S, LANES), np.float32)
    kblk = np.zeros((S, LANES), np.float32)
    kpos[:, HEAD_DIM:HEAD_DIM + n_split] = np.arange(S, dtype=np.float32)[:, None]
    kblk[:, HEAD_DIM + n_split:HEAD_DIM + 2 * n_split] = 1.0
    slopes = 2.0 ** (-8.0 * np.arange(1, N_HEADS + 1, dtype=np.float32) / N_HEADS)
    qaug = np.zeros((N_HEADS, LANES - HEAD_DIM, S), np.float32)
    for h in range(N_HEADS):
        for base, coef in ((0, slopes[h] * LOG2E), (n_split, slopes[h] * LOG2E * S)):
            rem = np.float32(coef)
            for r in range(n_split):
                piece = np.float32(np.asarray(rem).astype(BF16))
                qaug[h, base + r, :] = piece
                rem = np.float32(rem - piece)
    consts = [jnp.asarray(ksel, BF16), jnp.asarray(kpos), jnp.asarray(kblk), jnp.asarray(qaug)]
    scratch = [pltpu.VMEM((nb, N_HEADS, S, LANES), BF16),
               pltpu.VMEM((nb, N_HEADS, MOBA_VROWS, S), BF16),
               pltpu.VMEM((nbp, D_BRANCH), F32),
               pltpu.VMEM((N_HEADS, nbp + SUBLANES, S), F32),
               pltpu.VMEM((N_HEADS, LANES, S), BF16),
               pltpu.VMEM((N_HEADS, MOBA_VROWS, S), F32),
               pltpu.VMEM((N_HEADS, 1, S), F32),
               pltpu.VMEM((2, N_HEADS, 1, S), F32),
               pltpu.VMEM((2, N_HEADS, 1, S), F32),
               pltpu.VMEM((2, N_HEADS, S, S), F32),
               pltpu.VMEM((N_HEADS, S, S), BF16)]
    vmem = nb * N_HEADS * S * (LANES + MOBA_VROWS) * 2 + (24 << 20)
    cols = [(proj, D_BRANCH, COL_MOBA + i) for i in range(4)]
    return _mixer_call(_moba_kernel, cols, consts, scratch, n_batch, seq, "moba_mixer", rows=S, vmem=vmem)


def _pack_w_in(w_in):
    depth, d, n_in = w_in.shape
    wt = jnp.transpose(w_in, (2, 0, 1)).reshape(n_in, depth, d // LANES, LANES)
    o = np.cumsum((0, D_BRANCH, SSM_XBC, N_HEADS) + (D_BRANCH,) * 12)
    z, xbc, dt = wt[o[0]:o[1]], wt[o[1]:o[2]], wt[o[2]:o[3]]
    hq, hf, hig, ret_moba = wt[o[3]:o[4]], wt[o[4]:o[5]], wt[o[5]:o[7]], wt[o[7]:]
    pad = jnp.zeros((N_PROJ - n_in,) + wt.shape[1:], wt.dtype)
    packed = jnp.concatenate([xbc, z, hq, hig, ret_moba, hf, dt, pad], axis=0)
    return jnp.transpose(packed, (1, 0, 2, 3)).reshape(depth, N_PROJ * (d // LANES), LANES)


def kernel(x, emb_ln_g, emb_ln_b, w_in, ssd_conv_w, ssd_conv_b, ssd_dt_bias, ssd_a_log, ssd_d, ssd_norm_w,
           hgrn_lb_logits, hgrn_norm_w, w_out, ln_g, ln_b):
    n_batch, seq, d_model = x.shape
    depth = w_in.shape[0]
    alpha = (2.0 * depth) ** 0.25
    assert seq % MIX_ROWS == 0 and seq % MOBA_BLOCK == 0

    lbs = jnp.cumsum(jax.nn.softmax(hgrn_lb_logits.astype(F32), axis=0), axis=0)
    lbs = lbs - lbs[0]

    wt_all = _pack_w_in(w_in.astype(F32))
    h = x.reshape(n_batch * seq, d_model)
    for l in range(depth):
        if l == 0:
            proj, aux, h = _inproj(h, wt_all, l, norm=(emb_ln_g, emb_ln_b))
        else:
            proj, aux = _inproj(h, wt_all, l)
        ys = [_ssd(proj, aux, ssd_conv_w[l], ssd_conv_b[l], ssd_dt_bias[l], ssd_a_log[l], ssd_d[l],
                   ssd_norm_w[l], n_batch, seq),
              _hgrn2(proj, aux, lbs[l], hgrn_norm_w[l], n_batch, seq),
              _retention(proj, n_batch, seq),
              _moba(proj, n_batch, seq)]
        h = _outproj(h, ys, w_out[l].astype(BF16), ln_g[l], ln_b[l], alpha)
    return h.reshape(n_batch, seq, d_model)
```

```python
import functools
import math

import numpy as np
import jax
import jax.numpy as jnp
from jax import lax
from jax.experimental import pallas as pl
from jax.experimental.pallas import tpu as pltpu

F32 = jnp.float32
BF16 = jnp.bfloat16

D_BRANCH = 256
HEAD_DIM = 64
N_HEADS = 4
SSM_STATE = 64
SSM_GROUPS = 2
SSM_CONV = 4
SSM_XBC = D_BRANCH + 2 * SSM_GROUPS * SSM_STATE
MOBA_BLOCK = 256
MOBA_TOPK = 3
LN_EPS = 1e-5
RMS_EPS = 1e-6
NEG_BIG = -1e30
LOG2E = 1.4426950408889634

LANES = 128
SUBLANES = 8
BF16_ROWS = 16
VMEM_BYTES_V7X = 64 * 1024 * 1024

SSD_CHUNK = 128
RET_CHUNK = 256
HGRN_CHUNK = 128
HGRN_MATMUL_LEVELS = 3
MIX_ROWS = 2048
HGRN_ROWS = 2048
HGRN_VMEM_BYTES = 52 << 20

N_PROJ = 4096
N_AUX = 512
COL_Z = SSM_XBC // D_BRANCH
COL_HGRN = COL_Z + 1
COL_RET = COL_HGRN + 3
COL_MOBA = COL_RET + 4
AUX_COL_F = 0
AUX_COL_DT = D_BRANCH // LANES


def _dot(a, b):
    return jnp.dot(a, b, preferred_element_type=F32)


def _dot_nt(a, b):
    return lax.dot_general(a, b, (((1,), (1,)), ((), ())), preferred_element_type=F32)


def _dot_tn(a, b):
    return lax.dot_general(a, b, (((0,), (0,)), ((), ())), preferred_element_type=F32)


def _split3(x):
    hi = x.astype(BF16)
    r1 = x - hi.astype(F32)
    mid = r1.astype(BF16)
    lo = (r1 - mid.astype(F32)).astype(BF16)
    return hi, mid, lo


def _split2(x):
    hi = x.astype(BF16)
    return hi, (x - hi.astype(F32)).astype(BF16)


def _sel_dot(sel, x):
    hi, lo = _split2(x)
    return _dot(sel, hi) + _dot(sel, lo)


def _dot_sel(x, sel):
    hi, lo = _split2(x)
    return _dot(hi, sel) + _dot(lo, sel)


def _sigmoid(x):
    return 1.0 / (1.0 + jnp.exp(-x))


def _silu(x):
    return x * _sigmoid(x)


def _softplus(x):
    return jnp.maximum(x, 0.0) + jnp.log(1.0 + jnp.exp(-jnp.abs(x)))


def _pair_masks():
    pm = np.zeros((LANES // HEAD_DIM, LANES), np.float32)
    for p in range(LANES // HEAD_DIM):
        pm[p, p * HEAD_DIM:(p + 1) * HEAD_DIM] = 1.0
    return jnp.asarray(pm, BF16)


def _pair_stack(xb, g, pmask):
    xg = xb[:, g * LANES:(g + 1) * LANES]
    return jnp.concatenate([xg * pmask[p:p + 1, :] for p in range(pmask.shape[0])], axis=0)


def _row_stack(x, masks):
    return jnp.concatenate([x * masks[h:h + 1, :] for h in range(masks.shape[0])], axis=0)


def _params(n_axes, vmem_bytes):
    return pltpu.CompilerParams(dimension_semantics=("arbitrary",) * n_axes,
                                vmem_limit_bytes=int(vmem_bytes))


def _ln_rows(x, g, b):
    mu = jnp.mean(x, axis=-1, keepdims=True)
    xc = x - mu
    var = jnp.mean(xc * xc, axis=-1, keepdims=True)
    return xc * lax.rsqrt(var + LN_EPS) * g + b


def _inproj_kernel(*refs, tn, norm_first):
    if norm_first:
        x_ref, wt_ref, g_ref, b_ref, o_ref, aux_ref, h_ref, xb_ref, wtb_ref = refs
    else:
        x_ref, wt_ref, o_ref, aux_ref, xb_ref, wtb_ref = refs
    i, j = pl.program_id(0), pl.program_id(1)
    k_tiles = x_ref.shape[1] // LANES
    cols = pl.ds(pl.multiple_of(j * tn, tn), tn)

    @pl.when(j == 0)
    def _():
        if norm_first:
            h = _ln_rows(x_ref[...], g_ref[...], b_ref[...])
            h_ref[...] = h
            xb_ref[...] = h.astype(BF16)
        else:
            xb_ref[...] = x_ref[...].astype(BF16)

    @pl.when(i == 0)
    def _():
        wtb_ref[cols, :] = jnp.concatenate(
            [wt_ref[pl.ds(kt, tn, stride=k_tiles), :] for kt in range(k_tiles)], axis=1).astype(BF16)

    acc = _dot_nt(xb_ref[...], wtb_ref[cols, :])
    o_ref[...] = acc.astype(BF16)

    @pl.when(j == pl.num_programs(1) - 1)
    def _():
        aux_ref[...] = acc[:, tn - aux_ref.shape[1]:]


def _inproj(x2, wt_all, layer, norm=None, tm=1024, tn=1024):
    m, d = x2.shape
    rows = wt_all.shape[1]
    n = rows * LANES // d
    n_col_tiles = n // tn
    tile_rows = tn * d // LANES
    assert N_AUX <= tn
    vmem = (2 * tm * d * 4 + 2 * tile_rows * LANES * 4 + 2 * tm * tn * 2 + 2 * tm * N_AUX * 4 + tm * tn * 4
            + tm * d * 2 + n * d * 2 + (6 << 20))
    row_spec = pl.BlockSpec((tm, d), lambda i, j: (i, 0))
    vec_spec = pl.BlockSpec((1, d), lambda i, j: (0, 0))
    in_specs = [row_spec,
                pl.BlockSpec((None, tile_rows, LANES),
                             lambda i, j: (layer, jnp.where(i == 0, j, n_col_tiles - 1), 0))]
    out_specs = [pl.BlockSpec((tm, tn), lambda i, j: (i, j)),
                 pl.BlockSpec((tm, N_AUX), lambda i, j: (i, 0))]
    out_shape = [jax.ShapeDtypeStruct((m, n), BF16), jax.ShapeDtypeStruct((m, N_AUX), F32)]
    operands = [x2, wt_all]
    if norm is not None:
        in_specs += [vec_spec, vec_spec]
        out_specs.append(row_spec)
        out_shape.append(jax.ShapeDtypeStruct((m, d), F32))
        operands += [norm[0].reshape(1, d), norm[1].reshape(1, d)]
        vmem += 2 * tm * d * 4
    return pl.pallas_call(
        functools.partial(_inproj_kernel, tn=tn, norm_first=norm is not None),
        grid=(m // tm, n_col_tiles),
        in_specs=in_specs,
        out_specs=out_specs,
        out_shape=out_shape,
        scratch_shapes=[pltpu.VMEM((tm, d), BF16), pltpu.VMEM((n, d), BF16)],
        compiler_params=_params(2, vmem),
        name="in_projection",
    )(*operands)


def _outproj_kernel(h_ref, y0_ref, y1_ref, y2_ref, y3_ref, w_ref, g_ref, b_ref, o_ref, *, alpha):
    rows = h_ref.shape[0]
    for r in range(0, rows, rows // 2):
        sl = slice(r, r + rows // 2)
        ycat = jnp.concatenate([y0_ref[sl, :], y1_ref[sl, :], y2_ref[sl, :], y3_ref[sl, :]], axis=1)
        y = _dot(ycat, w_ref[...])
        o_ref[sl, :] = _ln_rows(alpha * h_ref[sl, :] + y, g_ref[...], b_ref[...])


def _outproj(h2, ys, w_bf16, g, b, alpha, tm=1024):
    m, d = h2.shape
    dm = w_bf16.shape[0]
    yspec = pl.BlockSpec((tm, D_BRANCH), lambda i: (i, 0))
    vmem = 2 * (2 * tm * d * 4 + tm * dm * 2 + dm * d * 2) + tm * d * 8 + (4 << 20)
    return pl.pallas_call(
        functools.partial(_outproj_kernel, alpha=alpha),
        grid=(m // tm,),
        in_specs=[pl.BlockSpec((tm, d), lambda i: (i, 0)), yspec, yspec, yspec, yspec,
                  pl.BlockSpec((dm, d), lambda i: (0, 0)),
                  pl.BlockSpec((1, d), lambda i: (0, 0)),
                  pl.BlockSpec((1, d), lambda i: (0, 0))],
        out_specs=pl.BlockSpec((tm, d), lambda i: (i, 0)),
        out_shape=jax.ShapeDtypeStruct((m, d), F32),
        compiler_params=_params(1, vmem),
        name="out_projection_layernorm",
    )(h2, *ys, w_bf16, g.reshape(1, d), b.reshape(1, d))


def _head_lane_masks():
    m = np.zeros((N_HEADS, D_BRANCH), np.float32)
    for h in range(N_HEADS):
        m[h, h * HEAD_DIM:(h + 1) * HEAD_DIM] = 1.0
    return m


def _head_block_diag(value=1.0):
    hm = _head_lane_masks()
    return (hm.T @ hm) * value


def _mixer_call(kernel, col_blocks, consts, scratch, n_batch, seq, name, rows=MIX_ROWS, vmem=40 << 20):
    steps = seq // rows
    in_specs = [pl.BlockSpec((rows, w), functools.partial(lambda b, t, c: (b * steps + t, c), c=c))
                for (_, w, c) in col_blocks]
    in_specs += [pl.BlockSpec(a.shape, functools.partial(lambda b, t, nd: (0,) * nd, nd=a.ndim))
                 for a in consts]
    return pl.pallas_call(
        kernel,
        grid=(n_batch, steps),
        in_specs=in_specs,
        out_specs=pl.BlockSpec((rows, D_BRANCH), lambda b, t: (b * steps + t, 0)),
        out_shape=jax.ShapeDtypeStruct((n_batch * seq, D_BRANCH), BF16),
        scratch_shapes=scratch,
        compiler_params=_params(2, vmem),
        name=name,
    )(*[a for (a, _, _) in col_blocks], *consts)


def _ssd_kernel(xbc_ref, z_ref, dt_ref, convw_ref, convb_ref, dtb_ref, aneg_ref, dskip_ref, nw_ref,
                tri_ref, negmask_ref, exph_ref, pmask_ref, smask_ref, shift_ref,
                o_ref, xp_ref, state_ref, u_ref, sb_ref, ea_ref, cm_ref, part_ref):
    C = SSD_CHUNK
    rows = xbc_ref.shape[0]
    n_chunks = rows // C
    halves = D_BRANCH // LANES
    pad = BF16_ROWS
    sls = [slice(c * C, (c + 1) * C) for c in range(n_chunks)]

    @pl.when(pl.program_id(1) == 0)
    def _():
        xp_ref[0:pad, :] = jnp.zeros((pad, SSM_XBC), BF16)
        state_ref[...] = jnp.zeros_like(state_ref)

    xp_ref[pad:pad + rows, :] = xbc_ref[...]

    convw = convw_ref[...]
    pmask = pmask_ref[...]
    exph = exph_ref[...]
    xs, bmb, lcat, dth, ach = [], [], [], [], []
    for c in range(n_chunks):
        r0 = c * C
        ext = xp_ref[r0:r0 + C + pad, :]
        shifted = _dot(shift_ref[...], ext)
        acc = convb_ref[...] + convw[SSM_CONV - 1:SSM_CONV, :] * ext[pad:, :].astype(F32)
        for j in range(SSM_CONV - 1):
            acc = acc + convw[j:j + 1, :] * shifted[j * C:(j + 1) * C, :]
        xbc = _silu(acc)
        xs.append(xbc[:, :D_BRANCH])
        bmb.append(xbc[:, D_BRANCH:D_BRANCH + LANES].astype(BF16))
        cm_ref[c] = xbc[:, D_BRANCH + LANES:].astype(BF16)
    for c in range(n_chunks):
        dt = _softplus(dt_ref[sls[c], :] + dtb_ref[...])
        a_cum = _sel_dot(tri_ref[...], dt * aneg_ref[...])
        a_cum_t = a_cum.T
        col = jnp.concatenate(
            [jnp.broadcast_to(a_cum[:, h:h + 1], (C, C)) for h in range(N_HEADS)], axis=1)
        row = jnp.concatenate([a_cum_t[h:h + 1, :] for h in range(N_HEADS)], axis=1)
        lcat.append(jnp.exp(col - row + negmask_ref[...]))
        dth.append(_dot_sel(dt, exph))
        ach.append(_dot_sel(a_cum, exph))
    heads_per_group = N_HEADS // SSM_GROUPS
    for c in range(n_chunks):
        sg = _dot_nt(cm_ref[c], _pair_stack(bmb[c], 0, pmask))
        s4 = jnp.concatenate([sg[:, (h // heads_per_group) * C:(h // heads_per_group + 1) * C]
                              for h in range(N_HEADS)], axis=1)
        p = (s4 * lcat[c]).astype(BF16)
        xdt = xs[c] * dth[c]
        xdtb = xdt.astype(BF16)
        y = jnp.concatenate(
            [_dot(p[:, g * 2 * C:(g + 1) * 2 * C], _pair_stack(xdtb, g, pmask)) for g in range(halves)], axis=1)
        part_ref[sls[c], :] = y + xs[c] * dskip_ref[...]
        alast = ach[c][C - 1:C, :]
        u_ref[c] = smask_ref[...] * _dot_tn(bmb[c], (xdt * jnp.exp(alast - ach[c])).astype(BF16))
        ea_ref[c] = jnp.exp(ach[c])

    xp_ref[0:pad, :] = xbc_ref[rows - pad:rows, :]

    st = state_ref[...]
    for c in range(n_chunks):
        sb_ref[c] = st.astype(BF16)
        st = st * ea_ref[c, C - 1:C, :] + u_ref[c]
    state_ref[...] = st

    for c in range(n_chunks):
        y = part_ref[sls[c], :] + _dot(cm_ref[c], sb_ref[c]) * ea_ref[c]
        g = y * _silu(z_ref[sls[c], :].astype(F32))
        outs = []
        for grp in range(SSM_GROUPS):
            gg = g[:, grp * LANES:(grp + 1) * LANES]
            ms = jnp.mean(gg * gg, axis=-1, keepdims=True)
            outs.append(gg * lax.rsqrt(ms + RMS_EPS))
        o_ref[sls[c], :] = (jnp.concatenate(outs, axis=1) * nw_ref[...]).astype(o_ref.dtype)


def _ssd_consts():
    C = SSD_CHUNK
    idx = np.arange(C)
    tri = (idx[:, None] >= idx[None, :]).astype(np.float32)
    negmask = np.tile(np.where(tri > 0, 0.0, NEG_BIG).astype(np.float32), (1, N_HEADS))
    exph = np.zeros((LANES, D_BRANCH), np.float32)
    exph[:N_HEADS] = _head_lane_masks()
    heads_per_group = N_HEADS // SSM_GROUPS
    smask = np.zeros((LANES, D_BRANCH), np.float32)
    for h in range(N_HEADS):
        g = h // heads_per_group
        smask[g * SSM_STATE:(g + 1) * SSM_STATE, h * HEAD_DIM:(h + 1) * HEAD_DIM] = 1.0
    shift = np.zeros(((SSM_CONV - 1) * C, C + BF16_ROWS), np.float32)
    for j in range(SSM_CONV - 1):
        shift[j * C + idx, BF16_ROWS - (SSM_CONV - 1) + j + idx] = 1.0
    return (jnp.asarray(tri, BF16), jnp.asarray(negmask), jnp.asarray(exph, BF16),
            _pair_masks(), jnp.asarray(smask), jnp.asarray(shift, BF16))


def _ssd(proj, aux, conv_w, conv_b, dt_bias, a_log, d_skip, norm_w, n_batch, seq):
    lane_pad = LANES - N_HEADS
    dtb = jnp.pad(dt_bias.astype(F32), (0, lane_pad)).reshape(1, LANES)
    aneg = jnp.pad(-jnp.exp(a_log.astype(F32)), (0, lane_pad)).reshape(1, LANES)
    dskip = jnp.repeat(d_skip.astype(F32), HEAD_DIM).reshape(1, D_BRANCH)
    consts = [conv_w.astype(F32), conv_b.reshape(1, SSM_XBC).astype(F32), dtb, aneg, dskip,
              norm_w.reshape(1, D_BRANCH).astype(F32), *_ssd_consts()]
    n_chunks = MIX_ROWS // SSD_CHUNK
    state_shape = (SSM_GROUPS * SSM_STATE, D_BRANCH)
    scratch = [pltpu.VMEM((MIX_ROWS + BF16_ROWS, SSM_XBC), BF16),
               pltpu.VMEM(state_shape, F32),
               pltpu.VMEM((n_chunks,) + state_shape, F32),
               pltpu.VMEM((n_chunks,) + state_shape, BF16),
               pltpu.VMEM((n_chunks, SSD_CHUNK, D_BRANCH), F32),
               pltpu.VMEM((n_chunks, SSD_CHUNK, LANES), BF16),
               pltpu.VMEM((MIX_ROWS, D_BRANCH), F32)]
    cols = [(proj, SSM_XBC, 0), (proj, D_BRANCH, COL_Z), (aux, LANES, AUX_COL_DT)]
    return _mixer_call(_ssd_kernel, cols, consts, scratch, n_batch, seq, "ssd_mixer")


def _seg_mean(x, avg):
    return _dot_sel(x, avg)


def _ret_kernel(q_ref, k_ref, v_ref, g_ref, dcat_ref, qdec_ref, kdec_ref, cg_ref, pmask_ref,
                bd_ref, avg_ref, o_ref, state_ref, u_ref, rb_ref, part_ref):
    C = RET_CHUNK
    rows = q_ref.shape[0]
    n_chunks = rows // C
    halves = D_BRANCH // LANES
    sls = [slice(c * C, (c + 1) * C) for c in range(n_chunks)]

    @pl.when(pl.program_id(1) == 0)
    def _():
        state_ref[...] = jnp.zeros_like(state_ref)

    pmask = pmask_ref[...]
    for c in range(n_chunks):
        qb, kb, vb = q_ref[sls[c], :], k_ref[sls[c], :], v_ref[sls[c], :]
        outs = []
        for g in range(halves):
            scores = _dot_nt(qb[:, g * LANES:(g + 1) * LANES], _pair_stack(kb, g, pmask))
            p = (scores * dcat_ref[:, g * 2 * C:(g + 1) * 2 * C]).astype(BF16)
            outs.append(_dot(p, _pair_stack(vb, g, pmask)))
        part_ref[sls[c], :] = jnp.concatenate(outs, axis=1)
        u_ref[c] = bd_ref[...] * _dot_tn((kb.astype(F32) * kdec_ref[...]).astype(BF16), vb)

    r = state_ref[...]
    for c in range(n_chunks):
        rb_ref[c] = r.astype(BF16)
        r = r * cg_ref[...] + u_ref[c]
    state_ref[...] = r

    for c in range(n_chunks):
        y = part_ref[sls[c], :] + _dot((q_ref[sls[c], :].astype(F32) * qdec_ref[...]).astype(BF16), rb_ref[c])
        mu = _seg_mean(y, avg_ref[...])
        yc = y - mu
        var = _seg_mean(yc * yc, avg_ref[...])
        o_ref[sls[c], :] = (yc * lax.rsqrt(var + LN_EPS) * _silu(g_ref[sls[c], :].astype(F32))).astype(o_ref.dtype)


def _ret_consts():
    C = RET_CHUNK
    log_g = jnp.log(1.0 - 2.0 ** (-5.0 - jnp.arange(N_HEADS, dtype=F32)))
    pos = jnp.arange(C, dtype=F32)
    dist = pos[:, None] - pos[None, :]
    intra = jnp.where(dist >= 0, jnp.exp(log_g[:, None, None] * jnp.maximum(dist, 0.0)), 0.0)
    scale = HEAD_DIM ** -0.5
    dcat = jnp.concatenate([intra[h] for h in range(N_HEADS)], axis=1) * scale
    k_decay = jnp.exp(log_g[:, None] * (C - 1.0 - pos)[None, :]) * scale
    q_decay = jnp.exp(log_g[:, None] * (pos + 1.0)[None, :])
    kdec = jnp.repeat(k_decay.T, HEAD_DIM, axis=1)
    qdec = jnp.repeat(q_decay.T, HEAD_DIM, axis=1)
    cg = jnp.repeat(jnp.exp(log_g * C), HEAD_DIM).reshape(1, D_BRANCH)
    return (dcat, qdec, kdec, cg, _pair_masks(),
            jnp.asarray(_head_block_diag()), jnp.asarray(_head_block_diag(1.0 / HEAD_DIM), BF16))


def _retention(proj, n_batch, seq):
    n_chunks = MIX_ROWS // RET_CHUNK
    scratch = [pltpu.VMEM((D_BRANCH, D_BRANCH), F32),
               pltpu.VMEM((n_chunks, D_BRANCH, D_BRANCH), F32),
               pltpu.VMEM((n_chunks, D_BRANCH, D_BRANCH), BF16),
               pltpu.VMEM((MIX_ROWS, D_BRANCH), F32)]
    cols = [(proj, D_BRANCH, COL_RET + i) for i in range(4)]
    return _mixer_call(_ret_kernel, cols, list(_ret_consts()), scratch, n_batch, seq, "retention_mixer")


def _hgrn_levels():
    return int(math.log2(HGRN_CHUNK))


def _hgrn_kernel(q_ref, i_ref, g_ref, f_ref, lb_ref, nw_ref, mstack_ref, sign_ref, lmask_ref, pmask_ref,
                 bd_ref, avg_ref, o_ref, state_ref, e_ref, u_ref, stb_ref, part_ref):
    C = HGRN_CHUNK
    nl = _hgrn_levels()
    rows = q_ref.shape[0]
    halves = D_BRANCH // LANES

    @pl.when(pl.program_id(1) == 0)
    def _():
        state_ref[...] = jnp.zeros_like(state_ref)

    pmask = pmask_ref[...]
    lb = lb_ref[...]

    def head_scores(qb, kb):
        return jnp.concatenate(
            [_dot_nt(qb[:, g * LANES:(g + 1) * LANES], _pair_stack(kb, g, pmask)) for g in range(halves)], axis=1)

    n_chunks = rows // C
    sls = [slice(c * C, (c + 1) * C) for c in range(n_chunks)]

    k_rows = slice((nl + 2) * C, (nl + 3) * C)
    nm = HGRN_MATMUL_LEVELS
    for c in range(n_chunks):
        f = lb + (1.0 - lb) * _sigmoid(f_ref[sls[c], :])
        x = _sel_dot(mstack_ref[...], jnp.log(f))
        b = x[nm * C:(nm + 1) * C, :]
        e_ref[c, 0:nm * C, :] = jnp.exp(x[0:nm * C, :])
        e_ref[c, nl * C:(nl + 1) * C, :] = jnp.exp(b)
        for j in range(nm, nl):
            s = 1 << j
            bnd = jnp.concatenate([jnp.broadcast_to(b[r + s - 1:r + s, :], (2 * s, D_BRANCH))
                                   for r in range(0, C, 2 * s)], axis=0)
            e_ref[c, j * C:(j + 1) * C, :] = jnp.exp((b - bnd) * sign_ref[j - nm])
        e_ref[c, (nl + 1) * C:(nl + 2) * C, :] = jnp.exp(jnp.broadcast_to(b[C - 1:C, :], (C, D_BRANCH)) - b)
        e_ref[c, k_rows, :] = 1.0 - f

    group = 2
    for c0 in range(0, n_chunks, group):
        cs = range(c0, min(c0 + group, n_chunks))
        att = {c: lmask_ref[nl] * head_scores(q_ref[sls[c], :], e_ref[c, k_rows, :].astype(BF16))
               for c in cs}
        for j in range(nl):
            for c in cs:
                ej = e_ref[c, j * C:(j + 1) * C, :]
                att[c] = att[c] + lmask_ref[j] * head_scores((q_ref[sls[c], :].astype(F32) * ej).astype(BF16),
                                                             (e_ref[c, k_rows, :] * ej).astype(BF16))
        for c in cs:
            attb = att[c].astype(BF16)
            vb = i_ref[sls[c], :]
            part_ref[sls[c], :] = jnp.concatenate(
                [_dot(attb[:, g * 2 * C:(g + 1) * 2 * C], _pair_stack(vb, g, pmask)) for g in range(halves)], axis=1)

    for c in range(n_chunks):
        er = e_ref[c, (nl + 1) * C:(nl + 2) * C, :]
        u_ref[c] = bd_ref[...] * _dot_tn(i_ref[sls[c], :], (e_ref[c, k_rows, :] * er).astype(BF16))

    st = state_ref[...]
    for c in range(n_chunks):
        stb_ref[c] = st.astype(BF16)
        st = st * e_ref[c, (nl + 1) * C - 1:(nl + 1) * C, :] + u_ref[c]
    state_ref[...] = st

    for c in range(n_chunks):
        eb = e_ref[c, nl * C:(nl + 1) * C, :]
        o = part_ref[sls[c], :] + _dot_nt((q_ref[sls[c], :].astype(F32) * eb).astype(BF16), stb_ref[c])
        ms = _seg_mean(o * o, avg_ref[...])
        o_ref[sls[c], :] = (o * lax.rsqrt(ms + RMS_EPS) * nw_ref[...]
                            * _silu(g_ref[sls[c], :].astype(F32))).astype(o_ref.dtype)


def _hgrn_consts():
    C = HGRN_CHUNK
    nl = _hgrn_levels()
    t = np.arange(C)[:, None]
    u = np.arange(C)[None, :]
    blocks, masks, signs = [], [], []
    for j in range(nl):
        s = 1 << j
        bnd = (t // (2 * s)) * (2 * s) + s - 1
        upper = (t % (2 * s)) >= s
        if j < HGRN_MATMUL_LEVELS:
            m = np.where(upper, (u > bnd) & (u <= t), (u > t) & (u <= bnd))
            blocks.append(m.astype(np.float32))
        else:
            signs.append(np.broadcast_to(np.where(upper, 1.0, -1.0), (C, D_BRANCH)).astype(np.float32))
        pair = (t // (2 * s)) == (u // (2 * s))
        masks.append((upper & pair & ((u % (2 * s)) < s)).astype(np.float32))
    blocks.append((u <= t).astype(np.float32))
    masks.append((u == t).astype(np.float32))
    mstack = np.concatenate(blocks, axis=0)
    lmask = np.stack([np.tile(m, (1, N_HEADS)) for m in masks])
    return (jnp.asarray(mstack, BF16), jnp.asarray(np.stack(signs)), jnp.asarray(lmask), _pair_masks(),
            jnp.asarray(_head_block_diag()), jnp.asarray(_head_block_diag(1.0 / HEAD_DIM), BF16))


def _hgrn2(proj, aux, lb, norm_w, n_batch, seq):
    consts = [lb.reshape(1, D_BRANCH).astype(F32), norm_w.reshape(1, D_BRANCH).astype(F32), *_hgrn_consts()]
    n_chunks = HGRN_ROWS // HGRN_CHUNK
    scratch = [pltpu.VMEM((D_BRANCH, D_BRANCH), F32),
               pltpu.VMEM((n_chunks, (_hgrn_levels() + 3) * HGRN_CHUNK, D_BRANCH), F32),
               pltpu.VMEM((n_chunks, D_BRANCH, D_BRANCH), F32),
               pltpu.VMEM((n_chunks, D_BRANCH, D_BRANCH), BF16),
               pltpu.VMEM((HGRN_ROWS, D_BRANCH), F32)]
    cols = [(proj, D_BRANCH, COL_HGRN + i) for i in range(3)] + [(aux, D_BRANCH, AUX_COL_F)]
    return _mixer_call(_hgrn_kernel, cols, consts, scratch, n_batch, seq, "hgrn2_mixer", rows=HGRN_ROWS,
                       vmem=HGRN_VMEM_BYTES)


MOBA_VROWS = 80
MOBA_BODY_POSITIONS = 4


def _moba_kernel(q_ref, k_ref, v_ref, g_ref, ksel_ref, kpos_ref, kblk_ref, qaug_ref, o_ref,
                 kaug_ref, vt_ref, kmean_ref, sel_ref, qt_ref, acc_ref, m_ref, mcur_ref, alpha_ref,
                 s_ref, p_ref):
    S = MOBA_BLOCK
    i = pl.program_id(1)
    nbp = kmean_ref.shape[0]

    @pl.when(i == 0)
    def _():
        kmean_ref[...] = jnp.zeros_like(kmean_ref)

    kb = k_ref[...]
    kmean_ref[pl.ds(i, 1), :] = jnp.mean(kb.astype(F32), axis=0, keepdims=True)
    v_t = v_ref[...].astype(F32).T
    q_t = (q_ref[...].astype(F32) * (HEAD_DIM ** -0.5)).T
    ones_pad = (lax.broadcasted_iota(jnp.int32, (MOBA_VROWS - HEAD_DIM, S), 0) == 0).astype(F32)
    blk = lax.broadcasted_iota(jnp.int32, (nbp, S), 0)
    blk_f = blk.astype(F32)
    past = blk < i
    kmean = kmean_ref[...]
    own_rows = (lax.broadcasted_iota(jnp.int32, (SUBLANES, S), 0) == 0).astype(F32)
    kextra = kpos_ref[...] + i.astype(F32) * kblk_ref[...]
    for h in range(N_HEADS):
        hs = slice(h * HEAD_DIM, (h + 1) * HEAD_DIM)
        kaug_ref[i, h] = (_dot(kb, ksel_ref[h]) + kextra).astype(BF16)
        vt_ref[i, h] = jnp.concatenate([v_t[hs, :], ones_pad], axis=0).astype(BF16)
        qh_t = q_t[hs, :]
        qt_ref[h] = jnp.concatenate([qh_t * LOG2E, qaug_ref[h]], axis=0).astype(BF16)
        km_hi, km_mid, _ = _split3(kmean[:, hs])
        q_hi, q_mid, _ = _split3(qh_t)
        gate = _dot(km_hi, q_hi) + _dot(km_hi, q_mid) + _dot(km_mid, q_hi)
        gate = jnp.where(past, gate, -jnp.inf)
        sel = jnp.zeros((nbp, S), F32)
        for _ in range(MOBA_TOPK):
            best = jnp.max(gate, axis=0, keepdims=True)
            first = jnp.min(jnp.where(gate == best, blk_f, float(nbp)), axis=0, keepdims=True)
            hit = blk_f == first
            sel = jnp.where(hit, 1.0, sel)
            gate = jnp.where(hit, -jnp.inf, gate)
        sel_ref[h, 0:nbp, :] = jnp.where(past, sel, 0.0)
        sel_ref[h, nbp:nbp + SUBLANES, :] = own_rows

    def block_of(t):
        return jnp.where(t == 0, i, jnp.minimum(t - 1, i))

    def sel_row_of(t):
        return jnp.where(t == 0, nbp, jnp.where(t <= i, t - 1, nbp + 1))

    def track_max(h, slot, s_t, chosen):
        m_old = m_ref[h]
        m_new = jnp.where(chosen, jnp.maximum(m_old, jnp.max(s_t, axis=0, keepdims=True)), m_old)
        m_ref[h] = m_new
        mcur_ref[slot, h] = m_new
        alpha_ref[slot, h] = jnp.exp2(m_old - m_new)

    def score_phase(t, slot):
        n, row = block_of(t), sel_row_of(t)
        for h in range(N_HEADS):
            s_t = _dot(kaug_ref[n, h], qt_ref[h])
            s_ref[slot, h] = s_t
            track_max(h, slot, s_t, sel_ref[h, pl.ds(row, 1), :] > 0.0)

    def value_phase(t, slot):
        n, row = block_of(t), sel_row_of(t)
        for h in range(N_HEADS):
            p_ref[h] = jnp.exp2(s_ref[slot, h] - mcur_ref[slot, h]).astype(BF16)
        for h in range(N_HEADS):
            chosen = sel_ref[h, pl.ds(row, 1), :] > 0.0
            upd = _dot(vt_ref[n, h], p_ref[h])
            acc_ref[h] = alpha_ref[slot, h] * acc_ref[h] + jnp.where(chosen, upd, 0.0)

    key_pos = lax.broadcasted_iota(jnp.int32, (S, S), 0)
    qry_pos = lax.broadcasted_iota(jnp.int32, (S, S), 1)
    for h in range(N_HEADS):
        acc_ref[h] = jnp.zeros((MOBA_VROWS, S), F32)
        m_ref[h] = jnp.full((1, S), NEG_BIG, F32)
        s_t = jnp.where(key_pos <= qry_pos, _dot(kaug_ref[i, h], qt_ref[h]), NEG_BIG)
        s_ref[0, h] = s_t
        track_max(h, 0, s_t, True)

    def visit_positions(j, carry):
        t0 = MOBA_BODY_POSITIONS * j
        for r in range(0, MOBA_BODY_POSITIONS, 2):
            score_phase(t0 + r + 1, 1)
            value_phase(t0 + r, 0)
            score_phase(t0 + r + 2, 0)
            value_phase(t0 + r + 1, 1)
        return carry

    lax.fori_loop(0, (i + MOBA_BODY_POSITIONS) // MOBA_BODY_POSITIONS, visit_positions, 0)

    outs = []
    for h in range(N_HEADS):
        acc = acc_ref[h]
        outs.append(acc[:HEAD_DIM, :] / acc[HEAD_DIM:HEAD_DIM + 1, :])
    o_ref[...] = (jnp.concatenate(outs, axis=0).T * _silu(g_ref[...].astype(F32))).astype(o_ref.dtype)


def _moba(proj, n_batch, seq):
    S = MOBA_BLOCK
    assert seq % S == 0
    nb = seq // S
    nbp = -(-nb // SUBLANES) * SUBLANES
    ksel = np.zeros((N_HEADS, D_BRANCH, LANES), np.float32)
    for h in range(N_HEADS):
        ksel[h, h * HEAD_DIM + np.arange(HEAD_DIM), np.arange(HEAD_DIM)] = 1.0
    n_split = 3
    kpos = np.zeros((S, LANES), np.float32)
    kblk = np.zeros((S, LANES), np.float32)
    kpos[:, HEAD_DIM:HEAD_DIM + n_split] = np.arange(S, dtype=np.float32)[:, None]
    kblk[:, HEAD_DIM + n_split:HEAD_DIM + 2 * n_split] = 1.0
    slopes = 2.0 ** (-8.0 * np.arange(1, N_HEADS + 1, dtype=np.float32) / N_HEADS)
    qaug = np.zeros((N_HEADS, LANES - HEAD_DIM, S), np.float32)
    for h in range(N_HEADS):
        for base, coef in ((0, slopes[h] * LOG2E), (n_split, slopes[h] * LOG2E * S)):
            rem = np.float32(coef)
            for r in range(n_split):
                piece = np.float32(np.asarray(rem).astype(BF16))
                qaug[h, base + r, :] = piece
                rem = np.float32(rem - piece)
    consts = [jnp.asarray(ksel, BF16), jnp.asarray(kpos), jnp.asarray(kblk), jnp.asarray(qaug)]
    scratch = [pltpu.VMEM((nb, N_HEADS, S, LANES), BF16),
               pltpu.VMEM((nb, N_HEADS, MOBA_VROWS, S), BF16),
               pltpu.VMEM((nbp, D_BRANCH), F32),
               pltpu.VMEM((N_HEADS, nbp + SUBLANES, S), F32),
               pltpu.VMEM((N_HEADS, LANES, S), BF16),
               pltpu.VMEM((N_HEADS, MOBA_VROWS, S), F32),
               pltpu.VMEM((N_HEADS, 1, S), F32),
               pltpu.VMEM((2, N_HEADS, 1, S), F32),
               pltpu.VMEM((2, N_HEADS, 1, S), F32),
               pltpu.VMEM((2, N_HEADS, S, S), F32),
               pltpu.VMEM((N_HEADS, S, S), BF16)]
    vmem = nb * N_HEADS * S * (LANES + MOBA_VROWS) * 2 + (24 << 20)
    cols = [(proj, D_BRANCH, COL_MOBA + i) for i in range(4)]
    return _mixer_call(_moba_kernel, cols, consts, scratch, n_batch, seq, "moba_mixer", rows=S, vmem=vmem)


def _pack_w_in(w_in):
    depth, d, n_in = w_in.shape
    wt = jnp.transpose(w_in, (2, 0, 1)).reshape(n_in, depth, d // LANES, LANES)
    o = np.cumsum((0, D_BRANCH, SSM_XBC, N_HEADS) + (D_BRANCH,) * 12)
    z, xbc, dt = wt[o[0]:o[1]], wt[o[1]:o[2]], wt[o[2]:o[3]]
    hq, hf, hig, ret_moba = wt[o[3]:o[4]], wt[o[4]:o[5]], wt[o[5]:o[7]], wt[o[7]:]
    pad = jnp.zeros((N_PROJ - n_in,) + wt.shape[1:], wt.dtype)
    packed = jnp.concatenate([xbc, z, hq, hig, ret_moba, hf, dt, pad], axis=0)
    return jnp.transpose(packed, (1, 0, 2, 3)).reshape(depth, N_PROJ * (d // LANES), LANES)


def kernel(x, emb_ln_g, emb_ln_b, w_in, ssd_conv_w, ssd_conv_b, ssd_dt_bias, ssd_a_log, ssd_d, ssd_norm_w,
           hgrn_lb_logits, hgrn_norm_w, w_out, ln_g, ln_b):
    n_batch, seq, d_model = x.shape
    depth = w_in.shape[0]
    alpha = (2.0 * depth) ** 0.25
    assert seq % MIX_ROWS == 0 and seq % MOBA_BLOCK == 0

    lbs = jnp.cumsum(jax.nn.softmax(hgrn_lb_logits.astype(F32), axis=0), axis=0)
    lbs = lbs - lbs[0]

    wt_all = _pack_w_in(w_in.astype(F32))
    h = x.reshape(n_batch * seq, d_model)
    for l in range(depth):
        if l == 0:
            proj, aux, h = _inproj(h, wt_all, l, norm=(emb_ln_g, emb_ln_b))
        else:
            proj, aux = _inproj(h, wt_all, l)
        ys = [_ssd(proj, aux, ssd_conv_w[l], ssd_conv_b[l], ssd_dt_bias[l], ssd_a_log[l], ssd_d[l],
                   ssd_norm_w[l], n_batch, seq),
              _hgrn2(proj, aux, lbs[l], hgrn_norm_w[l], n_batch, seq),
              _retention(proj, n_batch, seq),
              _moba(proj, n_batch, seq)]
        h = _outproj(h, ys, w_out[l].astype(BF16), ln_g[l], ln_b[l], alpha)
    return h.reshape(n_batch, seq, d_model)
```

```python
import functools
import math

import numpy as np
import jax
import jax.numpy as jnp
from jax import lax
from jax.experimental import pallas as pl
from jax.experimental.pallas import tpu as pltpu

F32 = jnp.float32
BF16 = jnp.bfloat16

D_BRANCH = 256
HEAD_DIM = 64
N_HEADS = 4
SSM_STATE = 64
SSM_GROUPS = 2
SSM_CONV = 4
SSM_XBC = D_BRANCH + 2 * SSM_GROUPS * SSM_STATE
MOBA_BLOCK = 256
MOBA_TOPK = 3
LN_EPS = 1e-5
RMS_EPS = 1e-6
NEG_BIG = -1e30
LOG2E = 1.4426950408889634

LANES = 128
SUBLANES = 8
BF16_ROWS = 16
VMEM_BYTES_V7X = 64 * 1024 * 1024

SSD_CHUNK = 128
RET_CHUNK = 256
HGRN_CHUNK = 128
HGRN_MATMUL_LEVELS = 3
MIX_ROWS = 2048
OUTPROJ_INPUT_BUFFERS = 3
HGRN_ROWS = 2048
HGRN_VMEM_BYTES = 52 << 20

N_PROJ = 4096
N_AUX = 512
COL_Z = SSM_XBC // D_BRANCH
COL_HGRN = COL_Z + 1
COL_RET = COL_HGRN + 3
COL_MOBA = COL_RET + 4
AUX_COL_F = 0
AUX_COL_DT = D_BRANCH // LANES


def _dot(a, b):
    return jnp.dot(a, b, preferred_element_type=F32)


def _dot_nt(a, b):
    return lax.dot_general(a, b, (((1,), (1,)), ((), ())), preferred_element_type=F32)


def _dot_tn(a, b):
    return lax.dot_general(a, b, (((0,), (0,)), ((), ())), preferred_element_type=F32)


def _split3(x):
    hi = x.astype(BF16)
    r1 = x - hi.astype(F32)
    mid = r1.astype(BF16)
    lo = (r1 - mid.astype(F32)).astype(BF16)
    return hi, mid, lo


def _split2(x):
    hi = x.astype(BF16)
    return hi, (x - hi.astype(F32)).astype(BF16)


def _sel_dot(sel, x):
    hi, lo = _split2(x)
    return _dot(sel, hi) + _dot(sel, lo)


def _dot_sel(x, sel):
    hi, lo = _split2(x)
    return _dot(hi, sel) + _dot(lo, sel)


def _sigmoid(x):
    return 1.0 / (1.0 + jnp.exp(-x))


def _silu(x):
    return x * _sigmoid(x)


def _softplus(x):
    return jnp.maximum(x, 0.0) + jnp.log(1.0 + jnp.exp(-jnp.abs(x)))


def _pair_masks():
    pm = np.zeros((LANES // HEAD_DIM, LANES), np.float32)
    for p in range(LANES // HEAD_DIM):
        pm[p, p * HEAD_DIM:(p + 1) * HEAD_DIM] = 1.0
    return jnp.asarray(pm, BF16)


def _pair_stack(xb, g, pmask):
    xg = xb[:, g * LANES:(g + 1) * LANES]
    return jnp.concatenate([xg * pmask[p:p + 1, :] for p in range(pmask.shape[0])], axis=0)


def _row_stack(x, masks):
    return jnp.concatenate([x * masks[h:h + 1, :] for h in range(masks.shape[0])], axis=0)


def _params(n_axes, vmem_bytes):
    return pltpu.CompilerParams(dimension_semantics=("arbitrary",) * n_axes,
                                vmem_limit_bytes=int(vmem_bytes))


def _ln_rows(x, g, b):
    mu = jnp.mean(x, axis=-1, keepdims=True)
    xc = x - mu
    var = jnp.mean(xc * xc, axis=-1, keepdims=True)
    return xc * lax.rsqrt(var + LN_EPS) * g + b


def _inproj_kernel(*refs, tn, norm_first):
    if norm_first:
        x_ref, wt_ref, g_ref, b_ref, o_ref, aux_ref, h_ref, xb_ref, wtb_ref = refs
    else:
        x_ref, wt_ref, o_ref, aux_ref, xb_ref, wtb_ref = refs
    i, j = pl.program_id(0), pl.program_id(1)
    k_tiles = x_ref.shape[1] // LANES
    cols = pl.ds(pl.multiple_of(j * tn, tn), tn)

    @pl.when(j == 0)
    def _():
        if norm_first:
            h = _ln_rows(x_ref[...], g_ref[...], b_ref[...])
            h_ref[...] = h
            xb_ref[...] = h.astype(BF16)
        else:
            xb_ref[...] = x_ref[...].astype(BF16)

    @pl.when(i == 0)
    def _():
        wtb_ref[cols, :] = jnp.concatenate(
            [wt_ref[pl.ds(kt, tn, stride=k_tiles), :] for kt in range(k_tiles)], axis=1).astype(BF16)

    acc = _dot_nt(xb_ref[...], wtb_ref[cols, :])
    o_ref[...] = acc.astype(BF16)

    @pl.when(j == pl.num_programs(1) - 1)
    def _():
        aux_ref[...] = acc[:, tn - aux_ref.shape[1]:]


def _inproj(x2, wt_all, layer, norm=None, tm=1024, tn=1024):
    m, d = x2.shape
    rows = wt_all.shape[1]
    n = rows * LANES // d
    n_col_tiles = n // tn
    tile_rows = tn * d // LANES
    assert N_AUX <= tn
    vmem = (2 * tm * d * 4 + 2 * tile_rows * LANES * 4 + 2 * tm * tn * 2 + 2 * tm * N_AUX * 4 + tm * tn * 4
            + tm * d * 2 + n * d * 2 + (6 << 20))
    row_spec = pl.BlockSpec((tm, d), lambda i, j: (i, 0))
    vec_spec = pl.BlockSpec((1, d), lambda i, j: (0, 0))
    in_specs = [row_spec,
                pl.BlockSpec((None, tile_rows, LANES),
                             lambda i, j: (layer, jnp.where(i == 0, j, n_col_tiles - 1), 0))]
    out_specs = [pl.BlockSpec((tm, tn), lambda i, j: (i, j)),
                 pl.BlockSpec((tm, N_AUX), lambda i, j: (i, 0))]
    out_shape = [jax.ShapeDtypeStruct((m, n), BF16), jax.ShapeDtypeStruct((m, N_AUX), F32)]
    operands = [x2, wt_all]
    if norm is not None:
        in_specs += [vec_spec, vec_spec]
        out_specs.append(row_spec)
        out_shape.append(jax.ShapeDtypeStruct((m, d), F32))
        operands += [norm[0].reshape(1, d), norm[1].reshape(1, d)]
        vmem += 2 * tm * d * 4
    return pl.pallas_call(
        functools.partial(_inproj_kernel, tn=tn, norm_first=norm is not None),
        grid=(m // tm, n_col_tiles),
        in_specs=in_specs,
        out_specs=out_specs,
        out_shape=out_shape,
        scratch_shapes=[pltpu.VMEM((tm, d), BF16), pltpu.VMEM((n, d), BF16)],
        compiler_params=_params(2, vmem),
        name="in_projection",
    )(*operands)


def _outproj_kernel(h_ref, y0_ref, y1_ref, y2_ref, y3_ref, w_ref, g_ref, b_ref, o_ref, *, alpha):
    rows = h_ref.shape[0]
    for r in range(0, rows, rows // 2):
        sl = slice(r, r + rows // 2)
        ycat = jnp.concatenate([y0_ref[sl, :], y1_ref[sl, :], y2_ref[sl, :], y3_ref[sl, :]], axis=1)
        y = _dot(ycat, w_ref[...])
        o_ref[sl, :] = _ln_rows(alpha * h_ref[sl, :] + y, g_ref[...], b_ref[...])


def _outproj_streamed(h_hbm, y0_hbm, y1_hbm, y2_hbm, y3_hbm, w_ref, g_ref, b_ref, o_hbm, *, alpha, tm):
    m, d = h_hbm.shape

    def tile(h_ref, y0_ref, y1_ref, y2_ref, y3_ref, o_ref):
        _outproj_kernel(h_ref, y0_ref, y1_ref, y2_ref, y3_ref, w_ref, g_ref, b_ref, o_ref, alpha=alpha)

    deep = pl.Buffered(OUTPROJ_INPUT_BUFFERS)
    yspec = pl.BlockSpec((tm, D_BRANCH), lambda i: (i, 0), pipeline_mode=deep)
    pltpu.emit_pipeline(
        tile,
        grid=(m // tm,),
        in_specs=[pl.BlockSpec((tm, d), lambda i: (i, 0), pipeline_mode=deep), yspec, yspec, yspec, yspec],
        out_specs=[pl.BlockSpec((tm, d), lambda i: (i, 0))],
    )(h_hbm, y0_hbm, y1_hbm, y2_hbm, y3_hbm, o_hbm)


def _outproj(h2, ys, w_bf16, g, b, alpha, tm=1024):
    m, d = h2.shape
    dm = w_bf16.shape[0]
    vmem = (OUTPROJ_INPUT_BUFFERS * (tm * d * 4 + tm * dm * 2) + 2 * (tm * d * 4 + dm * d * 2) + tm * d * 8
            + (4 << 20))
    hbm = pl.BlockSpec(memory_space=pl.ANY)
    whole = pl.BlockSpec(memory_space=pltpu.VMEM)
    return pl.pallas_call(
        functools.partial(_outproj_streamed, alpha=alpha, tm=tm),
        in_specs=[hbm, hbm, hbm, hbm, hbm, whole, whole, whole],
        out_specs=hbm,
        out_shape=jax.ShapeDtypeStruct((m, d), F32),
        compiler_params=pltpu.CompilerParams(vmem_limit_bytes=int(vmem)),
        name="out_projection_layernorm",
    )(h2, *ys, w_bf16, g.reshape(1, d), b.reshape(1, d))


def _head_lane_masks():
    m = np.zeros((N_HEADS, D_BRANCH), np.float32)
    for h in range(N_HEADS):
        m[h, h * HEAD_DIM:(h + 1) * HEAD_DIM] = 1.0
    return m


def _head_block_diag(value=1.0):
    hm = _head_lane_masks()
    return (hm.T @ hm) * value


def _mixer_call(kernel, col_blocks, consts, scratch, n_batch, seq, name, rows=MIX_ROWS, vmem=40 << 20):
    steps = seq // rows
    in_specs = [pl.BlockSpec((rows, w), functools.partial(lambda b, t, c: (b * steps + t, c), c=c))
                for (_, w, c) in col_blocks]
    in_specs += [pl.BlockSpec(a.shape, functools.partial(lambda b, t, nd: (0,) * nd, nd=a.ndim))
                 for a in consts]
    return pl.pallas_call(
        kernel,
        grid=(n_batch, steps),
        in_specs=in_specs,
        out_specs=pl.BlockSpec((rows, D_BRANCH), lambda b, t: (b * steps + t, 0)),
        out_shape=jax.ShapeDtypeStruct((n_batch * seq, D_BRANCH), BF16),
        scratch_shapes=scratch,
        compiler_params=_params(2, vmem),
        name=name,
    )(*[a for (a, _, _) in col_blocks], *consts)


def _ssd_kernel(xbc_ref, z_ref, dt_ref, convw_ref, convb_ref, dtb_ref, aneg_ref, dskip_ref, nw_ref,
                tri_ref, negmask_ref, exph_ref, pmask_ref, smask_ref, shift_ref,
                o_ref, xp_ref, state_ref, u_ref, sb_ref, ea_ref, cm_ref, part_ref):
    C = SSD_CHUNK
    rows = xbc_ref.shape[0]
    n_chunks = rows // C
    halves = D_BRANCH // LANES
    pad = BF16_ROWS
    sls = [slice(c * C, (c + 1) * C) for c in range(n_chunks)]

    @pl.when(pl.program_id(1) == 0)
    def _():
        xp_ref[0:pad, :] = jnp.zeros((pad, SSM_XBC), BF16)
        state_ref[...] = jnp.zeros_like(state_ref)

    xp_ref[pad:pad + rows, :] = xbc_ref[...]

    convw = convw_ref[...]
    pmask = pmask_ref[...]
    exph = exph_ref[...]
    xs, bmb, lcat, dth, ach = [], [], [], [], []
    for c in range(n_chunks):
        r0 = c * C
        ext = xp_ref[r0:r0 + C + pad, :]
        shifted = _dot(shift_ref[...], ext)
        acc = convb_ref[...] + convw[SSM_CONV - 1:SSM_CONV, :] * ext[pad:, :].astype(F32)
        for j in range(SSM_CONV - 1):
            acc = acc + convw[j:j + 1, :] * shifted[j * C:(j + 1) * C, :]
        xbc = _silu(acc)
        xs.append(xbc[:, :D_BRANCH])
        bmb.append(xbc[:, D_BRANCH:D_BRANCH + LANES].astype(BF16))
        cm_ref[c] = xbc[:, D_BRANCH + LANES:].astype(BF16)
    for c in range(n_chunks):
        dt = _softplus(dt_ref[sls[c], :] + dtb_ref[...])
        a_cum = _sel_dot(tri_ref[...], dt * aneg_ref[...])
        a_cum_t = a_cum.T
        col = jnp.concatenate(
            [jnp.broadcast_to(a_cum[:, h:h + 1], (C, C)) for h in range(N_HEADS)], axis=1)
        row = jnp.concatenate([a_cum_t[h:h + 1, :] for h in range(N_HEADS)], axis=1)
        lcat.append(jnp.exp(col - row + negmask_ref[...]))
        dth.append(_dot_sel(dt, exph))
        ach.append(_dot_sel(a_cum, exph))
    heads_per_group = N_HEADS // SSM_GROUPS
    for c in range(n_chunks):
        sg = _dot_nt(cm_ref[c], _pair_stack(bmb[c], 0, pmask))
        s4 = jnp.concatenate([sg[:, (h // heads_per_group) * C:(h // heads_per_group + 1) * C]
                              for h in range(N_HEADS)], axis=1)
        p = (s4 * lcat[c]).astype(BF16)
        xdt = xs[c] * dth[c]
        xdtb = xdt.astype(BF16)
        y = jnp.concatenate(
            [_dot(p[:, g * 2 * C:(g + 1) * 2 * C], _pair_stack(xdtb, g, pmask)) for g in range(halves)], axis=1)
        part_ref[sls[c], :] = y + xs[c] * dskip_ref[...]
        alast = ach[c][C - 1:C, :]
        u_ref[c] = smask_ref[...] * _dot_tn(bmb[c], (xdt * jnp.exp(alast - ach[c])).astype(BF16))
        ea_ref[c] = jnp.exp(ach[c])

    xp_ref[0:pad, :] = xbc_ref[rows - pad:rows, :]

    st = state_ref[...]
    for c in range(n_chunks):
        sb_ref[c] = st.astype(BF16)
        st = st * ea_ref[c, C - 1:C, :] + u_ref[c]
    state_ref[...] = st

    for c in range(n_chunks):
        y = part_ref[sls[c], :] + _dot(cm_ref[c], sb_ref[c]) * ea_ref[c]
        g = y * _silu(z_ref[sls[c], :].astype(F32))
        outs = []
        for grp in range(SSM_GROUPS):
            gg = g[:, grp * LANES:(grp + 1) * LANES]
            ms = jnp.mean(gg * gg, axis=-1, keepdims=True)
            outs.append(gg * lax.rsqrt(ms + RMS_EPS))
        o_ref[sls[c], :] = (jnp.concatenate(outs, axis=1) * nw_ref[...]).astype(o_ref.dtype)


def _ssd_consts():
    C = SSD_CHUNK
    idx = np.arange(C)
    tri = (idx[:, None] >= idx[None, :]).astype(np.float32)
    negmask = np.tile(np.where(tri > 0, 0.0, NEG_BIG).astype(np.float32), (1, N_HEADS))
    exph = np.zeros((LANES, D_BRANCH), np.float32)
    exph[:N_HEADS] = _head_lane_masks()
    heads_per_group = N_HEADS // SSM_GROUPS
    smask = np.zeros((LANES, D_BRANCH), np.float32)
    for h in range(N_HEADS):
        g = h // heads_per_group
        smask[g * SSM_STATE:(g + 1) * SSM_STATE, h * HEAD_DIM:(h + 1) * HEAD_DIM] = 1.0
    shift = np.zeros(((SSM_CONV - 1) * C, C + BF16_ROWS), np.float32)
    for j in range(SSM_CONV - 1):
        shift[j * C + idx, BF16_ROWS - (SSM_CONV - 1) + j + idx] = 1.0
    return (jnp.asarray(tri, BF16), jnp.asarray(negmask), jnp.asarray(exph, BF16),
            _pair_masks(), jnp.asarray(smask), jnp.asarray(shift, BF16))


def _ssd(proj, aux, conv_w, conv_b, dt_bias, a_log, d_skip, norm_w, n_batch, seq):
    lane_pad = LANES - N_HEADS
    dtb = jnp.pad(dt_bias.astype(F32), (0, lane_pad)).reshape(1, LANES)
    aneg = jnp.pad(-jnp.exp(a_log.astype(F32)), (0, lane_pad)).reshape(1, LANES)
    dskip = jnp.repeat(d_skip.astype(F32), HEAD_DIM).reshape(1, D_BRANCH)
    consts = [conv_w.astype(F32), conv_b.reshape(1, SSM_XBC).astype(F32), dtb, aneg, dskip,
              norm_w.reshape(1, D_BRANCH).astype(F32), *_ssd_consts()]
    n_chunks = MIX_ROWS // SSD_CHUNK
    state_shape = (SSM_GROUPS * SSM_STATE, D_BRANCH)
    scratch = [pltpu.VMEM((MIX_ROWS + BF16_ROWS, SSM_XBC), BF16),
               pltpu.VMEM(state_shape, F32),
               pltpu.VMEM((n_chunks,) + state_shape, F32),
               pltpu.VMEM((n_chunks,) + state_shape, BF16),
               pltpu.VMEM((n_chunks, SSD_CHUNK, D_BRANCH), F32),
               pltpu.VMEM((n_chunks, SSD_CHUNK, LANES), BF16),
               pltpu.VMEM((MIX_ROWS, D_BRANCH), F32)]
    cols = [(proj, SSM_XBC, 0), (proj, D_BRANCH, COL_Z), (aux, LANES, AUX_COL_DT)]
    return _mixer_call(_ssd_kernel, cols, consts, scratch, n_batch, seq, "ssd_mixer")


def _seg_mean(x, avg):
    return _dot_sel(x, avg)


def _ret_kernel(q_ref, k_ref, v_ref, g_ref, dcat_ref, qdec_ref, kdec_ref, cg_ref, pmask_ref,
                bd_ref, avg_ref, o_ref, state_ref, u_ref, rb_ref, part_ref):
    C = RET_CHUNK
    rows = q_ref.shape[0]
    n_chunks = rows // C
    halves = D_BRANCH // LANES
    sls = [slice(c * C, (c + 1) * C) for c in range(n_chunks)]

    @pl.when(pl.program_id(1) == 0)
    def _():
        state_ref[...] = jnp.zeros_like(state_ref)

    pmask = pmask_ref[...]
    for c in range(n_chunks):
        qb, kb, vb = q_ref[sls[c], :], k_ref[sls[c], :], v_ref[sls[c], :]
        outs = []
        for g in range(halves):
            scores = _dot_nt(qb[:, g * LANES:(g + 1) * LANES], _pair_stack(kb, g, pmask))
            p = (scores * dcat_ref[:, g * 2 * C:(g + 1) * 2 * C]).astype(BF16)
            outs.append(_dot(p, _pair_stack(vb, g, pmask)))
        part_ref[sls[c], :] = jnp.concatenate(outs, axis=1)
        u_ref[c] = bd_ref[...] * _dot_tn((kb.astype(F32) * kdec_ref[...]).astype(BF16), vb)

    r = state_ref[...]
    for c in range(n_chunks):
        rb_ref[c] = r.astype(BF16)
        r = r * cg_ref[...] + u_ref[c]
    state_ref[...] = r

    for c in range(n_chunks):
        y = part_ref[sls[c], :] + _dot((q_ref[sls[c], :].astype(F32) * qdec_ref[...]).astype(BF16), rb_ref[c])
        mu = _seg_mean(y, avg_ref[...])
        yc = y - mu
        var = _seg_mean(yc * yc, avg_ref[...])
        o_ref[sls[c], :] = (yc * lax.rsqrt(var + LN_EPS) * _silu(g_ref[sls[c], :].astype(F32))).astype(o_ref.dtype)


def _ret_consts():
    C = RET_CHUNK
    log_g = jnp.log(1.0 - 2.0 ** (-5.0 - jnp.arange(N_HEADS, dtype=F32)))
    pos = jnp.arange(C, dtype=F32)
    dist = pos[:, None] - pos[None, :]
    intra = jnp.where(dist >= 0, jnp.exp(log_g[:, None, None] * jnp.maximum(dist, 0.0)), 0.0)
    scale = HEAD_DIM ** -0.5
    dcat = jnp.concatenate([intra[h] for h in range(N_HEADS)], axis=1) * scale
    k_decay = jnp.exp(log_g[:, None] * (C - 1.0 - pos)[None, :]) * scale
    q_decay = jnp.exp(log_g[:, None] * (pos + 1.0)[None, :])
    kdec = jnp.repeat(k_decay.T, HEAD_DIM, axis=1)
    qdec = jnp.repeat(q_decay.T, HEAD_DIM, axis=1)
    cg = jnp.repeat(jnp.exp(log_g * C), HEAD_DIM).reshape(1, D_BRANCH)
    return (dcat, qdec, kdec, cg, _pair_masks(),
            jnp.asarray(_head_block_diag()), jnp.asarray(_head_block_diag(1.0 / HEAD_DIM), BF16))


def _retention(proj, n_batch, seq):
    n_chunks = MIX_ROWS // RET_CHUNK
    scratch = [pltpu.VMEM((D_BRANCH, D_BRANCH), F32),
               pltpu.VMEM((n_chunks, D_BRANCH, D_BRANCH), F32),
               pltpu.VMEM((n_chunks, D_BRANCH, D_BRANCH), BF16),
               pltpu.VMEM((MIX_ROWS, D_BRANCH), F32)]
    cols = [(proj, D_BRANCH, COL_RET + i) for i in range(4)]
    return _mixer_call(_ret_kernel, cols, list(_ret_consts()), scratch, n_batch, seq, "retention_mixer")


def _hgrn_levels():
    return int(math.log2(HGRN_CHUNK))


def _hgrn_kernel(q_ref, i_ref, g_ref, f_ref, lb_ref, nw_ref, mstack_ref, sign_ref, lmask_ref, pmask_ref,
                 bd_ref, avg_ref, o_ref, state_ref, e_ref, u_ref, stb_ref, part_ref):
    C = HGRN_CHUNK
    nl = _hgrn_levels()
    rows = q_ref.shape[0]
    halves = D_BRANCH // LANES

    @pl.when(pl.program_id(1) == 0)
    def _():
        state_ref[...] = jnp.zeros_like(state_ref)

    pmask = pmask_ref[...]
    lb = lb_ref[...]

    def head_scores(qb, kb):
        return jnp.concatenate(
            [_dot_nt(qb[:, g * LANES:(g + 1) * LANES], _pair_stack(kb, g, pmask)) for g in range(halves)], axis=1)

    n_chunks = rows // C
    sls = [slice(c * C, (c + 1) * C) for c in range(n_chunks)]

    k_rows = slice((nl + 2) * C, (nl + 3) * C)
    nm = HGRN_MATMUL_LEVELS
    for c in range(n_chunks):
        f = lb + (1.0 - lb) * _sigmoid(f_ref[sls[c], :])
        x = _sel_dot(mstack_ref[...], jnp.log(f))
        b = x[nm * C:(nm + 1) * C, :]
        e_ref[c, 0:nm * C, :] = jnp.exp(x[0:nm * C, :])
        e_ref[c, nl * C:(nl + 1) * C, :] = jnp.exp(b)
        for j in range(nm, nl):
            s = 1 << j
            bnd = jnp.concatenate([jnp.broadcast_to(b[r + s - 1:r + s, :], (2 * s, D_BRANCH))
                                   for r in range(0, C, 2 * s)], axis=0)
            e_ref[c, j * C:(j + 1) * C, :] = jnp.exp((b - bnd) * sign_ref[j - nm])
        e_ref[c, (nl + 1) * C:(nl + 2) * C, :] = jnp.exp(jnp.broadcast_to(b[C - 1:C, :], (C, D_BRANCH)) - b)
        e_ref[c, k_rows, :] = 1.0 - f

    group = 2
    for c0 in range(0, n_chunks, group):
        cs = range(c0, min(c0 + group, n_chunks))
        att = {c: lmask_ref[nl] * head_scores(q_ref[sls[c], :], e_ref[c, k_rows, :].astype(BF16))
               for c in cs}
        for j in range(nl):
            for c in cs:
                ej = e_ref[c, j * C:(j + 1) * C, :]
                att[c] = att[c] + lmask_ref[j] * head_scores((q_ref[sls[c], :].astype(F32) * ej).astype(BF16),
                                                             (e_ref[c, k_rows, :] * ej).astype(BF16))
        for c in cs:
            attb = att[c].astype(BF16)
            vb = i_ref[sls[c], :]
            part_ref[sls[c], :] = jnp.concatenate(
                [_dot(attb[:, g * 2 * C:(g + 1) * 2 * C], _pair_stack(vb, g, pmask)) for g in range(halves)], axis=1)

    for c in range(n_chunks):
        er = e_ref[c, (nl + 1) * C:(nl + 2) * C, :]
        u_ref[c] = bd_ref[...] * _dot_tn(i_ref[sls[c], :], (e_ref[c, k_rows, :] * er).astype(BF16))

    st = state_ref[...]
    for c in range(n_chunks):
        stb_ref[c] = st.astype(BF16)
        st = st * e_ref[c, (nl + 1) * C - 1:(nl + 1) * C, :] + u_ref[c]
    state_ref[...] = st

    for c in range(n_chunks):
        eb = e_ref[c, nl * C:(nl + 1) * C, :]
        o = part_ref[sls[c], :] + _dot_nt((q_ref[sls[c], :].astype(F32) * eb).astype(BF16), stb_ref[c])
        ms = _seg_mean(o * o, avg_ref[...])
        o_ref[sls[c], :] = (o * lax.rsqrt(ms + RMS_EPS) * nw_ref[...]
                            * _silu(g_ref[sls[c], :].astype(F32))).astype(o_ref.dtype)


def _hgrn_consts():
    C = HGRN_CHUNK
    nl = _hgrn_levels()
    t = np.arange(C)[:, None]
    u = np.arange(C)[None, :]
    blocks, masks, signs = [], [], []
    for j in range(nl):
        s = 1 << j
        bnd = (t // (2 * s)) * (2 * s) + s - 1
        upper = (t % (2 * s)) >= s
        if j < HGRN_MATMUL_LEVELS:
            m = np.where(upper, (u > bnd) & (u <= t), (u > t) & (u <= bnd))
            blocks.append(m.astype(np.float32))
        else:
            signs.append(np.broadcast_to(np.where(upper, 1.0, -1.0), (C, D_BRANCH)).astype(np.float32))
        pair = (t // (2 * s)) == (u // (2 * s))
        masks.append((upper & pair & ((u % (2 * s)) < s)).astype(np.float32))
    blocks.append((u <= t).astype(np.float32))
    masks.append((u == t).astype(np.float32))
    mstack = np.concatenate(blocks, axis=0)
    lmask = np.stack([np.tile(m, (1, N_HEADS)) for m in masks])
    return (jnp.asarray(mstack, BF16), jnp.asarray(np.stack(signs)), jnp.asarray(lmask), _pair_masks(),
            jnp.asarray(_head_block_diag()), jnp.asarray(_head_block_diag(1.0 / HEAD_DIM), BF16))


def _hgrn2(proj, aux, lb, norm_w, n_batch, seq):
    consts = [lb.reshape(1, D_BRANCH).astype(F32), norm_w.reshape(1, D_BRANCH).astype(F32), *_hgrn_consts()]
    n_chunks = HGRN_ROWS // HGRN_CHUNK
    scratch = [pltpu.VMEM((D_BRANCH, D_BRANCH), F32),
               pltpu.VMEM((n_chunks, (_hgrn_levels() + 3) * HGRN_CHUNK, D_BRANCH), F32),
               pltpu.VMEM((n_chunks, D_BRANCH, D_BRANCH), F32),
               pltpu.VMEM((n_chunks, D_BRANCH, D_BRANCH), BF16),
               pltpu.VMEM((HGRN_ROWS, D_BRANCH), F32)]
    cols = [(proj, D_BRANCH, COL_HGRN + i) for i in range(3)] + [(aux, D_BRANCH, AUX_COL_F)]
    return _mixer_call(_hgrn_kernel, cols, consts, scratch, n_batch, seq, "hgrn2_mixer", rows=HGRN_ROWS,
                       vmem=HGRN_VMEM_BYTES)


MOBA_VROWS = 80
MOBA_BODY_POSITIONS = 4


def _moba_kernel(q_ref, k_ref, v_ref, g_ref, ksel_ref, kpos_ref, kblk_ref, qaug_ref, o_ref,
                 kaug_ref, vt_ref, kmean_ref, sel_ref, qt_ref, acc_ref, m_ref, mcur_ref, alpha_ref,
                 s_ref, p_ref):
    S = MOBA_BLOCK
    i = pl.program_id(1)
    nbp = kmean_ref.shape[0]

    @pl.when(i == 0)
    def _():
        kmean_ref[...] = jnp.zeros_like(kmean_ref)

    kb = k_ref[...]
    kmean_ref[pl.ds(i, 1), :] = jnp.mean(kb.astype(F32), axis=0, keepdims=True)
    v_t = v_ref[...].astype(F32).T
    q_t = (q_ref[...].astype(F32) * (HEAD_DIM ** -0.5)).T
    ones_pad = (lax.broadcasted_iota(jnp.int32, (MOBA_VROWS - HEAD_DIM, S), 0) == 0).astype(F32)
    blk = lax.broadcasted_iota(jnp.int32, (nbp, S), 0)
    blk_f = blk.astype(F32)
    past = blk < i
    kmean = kmean_ref[...]
    own_rows = (lax.broadcasted_iota(jnp.int32, (SUBLANES, S), 0) == 0).astype(F32)
    kextra = kpos_ref[...] + i.astype(F32) * kblk_ref[...]
    for h in range(N_HEADS):
        hs = slice(h * HEAD_DIM, (h + 1) * HEAD_DIM)
        kaug_ref[i, h] = (_dot(kb, ksel_ref[h]) + kextra).astype(BF16)
        vt_ref[i, h] = jnp.concatenate([v_t[hs, :], ones_pad], axis=0).astype(BF16)
        qh_t = q_t[hs, :]
        qt_ref[h] = jnp.concatenate([qh_t * LOG2E, qaug_ref[h]], axis=0).astype(BF16)
        km_hi, km_mid, _ = _split3(kmean[:, hs])
        q_hi, q_mid, _ = _split3(qh_t)
        gate = _dot(km_hi, q_hi) + _dot(km_hi, q_mid) + _dot(km_mid, q_hi)
        gate = jnp.where(past, gate, -jnp.inf)
        sel = jnp.zeros((nbp, S), F32)
        for _ in range(MOBA_TOPK):
            best = jnp.max(gate, axis=0, keepdims=True)
            first = jnp.min(jnp.where(gate == best, blk_f, float(nbp)), axis=0, keepdims=True)
            hit = blk_f == first
            sel = jnp.where(hit, 1.0, sel)
            gate = jnp.where(hit, -jnp.inf, gate)
        sel_ref[h, 0:nbp, :] = jnp.where(past, sel, 0.0)
        sel_ref[h, nbp:nbp + SUBLANES, :] = own_rows

    def block_of(t):
        return jnp.where(t == 0, i, jnp.minimum(t - 1, i))

    def sel_row_of(t):
        return jnp.where(t == 0, nbp, jnp.where(t <= i, t - 1, nbp + 1))

    def track_max(h, slot, s_t, chosen):
        m_old = m_ref[h]
        m_new = jnp.where(chosen, jnp.maximum(m_old, jnp.max(s_t, axis=0, keepdims=True)), m_old)
        m_ref[h] = m_new
        mcur_ref[slot, h] = m_new
        alpha_ref[slot, h] = jnp.exp2(m_old - m_new)

    def score_phase(t, slot):
        n, row = block_of(t), sel_row_of(t)
        for h in range(N_HEADS):
            s_t = _dot(kaug_ref[n, h], qt_ref[h])
            s_ref[slot, h] = s_t
            track_max(h, slot, s_t, sel_ref[h, pl.ds(row, 1), :] > 0.0)

    def value_phase(t, slot):
        n, row = block_of(t), sel_row_of(t)
        for h in range(N_HEADS):
            p_ref[h] = jnp.exp2(s_ref[slot, h] - mcur_ref[slot, h]).astype(BF16)
        for h in range(N_HEADS):
            chosen = sel_ref[h, pl.ds(row, 1), :] > 0.0
            upd = _dot(vt_ref[n, h], p_ref[h])
            acc_ref[h] = alpha_ref[slot, h] * acc_ref[h] + jnp.where(chosen, upd, 0.0)

    key_pos = lax.broadcasted_iota(jnp.int32, (S, S), 0)
    qry_pos = lax.broadcasted_iota(jnp.int32, (S, S), 1)
    for h in range(N_HEADS):
        acc_ref[h] = jnp.zeros((MOBA_VROWS, S), F32)
        m_ref[h] = jnp.full((1, S), NEG_BIG, F32)
        s_t = jnp.where(key_pos <= qry_pos, _dot(kaug_ref[i, h], qt_ref[h]), NEG_BIG)
        s_ref[0, h] = s_t
        track_max(h, 0, s_t, True)

    def visit_positions(j, carry):
        t0 = MOBA_BODY_POSITIONS * j
        for r in range(0, MOBA_BODY_POSITIONS, 2):
            score_phase(t0 + r + 1, 1)
            value_phase(t0 + r, 0)
            score_phase(t0 + r + 2, 0)
            value_phase(t0 + r + 1, 1)
        return carry

    lax.fori_loop(0, (i + MOBA_BODY_POSITIONS) // MOBA_BODY_POSITIONS, visit_positions, 0)

    outs = []
    for h in range(N_HEADS):
        acc = acc_ref[h]
        outs.append(acc[:HEAD_DIM, :] / acc[HEAD_DIM:HEAD_DIM + 1, :])
    o_ref[...] = (jnp.concatenate(outs, axis=0).T * _silu(g_ref[...].astype(F32))).astype(o_ref.dtype)


def _moba(proj, n_batch, seq):
    S = MOBA_BLOCK
    assert seq % S == 0
    nb = seq // S
    nbp = -(-nb // SUBLANES) * SUBLANES
    ksel = np.zeros((N_HEADS, D_BRANCH, LANES), np.float32)
    for h in range(N_HEADS):
        ksel[h, h * HEAD_DIM + np.arange(HEAD_DIM), np.arange(HEAD_DIM)] = 1.0
    n_split = 3
    kpos = np.zeros((S, LANES), np.float32)
    kblk = np.zeros((S, LANES), np.float32)
    kpos[:, HEAD_DIM:HEAD_DIM + n_split] = np.arange(S, dtype=np.float32)[:, None]
    kblk[:, HEAD_DIM + n_split:HEAD_DIM + 2 * n_split] = 1.0
    slopes = 2.0 ** (-8.0 * np.arange(1, N_HEADS + 1, dtype=np.float32) / N_HEADS)
    qaug = np.zeros((N_HEADS, LANES - HEAD_DIM, S), np.float32)
    for h in range(N_HEADS):
        for base, coef in ((0, slopes[h] * LOG2E), (n_split, slopes[h] * LOG2E * S)):
            rem = np.float32(coef)
            for r in range(n_split):
                piece = np.float32(np.asarray(rem).astype(BF16))
                qaug[h, base + r, :] = piece
                rem = np.float32(rem - piece)
    consts = [jnp.asarray(ksel, BF16), jnp.asarray(kpos), jnp.asarray(kblk), jnp.asarray(qaug)]
    scratch = [pltpu.VMEM((nb, N_HEADS, S, LANES), BF16),
               pltpu.VMEM((nb, N_HEADS, MOBA_VROWS, S), BF16),
               pltpu.VMEM((nbp, D_BRANCH), F32),
               pltpu.VMEM((N_HEADS, nbp + SUBLANES, S), F32),
               pltpu.VMEM((N_HEADS, LANES, S), BF16),
               pltpu.VMEM((N_HEADS, MOBA_VROWS, S), F32),
               pltpu.VMEM((N_HEADS, 1, S), F32),
               pltpu.VMEM((2, N_HEADS, 1, S), F32),
               pltpu.VMEM((2, N_HEADS, 1, S), F32),
               pltpu.VMEM((2, N_HEADS, S, S), F32),
               pltpu.VMEM((N_HEADS, S, S), BF16)]
    vmem = nb * N_HEADS * S * (LANES + MOBA_VROWS) * 2 + (24 << 20)
    cols = [(proj, D_BRANCH, COL_MOBA + i) for i in range(4)]
    return _mixer_call(_moba_kernel, cols, consts, scratch, n_batch, seq, "moba_mixer", rows=S, vmem=vmem)


def _pack_w_in(w_in):
    depth, d, n_in = w_in.shape
    wt = jnp.transpose(w_in, (2, 0, 1)).reshape(n_in, depth, d // LANES, LANES)
    o = np.cumsum((0, D_BRANCH, SSM_XBC, N_HEADS) + (D_BRANCH,) * 12)
    z, xbc, dt = wt[o[0]:o[1]], wt[o[1]:o[2]], wt[o[2]:o[3]]
    hq, hf, hig, ret_moba = wt[o[3]:o[4]], wt[o[4]:o[5]], wt[o[5]:o[7]], wt[o[7]:]
    pad = jnp.zeros((N_PROJ - n_in,) + wt.shape[1:], wt.dtype)
    packed = jnp.concatenate([xbc, z, hq, hig, ret_moba, hf, dt, pad], axis=0)
    return jnp.transpose(packed, (1, 0, 2, 3)).reshape(depth, N_PROJ * (d // LANES), LANES)


def kernel(x, emb_ln_g, emb_ln_b, w_in, ssd_conv_w, ssd_conv_b, ssd_dt_bias, ssd_a_log, ssd_d, ssd_norm_w,
           hgrn_lb_logits, hgrn_norm_w, w_out, ln_g, ln_b):
    n_batch, seq, d_model = x.shape
    depth = w_in.shape[0]
    alpha = (2.0 * depth) ** 0.25
    assert seq % MIX_ROWS == 0 and seq % MOBA_BLOCK == 0

    lbs = jnp.cumsum(jax.nn.softmax(hgrn_lb_logits.astype(F32), axis=0), axis=0)
    lbs = lbs - lbs[0]

    wt_all = _pack_w_in(w_in.astype(F32))
    h = x.reshape(n_batch * seq, d_model)
    for l in range(depth):
        if l == 0:
            proj, aux, h = _inproj(h, wt_all, l, norm=(emb_ln_g, emb_ln_b))
        else:
            proj, aux = _inproj(h, wt_all, l)
        ys = [_ssd(proj, aux, ssd_conv_w[l], ssd_conv_b[l], ssd_dt_bias[l], ssd_a_log[l], ssd_d[l],
                   ssd_norm_w[l], n_batch, seq),
              _hgrn2(proj, aux, lbs[l], hgrn_norm_w[l], n_batch, seq),
              _retention(proj, n_batch, seq),
              _moba(proj, n_batch, seq)]
        h = _outproj(h, ys, w_out[l].astype(BF16), ln_g[l], ln_b[l], alpha)
    return h.reshape(n_batch, seq, d_model)
```
